```python
import math
import jax, jax.numpy as jnp
from jax import lax
import numpy as np

D_MODEL = 2048
BATCH = 2
SEQ = 4096
DEPTH = 2
DEC_BATCH = 16
DEC_SEQ = 64
PAST_LEN = 2048

CHUNK = 64
N_META = 16
BRANCH_WIDTH = D_MODEL // 2
A_HEADS = 8
A_HEAD_DIM = BRANCH_WIDTH // A_HEADS
IDX_HEADS = 16
IDX_DIM = 64
TOPK_MAX = 256
QBLK = 64
T5_BUCKETS = 32
T5_MAX_DIST = 128
POOL_WINDOWS = (2, 4, 8, 16)
POOL_GROUP = BRANCH_WIDTH // 4
POOL_PAST = 15
RET_HEADS = 8
RET_HEAD_DIM = BRANCH_WIDTH // RET_HEADS
RET_BLOCK = CHUNK
ROPE_BASE = 10000.0
N_BRANCH = 3
D_FF = 11 * D_MODEL // 4
N_EXPERTS = 8
TOP_K = 2
N_DENSE = (DEPTH + 1) // 2
N_MOE = DEPTH // 2
ALPHA = (2 * DEPTH) ** 0.25
BETA = (8 * DEPTH) ** -0.25
LN_EPS = 1e-5
IN_SPLITS = (BRANCH_WIDTH, BRANCH_WIDTH, BRANCH_WIDTH, IDX_HEADS * IDX_DIM, IDX_DIM, IDX_HEADS,
             BRANCH_WIDTH, BRANCH_WIDTH, BRANCH_WIDTH, BRANCH_WIDTH, BRANCH_WIDTH)
IN_WIDTH = sum(IN_SPLITS)

kernel_name = "hybrid_dsa_pool_retention_stream_step"


def layer_norm(x, g, b):
    xf = x.astype(jnp.float32)
    mu = jnp.mean(xf, -1, keepdims=True)
    var = jnp.mean(jnp.square(xf - mu), -1, keepdims=True)
    return ((xf - mu) * lax.rsqrt(var + LN_EPS) * g + b).astype(x.dtype)


def project(x, w):
    B, T = x.shape[0], x.shape[1]
    h = x @ w
    parts, start = [], 0
    for width in IN_SPLITS:
        parts.append(h[..., start:start + width])
        start += width
    qA, kA, vA, qI, kI, wI, uB, qC, kC, vC, gC = parts
    ad = (B, T, A_HEADS, A_HEAD_DIM)
    rd = (B, T, RET_HEADS, RET_HEAD_DIM)
    f32 = jnp.float32
    return (qA.reshape(ad), kA.reshape(ad), vA.reshape(ad), qI.reshape(B, T, IDX_HEADS, IDX_DIM), kI, wI, uB,
            qC.reshape(rd).astype(f32), kC.reshape(rd).astype(f32), vC.reshape(rd).astype(f32), gC)


def t5_bucket(rel):
    half = T5_BUCKETS // 2
    exact = half // 2
    n = jnp.abs(rel)
    large = exact + (jnp.log(jnp.maximum(n, 1).astype(jnp.float32) / exact)
                     / math.log(T5_MAX_DIST / exact) * (half - exact)).astype(jnp.int32)
    large = jnp.minimum(large, half - 1)
    return jnp.where(rel > 0, half, 0) + jnp.where(n < exact, n, large)


def dsa_attend(q, qi, wi, qpos, k, v, ki, kpos, t5_bias, k_sel):
    B, Tq = q.shape[0], q.shape[1]
    nblk = -(-Tq // QBLK)
    pad = nblk * QBLK - Tq

    def blocks(a):
        a = jnp.pad(a, [(0, 0), (0, pad)] + [(0, 0)] * (a.ndim - 2))
        return a.reshape((B, nblk, QBLK) + a.shape[2:])

    qb, qib, wib = blocks(q), blocks(qi), blocks(wi)
    qposb = jnp.pad(qpos, (0, pad), constant_values=2 ** 30).reshape(nblk, QBLK)
    kchunk = kpos // CHUNK
    scale = A_HEAD_DIM ** -0.5

    def one_block(args):
        q_, qi_, wi_, qp_, k_, v_, ki_ = args
        s = jax.nn.relu(jnp.einsum('thd,sd->ths', qi_, ki_).astype(jnp.float32) * IDX_DIM ** -0.5)
        score = jnp.einsum('th,ths->ts', wi_.astype(jnp.float32), s) * IDX_HEADS ** -0.5
        adm = kchunk[None, :] <= (qp_ // CHUNK)[:, None]
        score = jnp.where(adm, score, -jnp.inf)
        _, idx = lax.top_k(score, k_sel)
        valid = jnp.take_along_axis(adm, idx, axis=1)
        kg, vg = k_[idx], v_[idx]
        bias = t5_bias[t5_bucket(kpos[idx] - qp_[:, None])]
        logits = (jnp.einsum('qhd,qkhd->qhk', q_, kg).astype(jnp.float32) * scale
                  + jnp.transpose(bias, (0, 2, 1)).astype(jnp.float32))
        logits = jnp.where(valid[:, None, :], logits, -jnp.inf)
        p = jax.nn.softmax(logits, axis=-1).astype(v_.dtype)
        return jnp.einsum('qhk,qkhd->qhd', p, vg)

    def one_seq(args):
        q_s, qi_s, wi_s, k_s, v_s, ki_s = args
        return lax.map(lambda a: one_block(a + (k_s, v_s, ki_s)), (q_s, qi_s, wi_s, qposb))

    out = lax.map(one_seq, (qb, qib, wib, k, v, ki))
    return out.reshape(B, nblk * QBLK, A_HEADS * A_HEAD_DIM)[:, :Tq]


def pool_mix(u_ext, valid_ext, pool_w, pool_scale):
    B = u_ext.shape[0]
    T = u_ext.shape[1] - POOL_PAST
    uf = u_ext.astype(jnp.float32)
    cs = jnp.cumsum(jnp.pad(uf, ((0, 0), (1, 0), (0, 0))), axis=1)
    cn = jnp.cumsum(jnp.pad(valid_ext.astype(jnp.float32), (1, 0)))
    end = POOL_PAST + 1
    outs = []
    for g, w in enumerate(POOL_WINDOWS):
        sl = slice(g * POOL_GROUP, (g + 1) * POOL_GROUP)
        s = cs[:, end:end + T, sl] - cs[:, end - w:end - w + T, sl]
        n = cn[end:end + T] - cn[end - w:end - w + T]
        outs.append(s / n[None, :, None] - uf[:, POOL_PAST:, sl])
    d = jnp.stack(outs, axis=2)
    y = jnp.einsum('btgc,gce->btge', d, pool_w.astype(jnp.float32)).reshape(B, T, BRANCH_WIDTH)
    return (y * pool_scale).astype(u_ext.dtype)


def rotary(x, pos):
    half = x.shape[-1] // 2
    inv = ROPE_BASE ** (-jnp.arange(half, dtype=jnp.float32) / half)
    ang = pos.astype(jnp.float32)[:, None] * inv[None, :]
    cos, sin = jnp.cos(ang)[None, :, None, :], jnp.sin(ang)[None, :, None, :]
    x1, x2 = x[..., :half], x[..., half:]
    return jnp.concatenate([x1 * cos - x2 * sin, x1 * sin + x2 * cos], axis=-1)


def ret_chunk(S, q, k, v, log_g):
    n = q.shape[1]
    i = jnp.arange(n, dtype=jnp.float32)
    diff = i[:, None] - i[None, :]
    D = jnp.where(diff >= 0, jnp.exp(jnp.maximum(diff, 0.0)[None] * log_g[:, None, None]), 0.0)
    inner = jnp.einsum('bnhk,bmhk->bhnm', q, k) * D[None]
    cross = jnp.exp((i[None, :] + 1.0) * log_g[:, None])
    o = (jnp.einsum('bhnm,bmhv->bnhv', inner, v)
         + jnp.einsum('bnhk,bhkv->bnhv', q, S) * cross.T[None, :, :, None])
    kdec = jnp.exp((n - 1.0 - i)[None, :] * log_g[:, None])
    S_new = (jnp.exp(n * log_g)[None, :, None, None] * S
             + jnp.einsum('bmhk,bmhv->bhkv', k * kdec.T[None, :, :, None], v))
    return S_new, o


def retention_prompt(q, k, v, log_g):
    B, T = q.shape[0], q.shape[1]
    pad = (-T) % RET_BLOCK
    nb = (T + pad) // RET_BLOCK

    def to_blocks(a):
        a = jnp.pad(a, ((0, 0), (pad, 0), (0, 0), (0, 0)))
        return jnp.transpose(a.reshape(B, nb, RET_BLOCK, RET_HEADS, RET_HEAD_DIM), (1, 0, 2, 3, 4))

    S0 = jnp.zeros((B, RET_HEADS, RET_HEAD_DIM, RET_HEAD_DIM), jnp.float32)
    S, o = lax.scan(lambda S_, xs: ret_chunk(S_, xs[0], xs[1], xs[2], log_g), S0,
                    (to_blocks(q), to_blocks(k), to_blocks(v)))
    o = jnp.transpose(o, (1, 0, 2, 3, 4)).reshape(B, nb * RET_BLOCK, RET_HEADS, RET_HEAD_DIM)[:, pad:]
    return o, S


def retention_output(o, gate):
    B, T = o.shape[0], o.shape[1]
    mu = jnp.mean(o, -1, keepdims=True)
    var = jnp.mean(jnp.square(o - mu), -1, keepdims=True)
    on = ((o - mu) * lax.rsqrt(var + LN_EPS)).reshape(B, T, BRANCH_WIDTH)
    return (jax.nn.silu(gate.astype(jnp.float32)) * on).astype(gate.dtype)


def merge_branches(x, oA, oB, oC, w_branch, w_gate, b_gate, w_out):
    B, T = x.shape[0], x.shape[1]
    ob = jnp.stack([oA, oB.astype(oA.dtype), oC.astype(oA.dtype)], axis=2)
    u = jnp.einsum('btnw,nwd->btnd', ob, w_branch)
    g = jax.nn.sigmoid((x @ w_gate + b_gate).reshape(B, T, N_BRANCH, D_MODEL))
    return jnp.sum(g * u, axis=2) @ w_out


def swiglu(x, wg, wu, wd):
    return (jax.nn.silu(x @ wg) * (x @ wu)) @ wd


def moe(x, w_r, b_r, wg, wu, wd):
    logits = (x @ w_r).astype(jnp.float32) + b_r
    top_v, top_i = lax.top_k(logits, TOP_K)
    probs = jax.nn.softmax(top_v, axis=-1)
    gates = jnp.sum(probs[..., None] * jax.nn.one_hot(top_i, N_EXPERTS, dtype=jnp.float32), axis=-2)
    out = jnp.zeros_like(x)
    for e in range(N_EXPERTS):
        out = out + gates[..., e:e + 1].astype(x.dtype) * swiglu(x, wg[e], wu[e], wd[e])
    return out


def channel_mixer(l, x, ffn_w_gate, ffn_w_up, ffn_w_down, moe_w_router, moe_b_router, moe_w_gate, moe_w_up, moe_w_down):
    i = l // 2
    if l % 2 == 0:
        return swiglu(x, ffn_w_gate[i], ffn_w_up[i], ffn_w_down[i])
    return moe(x, moe_w_router[i], moe_b_router[i], moe_w_gate[i], moe_w_up[i], moe_w_down[i])


def setup_inputs(seed: int = 0) -> dict:
    key = jax.random.key(seed)
    ks = jax.random.split(key, 32)
    f32 = jnp.float32

    def nrm(i, shape, scale):
        return jax.random.normal(ks[i], shape, f32) * scale

    D = D_MODEL
    return {
        "x_prompt": nrm(0, (BATCH, SEQ, D), 1.0),
        "x_sample": nrm(1, (DEC_BATCH, DEC_SEQ, D), 1.0),
        "cache_k": nrm(2, (DEPTH, DEC_BATCH, PAST_LEN, A_HEADS, A_HEAD_DIM), 1.0),
        "cache_v": nrm(3, (DEPTH, DEC_BATCH, PAST_LEN, A_HEADS, A_HEAD_DIM), 1.0),
        "cache_ki": nrm(4, (DEPTH, DEC_BATCH, PAST_LEN, IDX_DIM), 1.0),
        "cache_pool": nrm(5, (DEPTH, DEC_BATCH, POOL_PAST, BRANCH_WIDTH), 1.0),
        "state_ret": nrm(6, (DEPTH, DEC_BATCH, RET_HEADS, RET_HEAD_DIM, RET_HEAD_DIM), 4.0),
        "meta_tokens": nrm(7, (N_META, D), 1.0),
        "ln_in_g": 1.0 + nrm(8, (D,), 0.02),
        "ln_in_b": nrm(9, (D,), 0.02),
        "w_in": nrm(10, (DEPTH, D, IN_WIDTH), D ** -0.5),
        "t5_bias": nrm(11, (T5_BUCKETS, A_HEADS), 0.5),
        "pool_w": nrm(12, (DEPTH, len(POOL_WINDOWS), POOL_GROUP, POOL_GROUP), POOL_GROUP ** -0.5),
        "pool_scale": 1.0 + nrm(13, (DEPTH, BRANCH_WIDTH), 0.1),
        "w_branch": nrm(14, (DEPTH, N_BRANCH, BRANCH_WIDTH, D), BRANCH_WIDTH ** -0.5),
        "w_gate": nrm(15, (DEPTH, D, N_BRANCH * D), D ** -0.5),
        "b_gate": nrm(16, (DEPTH, N_BRANCH * D), 0.02),
        "w_out": nrm(17, (DEPTH, D, D), BETA * D ** -0.5),
        "ln1_g": 1.0 + nrm(18, (DEPTH, D), 0.02),
        "ln1_b": nrm(19, (DEPTH, D), 0.02),
        "ln2_g": 1.0 + nrm(20, (DEPTH, D), 0.02),
        "ln2_b": nrm(21, (DEPTH, D), 0.02),
        "ffn_w_gate": nrm(22, (N_DENSE, D, D_FF), D ** -0.5),
        "ffn_w_up": nrm(23, (N_DENSE, D, D_FF), D ** -0.5),
        "ffn_w_down": nrm(24, (N_DENSE, D_FF, D), BETA * D_FF ** -0.5),
        "moe_w_router": nrm(25, (N_MOE, D, N_EXPERTS), D ** -0.5),
        "moe_b_router": nrm(26, (N_MOE, N_EXPERTS), 0.01),
        "moe_w_gate": nrm(27, (N_MOE, N_EXPERTS, D, D_FF), D ** -0.5),
        "moe_w_up": nrm(28, (N_MOE, N_EXPERTS, D, D_FF), D ** -0.5),
        "moe_w_down": nrm(29, (N_MOE, N_EXPERTS, D_FF, D), BETA * D_FF ** -0.5),
    }


def reference(x_prompt, x_sample, cache_k, cache_v, cache_ki, cache_pool, state_ret,
              meta_tokens, ln_in_g, ln_in_b, w_in, t5_bias, pool_w, pool_scale, w_branch, w_gate, b_gate,
              w_out, ln1_g, ln1_b, ln2_g, ln2_b, ffn_w_gate, ffn_w_up, ffn_w_down,
              moe_w_router, moe_b_router, moe_w_gate, moe_w_up, moe_w_down):
    f32 = jnp.float32
    B, S_len = x_prompt.shape[0], x_prompt.shape[1]
    Bs, Ts = x_sample.shape[0], x_sample.shape[1]
    past = cache_k.shape[2]
    meta = jnp.broadcast_to(meta_tokens.astype(x_prompt.dtype)[None], (B, N_META, D_MODEL))
    xp = layer_norm(jnp.concatenate([meta, x_prompt], axis=1), ln_in_g, ln_in_b)
    xs = layer_norm(x_sample, ln_in_g, ln_in_b)
    T = S_len + N_META
    pos_p = jnp.arange(T, dtype=jnp.int32) - N_META
    pos_s = jnp.arange(past + Ts, dtype=jnp.int32)
    pos_s_new = pos_s[past:]
    ksel_p = min(TOPK_MAX, S_len // 4)
    ksel_s = min(TOPK_MAX, (past + Ts) // 4)
    valid_p = jnp.concatenate([jnp.zeros((POOL_PAST,), f32), jnp.ones((T,), f32)])
    valid_s = jnp.ones((POOL_PAST + Ts,), f32)
    log_g = jnp.log(1.0 - 2.0 ** (-5.0 - jnp.arange(RET_HEADS, dtype=f32)))
    rscale = RET_HEAD_DIM ** -0.5

    kp_l, vp_l, kip_l, poolp_l, retp_l = [], [], [], [], []
    ks_l, vs_l, kis_l, pools_l, rets_l = [], [], [], [], []
    for l in range(DEPTH):
        qA, kA, vA, qI, kI, wI, uB, qC, kC, vC, gC = project(xp, w_in[l])
        oA = dsa_attend(qA, qI, wI, pos_p, kA, vA, kI, pos_p, t5_bias, ksel_p)
        oB = pool_mix(jnp.pad(uB, ((0, 0), (POOL_PAST, 0), (0, 0))), valid_p, pool_w[l], pool_scale[l])
        o_ret, S_p = retention_prompt(rotary(qC, pos_p), rotary(kC, pos_p) * rscale, vC, log_g)
        oC = retention_output(o_ret, gC)
        kp_l.append(kA); vp_l.append(vA); kip_l.append(kI)
        poolp_l.append(uB[:, -POOL_PAST:]); retp_l.append(S_p)
        xp = layer_norm(ALPHA * xp + merge_branches(xp, oA, oB, oC, w_branch[l], w_gate[l], b_gate[l], w_out[l]),
                        ln1_g[l], ln1_b[l])
        xp = layer_norm(ALPHA * xp + channel_mixer(l, xp, ffn_w_gate, ffn_w_up, ffn_w_down, moe_w_router,
                                                   moe_b_router, moe_w_gate, moe_w_up, moe_w_down),
                        ln2_g[l], ln2_b[l])

        qA, kA, vA, qI, kI, wI, uB, qC, kC, vC, gC = project(xs, w_in[l])
        k_all = jnp.concatenate([cache_k[l].astype(kA.dtype), kA], axis=1)
        v_all = jnp.concatenate([cache_v[l].astype(vA.dtype), vA], axis=1)
        ki_all = jnp.concatenate([cache_ki[l].astype(kI.dtype), kI], axis=1)
        oA = dsa_attend(qA, qI, wI, pos_s_new, k_all, v_all, ki_all, pos_s, t5_bias, ksel_s)
        u_ext = jnp.concatenate([cache_pool[l].astype(uB.dtype), uB], axis=1)
        oB = pool_mix(u_ext, valid_s, pool_w[l], pool_scale[l])
        S_s, o_ret = ret_chunk(state_ret[l].astype(f32), rotary(qC, pos_s_new),
                               rotary(kC, pos_s_new) * rscale, vC, log_g)
        oC = retention_output(o_ret, gC)
        ks_l.append(kA); vs_l.append(vA); kis_l.append(kI)
        pools_l.append(u_ext[:, -POOL_PAST:]); rets_l.append(S_s)
        xs = layer_norm(ALPHA * xs + merge_branches(xs, oA, oB, oC, w_branch[l], w_gate[l], b_gate[l], w_out[l]),
                        ln1_g[l], ln1_b[l])
        xs = layer_norm(ALPHA * xs + channel_mixer(l, xs, ffn_w_gate, ffn_w_up, ffn_w_down, moe_w_router,
                                                   moe_b_router, moe_w_gate, moe_w_up, moe_w_down),
                        ln2_g[l], ln2_b[l])

    y_prompt = xp[:, N_META:]
    y_sample = xs
    return (y_prompt, y_sample,
            jnp.stack(kp_l), jnp.stack(vp_l), jnp.stack(kip_l), jnp.stack(poolp_l), jnp.stack(retp_l),
            jnp.stack(ks_l), jnp.stack(vs_l), jnp.stack(kis_l), jnp.stack(pools_l), jnp.stack(rets_l))
```

```python
import functools
import math

import jax
import jax.numpy as jnp
from jax import lax
from jax.experimental import pallas as pl
from jax.experimental.pallas import tpu as pltpu

F32 = jnp.float32
BF16 = jnp.bfloat16

D_MODEL = 2048
DEPTH = 2
CHUNK = 64
N_META = 16
BRANCH_WIDTH = D_MODEL // 2
A_HEADS = 8
A_HEAD_DIM = BRANCH_WIDTH // A_HEADS
IDX_HEADS = 16
IDX_DIM = 64
TOPK_MAX = 256
T5_BUCKETS = 32
T5_MAX_DIST = 128
POOL_WINDOWS = (2, 4, 8, 16)
POOL_GROUP = BRANCH_WIDTH // 4
POOL_PAST = 15
RET_HEADS = 8
RET_HEAD_DIM = BRANCH_WIDTH // RET_HEADS
ROPE_BASE = 10000.0
N_BRANCH = 3
D_FF = 11 * D_MODEL // 4
N_EXPERTS = 8
ALPHA = (2 * DEPTH) ** 0.25
LN_EPS = 1e-5
IN_SPLITS = (BRANCH_WIDTH, BRANCH_WIDTH, BRANCH_WIDTH, IDX_HEADS * IDX_DIM, IDX_DIM, IDX_HEADS,
             BRANCH_WIDTH, BRANCH_WIDTH, BRANCH_WIDTH, BRANCH_WIDTH, BRANCH_WIDTH)
IN_WIDTH = sum(IN_SPLITS)
TAIL_OFF = 4 * BRANCH_WIDTH + IDX_DIM + IDX_HEADS

LANES = 128
HALO = 16
NEAR = 3 * CHUNK
KEY_TILE = 256
VMEM_LIMIT = 56 * 1024 * 1024
TM = 512
INT_MIN = -2 ** 31
NEG = -1e30


def _cparams(sem):
    return pltpu.CompilerParams(dimension_semantics=sem, vmem_limit_bytes=VMEM_LIMIT)


def _dot(a, b):
    return jnp.dot(a, b, preferred_element_type=F32)


def _dot_nt(a, b):
    return lax.dot_general(a, b, (((1,), (1,)), ((), ())), preferred_element_type=F32)


def _dot_tn(a, b):
    return lax.dot_general(a, b, (((0,), (0,)), ((), ())), preferred_element_type=F32)


def _ln_kernel(x_ref, g_ref, b_ref, o32_ref, o16_ref):
    x = x_ref[...]
    mu = jnp.mean(x, axis=-1, keepdims=True)
    xc = x - mu
    var = jnp.mean(xc * xc, axis=-1, keepdims=True)
    y = xc * lax.rsqrt(var + LN_EPS) * g_ref[...] + b_ref[...]
    o32_ref[...] = y
    o16_ref[...] = y.astype(BF16)


def _ln_res_kernel(x_ref, y_ref, g_ref, b_ref, o32_ref, o16_ref):
    x = ALPHA * x_ref[...] + y_ref[...]
    mu = jnp.mean(x, axis=-1, keepdims=True)
    xc = x - mu
    var = jnp.mean(xc * xc, axis=-1, keepdims=True)
    y = xc * lax.rsqrt(var + LN_EPS) * g_ref[...] + b_ref[...]
    o32_ref[...] = y
    o16_ref[...] = y.astype(BF16)


def _layer_norm(x, y, g, b, tm=256):
    m, d = x.shape
    row = pl.BlockSpec((tm, d), lambda i: (i, 0))
    vec = pl.BlockSpec((1, d), lambda i: (0, 0))
    args = (x,) if y is None else (x, y)
    return pl.pallas_call(
        _ln_kernel if y is None else _ln_res_kernel,
        out_shape=(jax.ShapeDtypeStruct((m, d), F32), jax.ShapeDtypeStruct((m, d), BF16)),
        grid=(m // tm,),
        in_specs=[row] * len(args) + [vec, vec],
        out_specs=(row, row),
        compiler_params=_cparams(("parallel",)),
        name="layer_norm",
    )(*args, g.reshape(1, d), b.reshape(1, d))


def _mm_kernel(x_ref, w_ref, o_ref, wb_ref):
    @pl.when(pl.program_id(1) == 0)
    def _():
        wb_ref[...] = w_ref[...].astype(BF16)

    o_ref[...] = _dot(x_ref[...], wb_ref[...]).astype(o_ref.dtype)


def _mm_acc_kernel(x_ref, w_ref, a_ref, o_ref, wb_ref):
    @pl.when(pl.program_id(1) == 0)
    def _():
        wb_ref[...] = w_ref[...].astype(BF16)

    o_ref[...] = a_ref[...] + _dot(x_ref[...], wb_ref[...])


def _matmul(x, w, lead, col0, n_tiles, tn, acc=None, name="matmul"):
    m, k = x.shape
    nl = len(lead)
    w_spec = pl.BlockSpec((None,) * nl + (k, tn), lambda j, i: tuple(lead) + (0, j + col0))
    x_spec = pl.BlockSpec((TM, k), lambda j, i: (i, 0))
    o_spec = pl.BlockSpec((TM, tn), lambda j, i: (i, j))
    common = dict(
        out_shape=jax.ShapeDtypeStruct((m, n_tiles * tn), F32),
        grid=(n_tiles, m // TM),
        out_specs=o_spec,
        scratch_shapes=[pltpu.VMEM((k, tn), BF16)],
        compiler_params=_cparams(("arbitrary", "arbitrary")),
        name=name,
    )
    if acc is None:
        return pl.pallas_call(_mm_kernel, in_specs=[x_spec, w_spec], **common)(x, w)
    return pl.pallas_call(_mm_acc_kernel, in_specs=[x_spec, w_spec, o_spec],
                          input_output_aliases={2: 0}, **common)(x, w, acc)


def _gate_up_kernel(x_ref, wg_ref, wu_ref, o_ref, wgb_ref, wub_ref):
    @pl.when(pl.program_id(1) == 0)
    def _():
        wgb_ref[...] = wg_ref[...].astype(BF16)
        wub_ref[...] = wu_ref[...].astype(BF16)

    x = x_ref[...]
    g = _dot(x, wgb_ref[...])
    u = _dot(x, wub_ref[...])
    o_ref[...] = (g * jax.nn.sigmoid(g) * u).astype(o_ref.dtype)


def _gate_up_scaled_kernel(x_ref, wg_ref, wu_ref, s_ref, o_ref, wgb_ref, wub_ref):
    @pl.when(pl.program_id(1) == 0)
    def _():
        wgb_ref[...] = wg_ref[...].astype(BF16)
        wub_ref[...] = wu_ref[...].astype(BF16)

    x = x_ref[...]
    g = _dot(x, wgb_ref[...])
    u = _dot(x, wub_ref[...])
    o_ref[...] = (g * jax.nn.sigmoid(g) * u * s_ref[...]).astype(o_ref.dtype)


def _gate_up(x, wg, wu, lead, row_scale=None, tn=512):
    m, k = x.shape
    n = wg.shape[-1]
    nl = len(lead)
    w_spec = pl.BlockSpec((None,) * nl + (k, tn), lambda j, i: tuple(lead) + (0, j))
    x_spec = pl.BlockSpec((TM, k), lambda j, i: (i, 0))
    common = dict(
        out_shape=jax.ShapeDtypeStruct((m, n), BF16),
        grid=(n // tn, m // TM),
        out_specs=pl.BlockSpec((TM, tn), lambda j, i: (i, j)),
        scratch_shapes=[pltpu.VMEM((k, tn), BF16), pltpu.VMEM((k, tn), BF16)],
        compiler_params=_cparams(("arbitrary", "arbitrary")),
        name="gate_up",
    )
    if row_scale is None:
        return pl.pallas_call(_gate_up_kernel, in_specs=[x_spec, w_spec, w_spec], **common)(x, wg, wu)
    s_spec = pl.BlockSpec((TM, 1), lambda j, i: (i, 0))
    return pl.pallas_call(_gate_up_scaled_kernel, in_specs=[x_spec, w_spec, w_spec, s_spec],
                          **common)(x, wg, wu, row_scale)


def _merge_kernel(x_ref, oa_ref, ob_ref, oc_ref, wg0_ref, wg1_ref, wg2_ref, wb0_ref, wb1_ref, wb2_ref,
                  bg0_ref, bg1_ref, bg2_ref, o_ref, wgb_ref, wbb_ref):
    wg_refs = (wg0_ref, wg1_ref, wg2_ref)
    wb_refs = (wb0_ref, wb1_ref, wb2_ref)

    @pl.when(pl.program_id(1) == 0)
    def _():
        for n in range(N_BRANCH):
            wgb_ref[n] = wg_refs[n][...].astype(BF16)
            wbb_ref[n] = wb_refs[n][...].astype(BF16)

    x = x_ref[...]
    acc = None
    for n, (o_in, bg) in enumerate(zip((oa_ref, ob_ref, oc_ref), (bg0_ref, bg1_ref, bg2_ref))):
        gate = jax.nn.sigmoid(_dot(x, wgb_ref[n]) + bg[...])
        term = gate * _dot(o_in[...], wbb_ref[n])
        acc = term if acc is None else acc + term
    o_ref[...] = acc.astype(o_ref.dtype)


def _merge(l, x, oa, ob, oc, w_branch, w_gate, b_gate, tn=256):
    m, d = x.shape
    w = oa.shape[1]
    nt = d // tn
    x_spec = pl.BlockSpec((TM, d), lambda j, i: (i, 0))
    o_in_spec = pl.BlockSpec((TM, w), lambda j, i: (i, 0))
    wg_specs = [pl.BlockSpec((None, d, tn), lambda j, i, n=n: (l, 0, n * nt + j)) for n in range(N_BRANCH)]
    wb_specs = [pl.BlockSpec((None, None, w, tn), lambda j, i, n=n: (l, n, 0, j)) for n in range(N_BRANCH)]
    bg_specs = [pl.BlockSpec((None, 1, tn), lambda j, i, n=n: (l, 0, n * nt + j)) for n in range(N_BRANCH)]
    return pl.pallas_call(
        _merge_kernel,
        out_shape=jax.ShapeDtypeStruct((m, d), BF16),
        grid=(nt, m // TM),
        in_specs=[x_spec, o_in_spec, o_in_spec, o_in_spec] + wg_specs + wb_specs + bg_specs,
        out_specs=pl.BlockSpec((TM, tn), lambda j, i: (i, j)),
        scratch_shapes=[pltpu.VMEM((N_BRANCH, d, tn), BF16), pltpu.VMEM((N_BRANCH, w, tn), BF16)],
        compiler_params=_cparams(("arbitrary", "arbitrary")),
        name="merge",
    )(x, oa, ob, oc, w_gate, w_gate, w_gate, w_branch, w_branch, w_branch,
      b_gate.reshape(DEPTH, 1, N_BRANCH * d), b_gate.reshape(DEPTH, 1, N_BRANCH * d),
      b_gate.reshape(DEPTH, 1, N_BRANCH * d))


def _split_bf16(a):
    hi = a.astype(BF16)
    lo = (a - hi.astype(F32)).astype(BF16)
    return hi, lo


def _router_kernel(x_ref, w_ref, b_ref, o_ref):
    xh, xl = _split_bf16(x_ref[...])
    wh, wl = _split_bf16(w_ref[...])
    logits = _dot(xh, wh) + (_dot(xh, wl) + _dot(xl, wh)) + b_ref[...]
    lane = lax.broadcasted_iota(jnp.int32, logits.shape, 1)
    logits = jnp.where(lane < N_EXPERTS, logits, -jnp.inf)
    m1 = jnp.max(logits, axis=-1, keepdims=True)
    i1 = jnp.min(jnp.where(logits == m1, lane, LANES), axis=-1, keepdims=True)
    rest = jnp.where(lane == i1, -jnp.inf, logits)
    m2 = jnp.max(rest, axis=-1, keepdims=True)
    i2 = jnp.min(jnp.where(rest == m2, lane, LANES), axis=-1, keepdims=True)
    e = jnp.exp(m2 - m1)
    p1 = 1.0 / (1.0 + e)
    p2 = e / (1.0 + e)
    o_ref[...] = jnp.where(lane == i1, p1, 0.0) + jnp.where(lane == i2, p2, 0.0)


def _router(x32, w_r, b_r, tm=256):
    m, d = x32.shape
    w_pad = jnp.pad(w_r, ((0, 0), (0, LANES - N_EXPERTS)))
    b_pad = jnp.pad(b_r, (0, LANES - N_EXPERTS)).reshape(1, LANES)
    return pl.pallas_call(
        _router_kernel,
        out_shape=jax.ShapeDtypeStruct((m, LANES), F32),
        grid=(m // tm,),
        in_specs=[pl.BlockSpec((tm, d), lambda i: (i, 0)), pl.BlockSpec((d, LANES), lambda i: (0, 0)),
                  pl.BlockSpec((1, LANES), lambda i: (0, 0))],
        out_specs=pl.BlockSpec((tm, LANES), lambda i: (i, 0)),
        compiler_params=_cparams(("parallel",)),
        name="router",
    )(x32, w_pad, b_pad)


def _pool_kernel(u_ref, halo_ref, w_ref, sc_ref, o_ref, ext_ref, *, tb, first_valid):
    row0 = pl.program_id(1) * tb
    r_cur = row0 + lax.broadcasted_iota(jnp.int32, (tb, 1), 0)
    r_halo = row0 - HALO + lax.broadcasted_iota(jnp.int32, (HALO, 1), 0)
    u = jnp.where(r_cur >= first_valid, u_ref[...], 0.0)
    ext_ref[0:HALO, :] = jnp.where(r_halo >= first_valid, halo_ref[...], 0.0)
    ext_ref[HALO:, :] = u
    seen = (r_cur - first_valid + 1).astype(F32)
    for g, win in enumerate(POOL_WINDOWS):
        c0, c1 = g * POOL_GROUP, (g + 1) * POOL_GROUP
        s = u[:, c0:c1]
        for back in range(1, win):
            s = s + ext_ref[HALO - back:HALO - back + tb, c0:c1]
        cnt = jnp.clip(seen, 1.0, float(win))
        diff = s / cnt - u[:, c0:c1]
        y = _dot(diff.astype(BF16), w_ref[g].astype(BF16))
        o_ref[:, c0:c1] = (y * sc_ref[:, c0:c1]).astype(o_ref.dtype)


def _pool(u2d, col_block, halo2d, halo_col_block, pool_w_l, pool_scale_l, *, batch, rows, tb, row0, halo_map,
          first_valid):
    nb = rows // tb
    base = row0 // tb
    return pl.pallas_call(
        functools.partial(_pool_kernel, tb=tb, first_valid=first_valid),
        out_shape=jax.ShapeDtypeStruct((batch * rows, BRANCH_WIDTH), BF16),
        grid=(batch, nb),
        in_specs=[
            pl.BlockSpec((tb, BRANCH_WIDTH), lambda b, i: (base + b * nb + i, col_block)),
            pl.BlockSpec((HALO, BRANCH_WIDTH), lambda b, i: (halo_map(b, i), halo_col_block)),
            pl.BlockSpec((len(POOL_WINDOWS), POOL_GROUP, POOL_GROUP), lambda b, i: (0, 0, 0)),
            pl.BlockSpec((1, BRANCH_WIDTH), lambda b, i: (0, 0)),
        ],
        out_specs=pl.BlockSpec((tb, BRANCH_WIDTH), lambda b, i: (b * nb + i, 0)),
        scratch_shapes=[pltpu.VMEM((HALO + tb, BRANCH_WIDTH), F32)],
        compiler_params=_cparams(("parallel", "arbitrary")),
        name="pool",
    )(u2d, halo2d, pool_w_l, pool_scale_l.reshape(1, BRANCH_WIDTH))


def _ret_kernel(q_ref, k_ref, v_ref, g_ref, cos_ref, sin_ref, dmat_ref, cross_ref, kdec_ref, gn_ref, s0_ref,
                o_ref, s_out_ref, s_scr, *, first_valid):
    i = pl.program_id(1)

    @pl.when(i == 0)
    def _():
        s_scr[...] = s0_ref[...]

    rows = i * CHUNK + lax.broadcasted_iota(jnp.int32, (CHUNK, 1), 0)
    valid = rows >= first_valid
    cos = cos_ref[...]
    sin = sin_ref[...]
    rscale = RET_HEAD_DIM ** -0.5
    half = RET_HEAD_DIM // 2
    for h in range(RET_HEADS):
        sl = slice(h * RET_HEAD_DIM, (h + 1) * RET_HEAD_DIM)
        q = q_ref[:, sl]
        k = k_ref[:, sl]
        v = jnp.where(valid, v_ref[:, sl], 0.0).astype(BF16)
        qr = (q * cos + pltpu.roll(q, half, 1) * sin).astype(BF16)
        kr = jnp.where(valid, (k * cos + pltpu.roll(k, half, 1) * sin) * rscale, 0.0)
        state = s_scr[h]
        inner = _dot_nt(qr, kr.astype(BF16)) * dmat_ref[h]
        o = _dot(inner.astype(BF16), v) + _dot(qr, state.astype(BF16)) * cross_ref[h]
        s_scr[h] = gn_ref[h] * state + _dot_tn((kr * kdec_ref[h]).astype(BF16), v)
        mu = jnp.mean(o, axis=-1, keepdims=True)
        oc = o - mu
        var = jnp.mean(oc * oc, axis=-1, keepdims=True)
        gate = g_ref[:, sl]
        o_ref[:, sl] = (gate * jax.nn.sigmoid(gate) * (oc * lax.rsqrt(var + LN_EPS))).astype(o_ref.dtype)

    @pl.when(i == pl.num_programs(1) - 1)
    def _():
        s_out_ref[...] = s_scr[...]


def _retention(p3, s0, cos, sin, tabs, *, batch, rows, row0, first_valid):
    nb = rows // CHUNK
    base = row0 // CHUNK
    dmat, cross, kdec, gn = tabs

    def col(c):
        return pl.BlockSpec((CHUNK, BRANCH_WIDTH), lambda b, i: (base + b * nb + i, c))

    tab_rows = pl.BlockSpec((CHUNK, RET_HEAD_DIM), lambda b, i: (i, 0))

    def full(a):
        return pl.BlockSpec(a.shape, lambda b, i: (0,) * a.ndim)

    state_spec = pl.BlockSpec((None, RET_HEADS, RET_HEAD_DIM, RET_HEAD_DIM), lambda b, i: (b, 0, 0, 0))
    return pl.pallas_call(
        functools.partial(_ret_kernel, first_valid=first_valid),
        out_shape=(jax.ShapeDtypeStruct((batch * rows, BRANCH_WIDTH), BF16),
                   jax.ShapeDtypeStruct((batch, RET_HEADS, RET_HEAD_DIM, RET_HEAD_DIM), F32)),
        grid=(batch, nb),
        in_specs=[col(1), col(2), col(3), col(4), tab_rows, tab_rows, full(dmat), full(cross), full(kdec),
                  full(gn), state_spec],
        out_specs=(pl.BlockSpec((CHUNK, BRANCH_WIDTH), lambda b, i: (b * nb + i, 0)), state_spec),
        scratch_shapes=[pltpu.VMEM((RET_HEADS, RET_HEAD_DIM, RET_HEAD_DIM), F32)],
        compiler_params=_cparams(("parallel", "arbitrary")),
        name="retention",
    )(p3, p3, p3, p3, cos, sin, dmat, cross, kdec, gn, s0)


def _sortable(x):
    bits = pltpu.bitcast(x + 0.0, jnp.int32)
    return bits ^ ((bits >> 31) & 0x7FFFFFFF)


def _dsa_kernel(cfar_ref, q_ref, qi_ref, wi_ref, k_ref, v_ref, ki_ref, bn_ref, o_ref, key_ref, logit_ref, *,
                n_keys, first_real, qb0, skip_below, k_sel):
    i = pl.program_id(1)
    n_far = n_keys // KEY_TILE

    @pl.when(i < skip_below)
    def _():
        o_ref[...] = jnp.zeros(o_ref.shape, o_ref.dtype)

    @pl.when(i >= skip_below)
    def _():
        win0 = pl.multiple_of((i + qb0 - 2) * CHUNK, CHUNK)
        near = pl.ds(win0, NEAR)
        wi = wi_ref[0]

        def index_scores(ki):
            acc = None
            for h in range(IDX_HEADS):
                term = wi[:, h:h + 1] * jnp.maximum(_dot_nt(qi_ref[0, h], ki), 0.0)
                acc = term if acc is None else acc + term
            return _sortable(acc)

        def far_keys(c, carry):
            c0 = pl.multiple_of(c * KEY_TILE, KEY_TILE)
            col = c0 + lax.broadcasted_iota(jnp.int32, (CHUNK, KEY_TILE), 1)
            adm = jnp.logical_and(col >= first_real, col < win0)
            key_ref[c] = jnp.where(adm, index_scores(ki_ref[0, pl.ds(c0, KEY_TILE), :]), INT_MIN)
            return carry

        lax.fori_loop(0, n_far, far_keys, 0)
        col_n = win0 + lax.broadcasted_iota(jnp.int32, (CHUNK, NEAR), 1)
        key_n = jnp.where(col_n >= first_real, index_scores(ki_ref[0, near, :]), INT_MIN)

        def count_ge(c):
            far = jnp.sum(jnp.where(key_ref[...] >= c[None], 1.0, 0.0), axis=0)
            return (jnp.sum(far, axis=-1, keepdims=True)
                    + jnp.sum(jnp.where(key_n >= c, 1.0, 0.0), axis=-1, keepdims=True))

        zero = jnp.zeros((CHUNK, 1), jnp.int32)
        thr0 = jnp.where(count_ge(zero) >= k_sel, zero, INT_MIN)

        def bit_step(it, thr):
            cand = thr + jnp.left_shift(jnp.int32(1), jnp.int32(30) - it)
            return jnp.where(count_ge(cand) >= k_sel, cand, thr)

        thr = lax.fori_loop(0, 31, bit_step, thr0)
        thr = jnp.maximum(thr, INT_MIN + 1)
        sel_n = key_n >= thr

        scale = A_HEAD_DIM ** -0.5
        for h in range(A_HEADS):
            sl = slice(h * A_HEAD_DIM, (h + 1) * A_HEAD_DIM)
            qh = q_ref[:, sl].astype(BF16)
            c_far = cfar_ref[h]
            ln = jnp.where(sel_n, _dot_nt(qh, k_ref[0, near, sl]) * scale + bn_ref[h], NEG)

            def far_logits(c, m):
                c0 = pl.multiple_of(c * KEY_TILE, KEY_TILE)
                lf = _dot_nt(qh, k_ref[0, pl.ds(c0, KEY_TILE), sl]) * scale + c_far
                lf = jnp.where(key_ref[c] >= thr, lf, NEG)
                logit_ref[c] = lf
                return jnp.maximum(m, jnp.max(lf, axis=-1, keepdims=True))

            m = lax.fori_loop(0, n_far, far_logits, jnp.max(ln, axis=-1, keepdims=True))
            pn = jnp.exp(ln - m)

            def far_pv(c, carry):
                den, acc = carry
                c0 = pl.multiple_of(c * KEY_TILE, KEY_TILE)
                pf = jnp.exp(logit_ref[c] - m)
                return (den + jnp.sum(pf, axis=-1, keepdims=True),
                        acc + _dot(pf.astype(BF16), v_ref[0, pl.ds(c0, KEY_TILE), sl]))

            den, acc = lax.fori_loop(0, n_far, far_pv, (jnp.sum(pn, axis=-1, keepdims=True),
                                                        _dot(pn.astype(BF16), v_ref[0, near, sl])))
            o_ref[:, sl] = (acc / den).astype(o_ref.dtype)


def _dsa(p1, qi, wi, k_all, v_all, ki_all, bias_near, bias_far, *, batch, rows, row0, first_real, qb0,
         skip_below, k_sel):
    nb = rows // CHUNK
    base = row0 // CHUNK
    n_keys = k_all.shape[1]
    return pl.pallas_call(
        functools.partial(_dsa_kernel, n_keys=n_keys, first_real=first_real, qb0=qb0, skip_below=skip_below,
                          k_sel=k_sel),
        out_shape=jax.ShapeDtypeStruct((batch * rows, BRANCH_WIDTH), BF16),
        grid=(batch, nb),
        in_specs=[
            pl.BlockSpec(memory_space=pltpu.SMEM),
            pl.BlockSpec((CHUNK, BRANCH_WIDTH), lambda b, i: (base + b * nb + i, 0)),
            pl.BlockSpec((1, IDX_HEADS, CHUNK, IDX_DIM), lambda b, i: (base + b * nb + i, 0, 0, 0)),
            pl.BlockSpec((1, CHUNK, IDX_HEADS), lambda b, i: (base + b * nb + i, 0, 0)),
            pl.BlockSpec((1, n_keys, BRANCH_WIDTH), lambda b, i: (b, 0, 0)),
            pl.BlockSpec((1, n_keys, BRANCH_WIDTH), lambda b, i: (b, 0, 0)),
            pl.BlockSpec((1, n_keys, IDX_DIM), lambda b, i: (b, 0, 0)),
            pl.BlockSpec((A_HEADS, CHUNK, NEAR), lambda b, i: (0, 0, 0)),
        ],
        out_specs=pl.BlockSpec((CHUNK, BRANCH_WIDTH), lambda b, i: (b * nb + i, 0)),
        scratch_shapes=[pltpu.VMEM((n_keys // KEY_TILE, CHUNK, KEY_TILE), jnp.int32),
                        pltpu.VMEM((n_keys // KEY_TILE, CHUNK, KEY_TILE), F32)],
        compiler_params=_cparams(("parallel", "arbitrary")),
        name="dsa",
    )(bias_far, p1, qi, wi, k_all, v_all, ki_all, bias_near)


def _t5_bucket(rel):
    half = T5_BUCKETS // 2
    exact = half // 2
    n = jnp.abs(rel)
    large = exact + (jnp.log(jnp.maximum(n, 1).astype(F32) / exact)
                     / math.log(T5_MAX_DIST / exact) * (half - exact)).astype(jnp.int32)
    large = jnp.minimum(large, half - 1)
    return jnp.where(rel > 0, half, 0) + jnp.where(n < exact, n, large)


def _bias_tables(t5_bias):
    rel = (jnp.arange(NEAR, dtype=jnp.int32)[None, :] - 2 * CHUNK) - jnp.arange(CHUNK, dtype=jnp.int32)[:, None]
    near = jnp.transpose(t5_bias[_t5_bucket(rel)], (2, 0, 1)).astype(F32)
    far = t5_bias[_t5_bucket(jnp.int32(-2 * CHUNK - 1))].astype(F32)
    return near, far


def _rope_tables(pos):
    half = RET_HEAD_DIM // 2
    inv = ROPE_BASE ** (-jnp.arange(half, dtype=F32) / half)
    ang = pos.astype(F32)[:, None] * inv[None, :]
    cos, sin = jnp.cos(ang), jnp.sin(ang)
    return jnp.concatenate([cos, cos], axis=-1), jnp.concatenate([-sin, sin], axis=-1)


def _decay_tables():
    n = CHUNK
    log_g = jnp.log(1.0 - 2.0 ** (-5.0 - jnp.arange(RET_HEADS, dtype=F32)))
    i = jnp.arange(n, dtype=F32)
    diff = i[:, None] - i[None, :]
    dmat = jnp.where(diff >= 0, jnp.exp(jnp.maximum(diff, 0.0)[None] * log_g[:, None, None]), 0.0)
    cross = jnp.exp((i[None, :] + 1.0) * log_g[:, None])
    kdec = jnp.exp((n - 1.0 - i)[None, :] * log_g[:, None])
    gn = jnp.exp(n * log_g)
    wide = (RET_HEADS, n, RET_HEAD_DIM)
    return (dmat, jnp.broadcast_to(cross[:, :, None], wide), jnp.broadcast_to(kdec[:, :, None], wide),
            jnp.broadcast_to(gn[:, None, None], (RET_HEADS, 1, RET_HEAD_DIM)))


def kernel(x_prompt, x_sample, cache_k, cache_v, cache_ki, cache_pool, state_ret, meta_tokens, ln_in_g, ln_in_b, w_in, t5_bias, pool_w, pool_scale, w_branch, w_gate, b_gate, w_out, ln1_g, ln1_b, ln2_g, ln2_b, ffn_w_gate, ffn_w_up, ffn_w_down, moe_w_router, moe_b_router, moe_w_gate, moe_w_up, moe_w_down):
    bp, seq, d = x_prompt.shape
    bs, ts, _ = x_sample.shape
    past = cache_k.shape[2]
    t_real = seq + N_META
    tp = -(-(t_real + 2 * CHUNK) // KEY_TILE) * KEY_TILE
    front = tp - t_real
    assert front % CHUNK == CHUNK - N_META and ts == CHUNK and past % CHUNK == 0
    mp, ms = bp * tp, bs * ts
    m = mp + ms
    assert mp % TM == 0 and ms % TM == 0
    ksel_p = min(TOPK_MAX, seq // 4)
    ksel_s = min(TOPK_MAX, (past + ts) // 4)
    s_front = (-(past + ts)) % KEY_TILE
    n_keys_s = s_front + past + ts

    meta = jnp.broadcast_to(meta_tokens.astype(F32)[None], (bp, N_META, d))
    x_rows = jnp.concatenate(
        [jnp.concatenate([jnp.zeros((bp, front, d), F32), meta, x_prompt], axis=1).reshape(mp, d),
         x_sample.reshape(ms, d)], axis=0)
    x32, xb = _layer_norm(x_rows, None, ln_in_g, ln_in_b)

    bias_near, bias_far = _bias_tables(t5_bias)
    cos_p, sin_p = _rope_tables(jnp.arange(tp, dtype=jnp.int32) - (front + N_META))
    cos_s, sin_s = _rope_tables(past + jnp.arange(ts, dtype=jnp.int32))
    decay = _decay_tables()
    zero_state = jnp.zeros((bp, RET_HEADS, RET_HEAD_DIM, RET_HEAD_DIM), F32)
    w_tail = w_in[:, :, TAIL_OFF:]

    outs = {name: [] for name in ("kp", "vp", "kip", "poolp", "retp", "ks", "vs", "kis", "pools", "rets")}
    bw = BRANCH_WIDTH
    for l in range(DEPTH):
        p1 = _matmul(xb, w_in, (l,), 0, 4, bw, name="proj_attn")
        p2 = _matmul(xb, w_in, (l,), 4 * bw // LANES, 1, LANES, name="proj_idx")
        p3 = _matmul(xb, w_tail, (l,), 0, 5, bw, name="proj_tail")

        k_new, v_new, ki_new = p1[:, bw:2 * bw], p1[:, 2 * bw:3 * bw], p2[:, :IDX_DIM]
        qi = jnp.transpose(p1[:, 3 * bw:].reshape(m // CHUNK, CHUNK, IDX_HEADS, IDX_DIM), (0, 2, 1, 3)).astype(BF16)
        wi = p2[:, IDX_DIM:IDX_DIM + IDX_HEADS].reshape(m // CHUNK, CHUNK, IDX_HEADS)

        def sample_keys(cache, new, width):
            rows = [jnp.zeros((bs, s_front, width), F32), cache.reshape(bs, past, width),
                    new[mp:].reshape(bs, ts, width)]
            return jnp.concatenate(rows, axis=1).astype(BF16)

        oa_p = _dsa(p1, qi, wi, k_new[:mp].reshape(bp, tp, bw).astype(BF16),
                    v_new[:mp].reshape(bp, tp, bw).astype(BF16), ki_new[:mp].reshape(bp, tp, IDX_DIM).astype(BF16),
                    bias_near, bias_far, batch=bp, rows=tp, row0=0, first_real=front, qb0=0,
                    skip_below=front // CHUNK, k_sel=ksel_p)
        oa_s = _dsa(p1, qi, wi, sample_keys(cache_k[l], k_new, bw), sample_keys(cache_v[l], v_new, bw),
                    sample_keys(cache_ki[l], ki_new, IDX_DIM), bias_near, bias_far, batch=bs, rows=ts, row0=mp,
                    first_real=s_front, qb0=(n_keys_s - ts) // CHUNK, skip_below=0, k_sel=ksel_s)

        tb_p = 256
        ob_p = _pool(p3, 0, p3, 0, pool_w[l], pool_scale[l], batch=bp, rows=tp, tb=tb_p, row0=0,
                     halo_map=lambda b, i: jnp.maximum((b * tp + i * tb_p) // HALO - 1, 0), first_valid=front)
        pool_hist = jnp.pad(cache_pool[l], ((0, 0), (HALO - POOL_PAST, 0), (0, 0))).reshape(bs * HALO, bw)
        ob_s = _pool(p3, 0, pool_hist, 0, pool_w[l], pool_scale[l], batch=bs, rows=ts, tb=ts, row0=mp,
                     halo_map=lambda b, i: b, first_valid=-POOL_PAST)

        oc_p, ret_p = _retention(p3, zero_state, cos_p, sin_p, decay, batch=bp, rows=tp, row0=0, first_valid=front)
        oc_s, ret_s = _retention(p3, state_ret[l].astype(F32), cos_s, sin_s, decay, batch=bs, rows=ts, row0=mp,
                                 first_valid=0)

        merged = _merge(l, xb, jnp.concatenate([oa_p, oa_s]), jnp.concatenate([ob_p, ob_s]),
                        jnp.concatenate([oc_p, oc_s]), w_branch, w_gate, b_gate)
        y = _matmul(merged, w_out, (l,), 0, d // 512, 512, name="w_out")
        x32, xb = _layer_norm(x32, y, ln1_g[l], ln1_b[l])

        if l % 2 == 0:
            h = _gate_up(xb, ffn_w_gate, ffn_w_up, (l // 2,))
            f = _matmul(h, ffn_w_down, (l // 2,), 0, d // 512, 512, name="ffn_down")
        else:
            gates = _router(x32, moe_w_router[l // 2], moe_b_router[l // 2])
            f = None
            for e in range(N_EXPERTS):
                h = _gate_up(xb, moe_w_gate, moe_w_up, (l // 2, e), row_scale=gates[:, e:e + 1])
                f = _matmul(h, moe_w_down, (l // 2, e), 0, d // 512, 512, acc=f, name="moe_down")
        x32, xb = _layer_norm(x32, f, ln2_g[l], ln2_b[l])

        def prompt_rows(a, width):
            return a[:mp].reshape(bp, tp, width)[:, front:]

        u = p3[:, :bw]
        outs["kp"].append(prompt_rows(k_new, bw).reshape(bp, t_real, A_HEADS, A_HEAD_DIM))
        outs["vp"].append(prompt_rows(v_new, bw).reshape(bp, t_real, A_HEADS, A_HEAD_DIM))
        outs["kip"].append(prompt_rows(ki_new, IDX_DIM))
        outs["poolp"].append(prompt_rows(u, bw)[:, -POOL_PAST:])
        outs["retp"].append(ret_p)
        outs["ks"].append(k_new[mp:].reshape(bs, ts, A_HEADS, A_HEAD_DIM))
        outs["vs"].append(v_new[mp:].reshape(bs, ts, A_HEADS, A_HEAD_DIM))
        outs["kis"].append(ki_new[mp:].reshape(bs, ts, IDX_DIM))
        outs["pools"].append(u[mp:].reshape(bs, ts, bw)[:, -POOL_PAST:])
        outs["rets"].append(ret_s)

    y_prompt = x32[:mp].reshape(bp, tp, d)[:, front + N_META:]
    y_sample = x32[mp:].reshape(bs, ts, d)
    return (y_prompt, y_sample) + tuple(
        jnp.stack(outs[name]) for name in ("kp", "vp", "kip", "poolp", "retp", "ks", "vs", "kis", "pools", "rets"))
```

```python
import functools
import math

import jax
import jax.numpy as jnp
from jax import lax
from jax.experimental import pallas as pl
from jax.experimental.pallas import tpu as pltpu

F32 = jnp.float32
BF16 = jnp.bfloat16

D_MODEL = 2048
DEPTH = 2
CHUNK = 64
N_META = 16
BRANCH_WIDTH = D_MODEL // 2
A_HEADS = 8
A_HEAD_DIM = BRANCH_WIDTH // A_HEADS
IDX_HEADS = 16
IDX_DIM = 64
TOPK_MAX = 256
T5_BUCKETS = 32
T5_MAX_DIST = 128
POOL_WINDOWS = (2, 4, 8, 16)
POOL_GROUP = BRANCH_WIDTH // 4
POOL_PAST = 15
RET_HEADS = 8
RET_HEAD_DIM = BRANCH_WIDTH // RET_HEADS
ROPE_BASE = 10000.0
N_BRANCH = 3
D_FF = 11 * D_MODEL // 4
N_EXPERTS = 8
ALPHA = (2 * DEPTH) ** 0.25
LN_EPS = 1e-5
IN_SPLITS = (BRANCH_WIDTH, BRANCH_WIDTH, BRANCH_WIDTH, IDX_HEADS * IDX_DIM, IDX_DIM, IDX_HEADS,
             BRANCH_WIDTH, BRANCH_WIDTH, BRANCH_WIDTH, BRANCH_WIDTH, BRANCH_WIDTH)
IN_WIDTH = sum(IN_SPLITS)
TAIL_OFF = 4 * BRANCH_WIDTH + IDX_DIM + IDX_HEADS

LANES = 128
SUBLANES = 8
HALO = 16
QT = 2 * CHUNK
WN = QT + 2 * CHUNK
KEY_TILE = 256
VMEM_LIMIT = 56 * 1024 * 1024
TM = 512
INT_MIN = -2 ** 31
NEG = -1e30
M_INIT = -1e20
TINY = 1e-30


def _cparams(sem):
    return pltpu.CompilerParams(dimension_semantics=sem, vmem_limit_bytes=VMEM_LIMIT)


def _dot(a, b):
    return jnp.dot(a, b, preferred_element_type=F32)


def _dot_nt(a, b):
    return lax.dot_general(a, b, (((1,), (1,)), ((), ())), preferred_element_type=F32)


def _dot_tn(a, b):
    return lax.dot_general(a, b, (((0,), (0,)), ((), ())), preferred_element_type=F32)


def _ln_kernel(x_ref, g_ref, b_ref, o32_ref, o16_ref):
    x = x_ref[...]
    mu = jnp.mean(x, axis=-1, keepdims=True)
    xc = x - mu
    var = jnp.mean(xc * xc, axis=-1, keepdims=True)
    y = xc * lax.rsqrt(var + LN_EPS) * g_ref[...] + b_ref[...]
    o32_ref[...] = y
    o16_ref[...] = y.astype(BF16)


def _ln_res_kernel(x_ref, y_ref, g_ref, b_ref, o32_ref, o16_ref):
    x = ALPHA * x_ref[...] + y_ref[...]
    mu = jnp.mean(x, axis=-1, keepdims=True)
    xc = x - mu
    var = jnp.mean(xc * xc, axis=-1, keepdims=True)
    y = xc * lax.rsqrt(var + LN_EPS) * g_ref[...] + b_ref[...]
    o32_ref[...] = y
    o16_ref[...] = y.astype(BF16)


def _layer_norm(x, y, g, b, tm=256):
    m, d = x.shape
    row = pl.BlockSpec((tm, d), lambda i: (i, 0))
    vec = pl.BlockSpec((1, d), lambda i: (0, 0))
    args = (x,) if y is None else (x, y)
    return pl.pallas_call(
        _ln_kernel if y is None else _ln_res_kernel,
        out_shape=(jax.ShapeDtypeStruct((m, d), F32), jax.ShapeDtypeStruct((m, d), BF16)),
        grid=(m // tm,),
        in_specs=[row] * len(args) + [vec, vec],
        out_specs=(row, row),
        compiler_params=_cparams(("parallel",)),
        name="layer_norm",
    )(*args, g.reshape(1, d), b.reshape(1, d))


def _mm_kernel(x_ref, w_ref, o_ref, wb_ref):
    @pl.when(pl.program_id(1) == 0)
    def _():
        wb_ref[...] = w_ref[...].astype(BF16)

    o_ref[...] = _dot(x_ref[...], wb_ref[...]).astype(o_ref.dtype)


def _mm_acc_kernel(x_ref, w_ref, a_ref, o_ref, wb_ref):
    @pl.when(pl.program_id(1) == 0)
    def _():
        wb_ref[...] = w_ref[...].astype(BF16)

    o_ref[...] = a_ref[...] + _dot(x_ref[...], wb_ref[...])


def _matmul(x, w, lead, col0, n_tiles, tn, acc=None, name="matmul"):
    m, k = x.shape
    nl = len(lead)
    w_spec = pl.BlockSpec((None,) * nl + (k, tn), lambda j, i: tuple(lead) + (0, j + col0))
    x_spec = pl.BlockSpec((TM, k), lambda j, i: (i, 0))
    o_spec = pl.BlockSpec((TM, tn), lambda j, i: (i, j))
    common = dict(
        out_shape=jax.ShapeDtypeStruct((m, n_tiles * tn), F32),
        grid=(n_tiles, m // TM),
        out_specs=o_spec,
        scratch_shapes=[pltpu.VMEM((k, tn), BF16)],
        compiler_params=_cparams(("arbitrary", "arbitrary")),
        name=name,
    )
    if acc is None:
        return pl.pallas_call(_mm_kernel, in_specs=[x_spec, w_spec], **common)(x, w)
    return pl.pallas_call(_mm_acc_kernel, in_specs=[x_spec, w_spec, o_spec],
                          input_output_aliases={2: 0}, **common)(x, w, acc)


def _gate_up_kernel(x_ref, wg_ref, wu_ref, o_ref, wgb_ref, wub_ref):
    @pl.when(pl.program_id(1) == 0)
    def _():
        wgb_ref[...] = wg_ref[...].astype(BF16)
        wub_ref[...] = wu_ref[...].astype(BF16)

    x = x_ref[...]
    g = _dot(x, wgb_ref[...])
    u = _dot(x, wub_ref[...])
    o_ref[...] = (g * jax.nn.sigmoid(g) * u).astype(o_ref.dtype)


def _gate_up_scaled_kernel(x_ref, wg_ref, wu_ref, s_ref, o_ref, wgb_ref, wub_ref):
    @pl.when(pl.program_id(1) == 0)
    def _():
        wgb_ref[...] = wg_ref[...].astype(BF16)
        wub_ref[...] = wu_ref[...].astype(BF16)

    x = x_ref[...]
    g = _dot(x, wgb_ref[...])
    u = _dot(x, wub_ref[...])
    o_ref[...] = (g * jax.nn.sigmoid(g) * u * s_ref[...]).astype(o_ref.dtype)


def _gate_up(x, wg, wu, lead, row_scale=None, tn=512):
    m, k = x.shape
    n = wg.shape[-1]
    nl = len(lead)
    w_spec = pl.BlockSpec((None,) * nl + (k, tn), lambda j, i: tuple(lead) + (0, j))
    x_spec = pl.BlockSpec((TM, k), lambda j, i: (i, 0))
    common = dict(
        out_shape=jax.ShapeDtypeStruct((m, n), BF16),
        grid=(n // tn, m // TM),
        out_specs=pl.BlockSpec((TM, tn), lambda j, i: (i, j)),
        scratch_shapes=[pltpu.VMEM((k, tn), BF16), pltpu.VMEM((k, tn), BF16)],
        compiler_params=_cparams(("arbitrary", "arbitrary")),
        name="gate_up",
    )
    if row_scale is None:
        return pl.pallas_call(_gate_up_kernel, in_specs=[x_spec, w_spec, w_spec], **common)(x, wg, wu)
    s_spec = pl.BlockSpec((TM, 1), lambda j, i: (i, 0))
    return pl.pallas_call(_gate_up_scaled_kernel, in_specs=[x_spec, w_spec, w_spec, s_spec],
                          **common)(x, wg, wu, row_scale)


def _merge_kernel(x_ref, oa_ref, ob_ref, oc_ref, wg0_ref, wg1_ref, wg2_ref, wb0_ref, wb1_ref, wb2_ref,
                  bg0_ref, bg1_ref, bg2_ref, o_ref, wgb_ref, wbb_ref):
    wg_refs = (wg0_ref, wg1_ref, wg2_ref)
    wb_refs = (wb0_ref, wb1_ref, wb2_ref)

    @pl.when(pl.program_id(1) == 0)
    def _():
        for n in range(N_BRANCH):
            wgb_ref[n] = wg_refs[n][...].astype(BF16)
            wbb_ref[n] = wb_refs[n][...].astype(BF16)

    x = x_ref[...]
    acc = None
    for n, (o_in, bg) in enumerate(zip((oa_ref, ob_ref, oc_ref), (bg0_ref, bg1_ref, bg2_ref))):
        gate = jax.nn.sigmoid(_dot(x, wgb_ref[n]) + bg[...])
        term = gate * _dot(o_in[...], wbb_ref[n])
        acc = term if acc is None else acc + term
    o_ref[...] = acc.astype(o_ref.dtype)


def _merge(l, x, oa, ob, oc, w_branch, w_gate, b_gate, tn=256):
    m, d = x.shape
    w = oa.shape[1]
    nt = d // tn
    x_spec = pl.BlockSpec((TM, d), lambda j, i: (i, 0))
    o_in_spec = pl.BlockSpec((TM, w), lambda j, i: (i, 0))
    wg_specs = [pl.BlockSpec((None, d, tn), lambda j, i, n=n: (l, 0, n * nt + j)) for n in range(N_BRANCH)]
    wb_specs = [pl.BlockSpec((None, None, w, tn), lambda j, i, n=n: (l, n, 0, j)) for n in range(N_BRANCH)]
    bg_specs = [pl.BlockSpec((None, 1, tn), lambda j, i, n=n: (l, 0, n * nt + j)) for n in range(N_BRANCH)]
    return pl.pallas_call(
        _merge_kernel,
        out_shape=jax.ShapeDtypeStruct((m, d), BF16),
        grid=(nt, m // TM),
        in_specs=[x_spec, o_in_spec, o_in_spec, o_in_spec] + wg_specs + wb_specs + bg_specs,
        out_specs=pl.BlockSpec((TM, tn), lambda j, i: (i, j)),
        scratch_shapes=[pltpu.VMEM((N_BRANCH, d, tn), BF16), pltpu.VMEM((N_BRANCH, w, tn), BF16)],
        compiler_params=_cparams(("arbitrary", "arbitrary")),
        name="merge",
    )(x, oa, ob, oc, w_gate, w_gate, w_gate, w_branch, w_branch, w_branch,
      b_gate.reshape(DEPTH, 1, N_BRANCH * d), b_gate.reshape(DEPTH, 1, N_BRANCH * d),
      b_gate.reshape(DEPTH, 1, N_BRANCH * d))


def _split_bf16(a):
    hi = a.astype(BF16)
    lo = (a - hi.astype(F32)).astype(BF16)
    return hi, lo


def _router_kernel(x_ref, w_ref, b_ref, o_ref):
    xh, xl = _split_bf16(x_ref[...])
    wh, wl = _split_bf16(w_ref[...])
    logits = _dot(xh, wh) + (_dot(xh, wl) + _dot(xl, wh)) + b_ref[...]
    lane = lax.broadcasted_iota(jnp.int32, logits.shape, 1)
    logits = jnp.where(lane < N_EXPERTS, logits, -jnp.inf)
    m1 = jnp.max(logits, axis=-1, keepdims=True)
    i1 = jnp.min(jnp.where(logits == m1, lane, LANES), axis=-1, keepdims=True)
    rest = jnp.where(lane == i1, -jnp.inf, logits)
    m2 = jnp.max(rest, axis=-1, keepdims=True)
    i2 = jnp.min(jnp.where(rest == m2, lane, LANES), axis=-1, keepdims=True)
    e = jnp.exp(m2 - m1)
    p1 = 1.0 / (1.0 + e)
    p2 = e / (1.0 + e)
    o_ref[...] = jnp.where(lane == i1, p1, 0.0) + jnp.where(lane == i2, p2, 0.0)


def _router(x32, w_r, b_r, tm=256):
    m, d = x32.shape
    w_pad = jnp.pad(w_r, ((0, 0), (0, LANES - N_EXPERTS)))
    b_pad = jnp.pad(b_r, (0, LANES - N_EXPERTS)).reshape(1, LANES)
    return pl.pallas_call(
        _router_kernel,
        out_shape=jax.ShapeDtypeStruct((m, LANES), F32),
        grid=(m // tm,),
        in_specs=[pl.BlockSpec((tm, d), lambda i: (i, 0)), pl.BlockSpec((d, LANES), lambda i: (0, 0)),
                  pl.BlockSpec((1, LANES), lambda i: (0, 0))],
        out_specs=pl.BlockSpec((tm, LANES), lambda i: (i, 0)),
        compiler_params=_cparams(("parallel",)),
        name="router",
    )(x32, w_pad, b_pad)


def _pool_kernel(u_ref, halo_ref, w_ref, sc_ref, o_ref, ext_ref, *, tb, first_valid):
    row0 = pl.program_id(1) * tb
    r_cur = row0 + lax.broadcasted_iota(jnp.int32, (tb, 1), 0)
    r_halo = row0 - HALO + lax.broadcasted_iota(jnp.int32, (HALO, 1), 0)
    u = jnp.where(r_cur >= first_valid, u_ref[...], 0.0)
    ext_ref[0:HALO, :] = jnp.where(r_halo >= first_valid, halo_ref[...], 0.0)
    ext_ref[HALO:, :] = u
    seen = (r_cur - first_valid + 1).astype(F32)
    for g, win in enumerate(POOL_WINDOWS):
        c0, c1 = g * POOL_GROUP, (g + 1) * POOL_GROUP
        s = u[:, c0:c1]
        for back in range(1, win):
            s = s + ext_ref[HALO - back:HALO - back + tb, c0:c1]
        cnt = jnp.clip(seen, 1.0, float(win))
        diff = s / cnt - u[:, c0:c1]
        y = _dot(diff.astype(BF16), w_ref[g].astype(BF16))
        o_ref[:, c0:c1] = (y * sc_ref[:, c0:c1]).astype(o_ref.dtype)


def _pool(u2d, col_block, halo2d, halo_col_block, pool_w_l, pool_scale_l, *, batch, rows, tb, row0, halo_map,
          first_valid):
    nb = rows // tb
    base = row0 // tb
    return pl.pallas_call(
        functools.partial(_pool_kernel, tb=tb, first_valid=first_valid),
        out_shape=jax.ShapeDtypeStruct((batch * rows, BRANCH_WIDTH), BF16),
        grid=(batch, nb),
        in_specs=[
            pl.BlockSpec((tb, BRANCH_WIDTH), lambda b, i: (base + b * nb + i, col_block)),
            pl.BlockSpec((HALO, BRANCH_WIDTH), lambda b, i: (halo_map(b, i), halo_col_block)),
            pl.BlockSpec((len(POOL_WINDOWS), POOL_GROUP, POOL_GROUP), lambda b, i: (0, 0, 0)),
            pl.BlockSpec((1, BRANCH_WIDTH), lambda b, i: (0, 0)),
        ],
        out_specs=pl.BlockSpec((tb, BRANCH_WIDTH), lambda b, i: (b * nb + i, 0)),
        scratch_shapes=[pltpu.VMEM((HALO + tb, BRANCH_WIDTH), F32)],
        compiler_params=_cparams(("parallel", "arbitrary")),
        name="pool",
    )(u2d, halo2d, pool_w_l, pool_scale_l.reshape(1, BRANCH_WIDTH))


def _ret_kernel(q_ref, k_ref, v_ref, g_ref, cos_ref, sin_ref, dmat_ref, cross_ref, kdec_ref, gn_ref, s0_ref,
                o_ref, s_out_ref, s_scr, *, first_valid):
    i = pl.program_id(1)

    @pl.when(i == 0)
    def _():
        s_scr[...] = s0_ref[...]

    rows = i * CHUNK + lax.broadcasted_iota(jnp.int32, (CHUNK, 1), 0)
    valid = rows >= first_valid
    cos = cos_ref[...]
    sin = sin_ref[...]
    rscale = RET_HEAD_DIM ** -0.5
    half = RET_HEAD_DIM // 2
    for h in range(RET_HEADS):
        sl = slice(h * RET_HEAD_DIM, (h + 1) * RET_HEAD_DIM)
        q = q_ref[:, sl]
        k = k_ref[:, sl]
        v = jnp.where(valid, v_ref[:, sl], 0.0).astype(BF16)
        qr = (q * cos + pltpu.roll(q, half, 1) * sin).astype(BF16)
        kr = jnp.where(valid, (k * cos + pltpu.roll(k, half, 1) * sin) * rscale, 0.0)
        state = s_scr[h]
        inner = _dot_nt(qr, kr.astype(BF16)) * dmat_ref[h]
        o = _dot(inner.astype(BF16), v) + _dot(qr, state.astype(BF16)) * cross_ref[h]
        s_scr[h] = gn_ref[h] * state + _dot_tn((kr * kdec_ref[h]).astype(BF16), v)
        mu = jnp.mean(o, axis=-1, keepdims=True)
        oc = o - mu
        var = jnp.mean(oc * oc, axis=-1, keepdims=True)
        gate = g_ref[:, sl]
        o_ref[:, sl] = (gate * jax.nn.sigmoid(gate) * (oc * lax.rsqrt(var + LN_EPS))).astype(o_ref.dtype)

    @pl.when(i == pl.num_programs(1) - 1)
    def _():
        s_out_ref[...] = s_scr[...]


def _retention(p3, s0, cos, sin, tabs, *, batch, rows, row0, first_valid):
    nb = rows // CHUNK
    base = row0 // CHUNK
    dmat, cross, kdec, gn = tabs

    def col(c):
        return pl.BlockSpec((CHUNK, BRANCH_WIDTH), lambda b, i: (base + b * nb + i, c))

    tab_rows = pl.BlockSpec((CHUNK, RET_HEAD_DIM), lambda b, i: (i, 0))

    def full(a):
        return pl.BlockSpec(a.shape, lambda b, i: (0,) * a.ndim)

    state_spec = pl.BlockSpec((None, RET_HEADS, RET_HEAD_DIM, RET_HEAD_DIM), lambda b, i: (b, 0, 0, 0))
    return pl.pallas_call(
        functools.partial(_ret_kernel, first_valid=first_valid),
        out_shape=(jax.ShapeDtypeStruct((batch * rows, BRANCH_WIDTH), BF16),
                   jax.ShapeDtypeStruct((batch, RET_HEADS, RET_HEAD_DIM, RET_HEAD_DIM), F32)),
        grid=(batch, nb),
        in_specs=[col(1), col(2), col(3), col(4), tab_rows, tab_rows, full(dmat), full(cross), full(kdec),
                  full(gn), state_spec],
        out_specs=(pl.BlockSpec((CHUNK, BRANCH_WIDTH), lambda b, i: (b * nb + i, 0)), state_spec),
        scratch_shapes=[pltpu.VMEM((RET_HEADS, RET_HEAD_DIM, RET_HEAD_DIM), F32)],
        compiler_params=_cparams(("parallel", "arbitrary")),
        name="retention",
    )(p3, p3, p3, p3, cos, sin, dmat, cross, kdec, gn, s0)


def _sortable(x):
    bits = pltpu.bitcast(x + 0.0, jnp.int32)
    return bits ^ ((bits >> 31) & 0x7FFFFFFF)


def _dsa_kernel(q_ref, qi_ref, w_ref, k_ref, vt_ref, ki_ref, bn_ref, ok_ref, o_ref,
                key_ref, keyn_ref, m_ref, den_ref, acc_ref, *, first_real, qb0, skip_below, k_sel):
    j = pl.program_id(1)

    @pl.when(j < skip_below)
    def _():
        o_ref[...] = jnp.zeros(o_ref.shape, o_ref.dtype)

    @pl.when(j >= skip_below)
    def _():
        win0 = pl.multiple_of(j * QT + (qb0 * CHUNK - (WN - QT)), LANES)
        n_far = (win0 + KEY_TILE - 1) // KEY_TILE
        near = pl.ds(win0, WN)

        def index_keys(ki, adm):
            acc = None
            for h in range(IDX_HEADS):
                term = w_ref[0, h:h + 1, :] * jnp.maximum(_dot(ki, qi_ref[0, h]), 0.0)
                acc = term if acc is None else acc + term
            return jnp.where(adm, _sortable(acc), INT_MIN)

        def far_keys(c, carry):
            c0 = pl.multiple_of(c * KEY_TILE, KEY_TILE)
            row = c0 + lax.broadcasted_iota(jnp.int32, (KEY_TILE, 1), 0)
            adm = jnp.logical_and(row >= first_real, row < win0)
            key_ref[c] = index_keys(ki_ref[0, pl.ds(c0, KEY_TILE), :], adm)
            return carry

        lax.fori_loop(0, n_far, far_keys, 0)
        row_n = win0 + lax.broadcasted_iota(jnp.int32, (WN, 1), 0)
        keyn_ref[...] = index_keys(ki_ref[0, near, :], jnp.logical_and(row_n >= first_real, ok_ref[...] != 0))

        def fold(hit):
            return jnp.sum(hit.reshape(hit.shape[0] // SUBLANES, SUBLANES, QT), axis=0)

        def count_ge(c):
            part = lax.fori_loop(0, n_far, lambda t, p: p + fold(jnp.where(key_ref[t] >= c, 1.0, 0.0)),
                                 fold(jnp.where(keyn_ref[...] >= c, 1.0, 0.0)))
            return jnp.sum(part, axis=0, keepdims=True)

        zero = jnp.zeros((1, QT), jnp.int32)
        thr0 = jnp.where(count_ge(zero) >= k_sel, zero, INT_MIN)

        def bit_step(it, thr):
            cand = thr + jnp.left_shift(jnp.int32(1), jnp.int32(30) - it)
            return jnp.where(count_ge(cand) >= k_sel, cand, thr)

        thr = lax.fori_loop(0, 31, bit_step, thr0)
        thr = jnp.maximum(thr, INT_MIN + 1)

        m_ref[...] = jnp.full(m_ref.shape, M_INIT, F32)
        den_ref[...] = jnp.zeros(den_ref.shape, F32)
        acc_ref[...] = jnp.zeros(acc_ref.shape, F32)
        scale = A_HEAD_DIM ** -0.5

        def attend(h, logits, sel, vt_tiles):
            lg = jnp.where(sel, logits, NEG)
            m_old = m_ref[h]
            m_new = jnp.maximum(m_old, jnp.max(lg, axis=0, keepdims=True))
            p = jnp.exp(lg - m_new)
            alpha = jnp.exp(m_old - m_new)
            den_ref[h] = alpha * den_ref[h] + jnp.sum(p, axis=0, keepdims=True)
            pb = p.astype(BF16)
            pv = None
            for u, vt in enumerate(vt_tiles):
                term = _dot(vt, pb[u * LANES:(u + 1) * LANES])
                pv = term if pv is None else pv + term
            acc_ref[h] = alpha * acc_ref[h] + pv
            m_ref[h] = m_new

        def far_tile(c, carry):
            c0 = pl.multiple_of(c * KEY_TILE, KEY_TILE)
            sel = key_ref[c] >= thr
            for h in range(A_HEADS):
                sl = slice(h * A_HEAD_DIM, (h + 1) * A_HEAD_DIM)
                logits = _dot(k_ref[0, pl.ds(c0, KEY_TILE), sl], q_ref[0, h]) * scale
                attend(h, logits, sel,
                       [vt_ref[0, c * (KEY_TILE // LANES) + u, sl, :] for u in range(KEY_TILE // LANES)])
            return carry

        lax.fori_loop(0, n_far, far_tile, 0)
        sel_n = keyn_ref[...] >= thr
        wt = win0 // LANES
        for h in range(A_HEADS):
            sl = slice(h * A_HEAD_DIM, (h + 1) * A_HEAD_DIM)
            logits = _dot(k_ref[0, near, sl], q_ref[0, h]) * scale + bn_ref[h]
            attend(h, logits, sel_n, [vt_ref[0, wt + u, sl, :] for u in range(WN // LANES)])
            out_t = acc_ref[h] / jnp.maximum(den_ref[h], TINY)
            o_ref[:, sl] = out_t.T.astype(o_ref.dtype)


def _dsa(q_rows, qi_rows, w_rows, k_rows, v_rows, ki_rows, bias_near, near_ok, *, first_real, qb0, skip_below, k_sel):
    batch, tq, width = q_rows.shape
    n_keys = k_rows.shape[1]
    nb = tq // QT
    q_t = jnp.transpose(q_rows.reshape(batch * nb, QT, A_HEADS, A_HEAD_DIM), (0, 2, 3, 1)).astype(BF16)
    qi_t = jnp.transpose(qi_rows.reshape(batch * nb, QT, IDX_HEADS, IDX_DIM), (0, 2, 3, 1)).astype(BF16)
    w_t = jnp.transpose(w_rows.reshape(batch * nb, QT, IDX_HEADS), (0, 2, 1)).astype(F32)
    v_t = jnp.transpose(v_rows.reshape(batch, n_keys // LANES, LANES, width), (0, 1, 3, 2)).astype(BF16)
    return pl.pallas_call(
        functools.partial(_dsa_kernel, first_real=first_real, qb0=qb0, skip_below=skip_below, k_sel=k_sel),
        out_shape=jax.ShapeDtypeStruct((batch * tq, width), BF16),
        grid=(batch, nb),
        in_specs=[
            pl.BlockSpec((1, A_HEADS, A_HEAD_DIM, QT), lambda b, j: (b * nb + j, 0, 0, 0)),
            pl.BlockSpec((1, IDX_HEADS, IDX_DIM, QT), lambda b, j: (b * nb + j, 0, 0, 0)),
            pl.BlockSpec((1, IDX_HEADS, QT), lambda b, j: (b * nb + j, 0, 0)),
            pl.BlockSpec((1, n_keys, width), lambda b, j: (b, 0, 0)),
            pl.BlockSpec((1, n_keys // LANES, width, LANES), lambda b, j: (b, 0, 0, 0)),
            pl.BlockSpec((1, n_keys, IDX_DIM), lambda b, j: (b, 0, 0)),
            pl.BlockSpec((A_HEADS, WN, QT), lambda b, j: (0, 0, 0)),
            pl.BlockSpec((WN, QT), lambda b, j: (0, 0)),
        ],
        out_specs=pl.BlockSpec((QT, width), lambda b, j: (b * nb + j, 0)),
        scratch_shapes=[pltpu.VMEM((n_keys // KEY_TILE, KEY_TILE, QT), jnp.int32),
                        pltpu.VMEM((WN, QT), jnp.int32),
                        pltpu.VMEM((A_HEADS, 1, QT), F32),
                        pltpu.VMEM((A_HEADS, 1, QT), F32),
                        pltpu.VMEM((A_HEADS, A_HEAD_DIM, QT), F32)],
        compiler_params=_cparams(("parallel", "arbitrary")),
        name="dsa",
    )(q_t, qi_t, w_t, k_rows.astype(BF16), v_t, ki_rows.astype(BF16), bias_near, near_ok)


def _t5_bucket(rel):
    half = T5_BUCKETS // 2
    exact = half // 2
    n = jnp.abs(rel)
    large = exact + (jnp.log(jnp.maximum(n, 1).astype(F32) / exact)
                     / math.log(T5_MAX_DIST / exact) * (half - exact)).astype(jnp.int32)
    large = jnp.minimum(large, half - 1)
    return jnp.where(rel > 0, half, 0) + jnp.where(n < exact, n, large)


def _bias_tables(t5_bias):
    a = jnp.arange(WN, dtype=jnp.int32)[:, None]
    t = jnp.arange(QT, dtype=jnp.int32)[None, :]
    near = jnp.transpose(t5_bias[_t5_bucket(a - (WN - QT) - t)], (2, 0, 1)).astype(F32)
    far = t5_bias[_t5_bucket(jnp.int32(-2 * CHUNK - 1))].astype(F32)
    ok = (a // CHUNK - (WN - QT) // CHUNK <= t // CHUNK).astype(jnp.int32)
    return near - far[:, None, None], ok


def _rope_tables(pos):
    half = RET_HEAD_DIM // 2
    inv = ROPE_BASE ** (-jnp.arange(half, dtype=F32) / half)
    ang = pos.astype(F32)[:, None] * inv[None, :]
    cos, sin = jnp.cos(ang), jnp.sin(ang)
    return jnp.concatenate([cos, cos], axis=-1), jnp.concatenate([-sin, sin], axis=-1)


def _decay_tables():
    n = CHUNK
    log_g = jnp.log(1.0 - 2.0 ** (-5.0 - jnp.arange(RET_HEADS, dtype=F32)))
    i = jnp.arange(n, dtype=F32)
    diff = i[:, None] - i[None, :]
    dmat = jnp.where(diff >= 0, jnp.exp(jnp.maximum(diff, 0.0)[None] * log_g[:, None, None]), 0.0)
    cross = jnp.exp((i[None, :] + 1.0) * log_g[:, None])
    kdec = jnp.exp((n - 1.0 - i)[None, :] * log_g[:, None])
    gn = jnp.exp(n * log_g)
    wide = (RET_HEADS, n, RET_HEAD_DIM)
    return (dmat, jnp.broadcast_to(cross[:, :, None], wide), jnp.broadcast_to(kdec[:, :, None], wide),
            jnp.broadcast_to(gn[:, None, None], (RET_HEADS, 1, RET_HEAD_DIM)))


def kernel(x_prompt, x_sample, cache_k, cache_v, cache_ki, cache_pool, state_ret, meta_tokens, ln_in_g, ln_in_b, w_in, t5_bias, pool_w, pool_scale, w_branch, w_gate, b_gate, w_out, ln1_g, ln1_b, ln2_g, ln2_b, ffn_w_gate, ffn_w_up, ffn_w_down, moe_w_router, moe_b_router, moe_w_gate, moe_w_up, moe_w_down):
    bp, seq, d = x_prompt.shape
    bs, ts, _ = x_sample.shape
    past = cache_k.shape[2]
    t_real = seq + N_META
    tp = -(-(t_real + 2 * CHUNK) // KEY_TILE) * KEY_TILE
    front = tp - t_real
    assert front % CHUNK == CHUNK - N_META and ts == CHUNK and past % CHUNK == 0
    mp, ms = bp * tp, bs * ts
    m = mp + ms
    assert mp % TM == 0 and ms % TM == 0
    ksel_p = min(TOPK_MAX, seq // 4)
    ksel_s = min(TOPK_MAX, (past + ts) // 4)
    s_front = (-(past + QT)) % KEY_TILE
    assert tp % QT == 0 and front >= WN - QT and (s_front + past) % QT == 0

    meta = jnp.broadcast_to(meta_tokens.astype(F32)[None], (bp, N_META, d))
    x_rows = jnp.concatenate(
        [jnp.concatenate([jnp.zeros((bp, front, d), F32), meta, x_prompt], axis=1).reshape(mp, d),
         x_sample.reshape(ms, d)], axis=0)
    x32, xb = _layer_norm(x_rows, None, ln_in_g, ln_in_b)

    bias_near, near_ok = _bias_tables(t5_bias)
    cos_p, sin_p = _rope_tables(jnp.arange(tp, dtype=jnp.int32) - (front + N_META))
    cos_s, sin_s = _rope_tables(past + jnp.arange(ts, dtype=jnp.int32))
    decay = _decay_tables()
    zero_state = jnp.zeros((bp, RET_HEADS, RET_HEAD_DIM, RET_HEAD_DIM), F32)
    w_tail = w_in[:, :, TAIL_OFF:]

    outs = {name: [] for name in ("kp", "vp", "kip", "poolp", "retp", "ks", "vs", "kis", "pools", "rets")}
    bw = BRANCH_WIDTH
    for l in range(DEPTH):
        p1 = _matmul(xb, w_in, (l,), 0, 4, bw, name="proj_attn")
        p2 = _matmul(xb, w_in, (l,), 4 * bw // LANES, 1, LANES, name="proj_idx")
        p3 = _matmul(xb, w_tail, (l,), 0, 5, bw, name="proj_tail")

        k_new, v_new, ki_new = p1[:, bw:2 * bw], p1[:, 2 * bw:3 * bw], p2[:, :IDX_DIM]
        q_a, q_i, w_i = p1[:, :bw], p1[:, 3 * bw:], p2[:, IDX_DIM:IDX_DIM + IDX_HEADS]

        def prompt_seq(a):
            return a[:mp].reshape(bp, tp, a.shape[1])

        def sample_queries(a):
            return jnp.pad(a[mp:].reshape(bs, ts, a.shape[1]), ((0, 0), (0, QT - ts), (0, 0)))

        def sample_keys(cache, new):
            width = new.shape[1]
            return jnp.pad(jnp.concatenate([cache.reshape(bs, past, width), new[mp:].reshape(bs, ts, width)], axis=1),
                           ((0, 0), (s_front, QT - ts), (0, 0)))

        oa_p = _dsa(prompt_seq(q_a), prompt_seq(q_i), prompt_seq(w_i), prompt_seq(k_new), prompt_seq(v_new),
                    prompt_seq(ki_new), bias_near, near_ok, first_real=front, qb0=0,
                    skip_below=max(1, front // QT), k_sel=ksel_p)
        oa_s = _dsa(sample_queries(q_a), sample_queries(q_i), sample_queries(w_i), sample_keys(cache_k[l], k_new),
                    sample_keys(cache_v[l], v_new), sample_keys(cache_ki[l], ki_new), bias_near, near_ok,
                    first_real=s_front, qb0=(s_front + past) // CHUNK, skip_below=0, k_sel=ksel_s)
        oa_s = oa_s.reshape(bs, QT, bw)[:, :ts].reshape(ms, bw)

        tb_p = 256
        ob_p = _pool(p3, 0, p3, 0, pool_w[l], pool_scale[l], batch=bp, rows=tp, tb=tb_p, row0=0,
                     halo_map=lambda b, i: jnp.maximum((b * tp + i * tb_p) // HALO - 1, 0), first_valid=front)
        pool_hist = jnp.pad(cache_pool[l], ((0, 0), (HALO - POOL_PAST, 0), (0, 0))).reshape(bs * HALO, bw)
        ob_s = _pool(p3, 0, pool_hist, 0, pool_w[l], pool_scale[l], batch=bs, rows=ts, tb=ts, row0=mp,
                     halo_map=lambda b, i: b, first_valid=-POOL_PAST)

        oc_p, ret_p = _retention(p3, zero_state, cos_p, sin_p, decay, batch=bp, rows=tp, row0=0, first_valid=front)
        oc_s, ret_s = _retention(p3, state_ret[l].astype(F32), cos_s, sin_s, decay, batch=bs, rows=ts, row0=mp,
                                 first_valid=0)

        merged = _merge(l, xb, jnp.concatenate([oa_p, oa_s]), jnp.concatenate([ob_p, ob_s]),
                        jnp.concatenate([oc_p, oc_s]), w_branch, w_gate, b_gate)
        y = _matmul(merged, w_out, (l,), 0, d // 512, 512, name="w_out")
        x32, xb = _layer_norm(x32, y, ln1_g[l], ln1_b[l])

        if l % 2 == 0:
            h = _gate_up(xb, ffn_w_gate, ffn_w_up, (l // 2,))
            f = _matmul(h, ffn_w_down, (l // 2,), 0, d // 512, 512, name="ffn_down")
        else:
            gates = _router(x32, moe_w_router[l // 2], moe_b_router[l // 2])
            f = None
            for e in range(N_EXPERTS):
                h = _gate_up(xb, moe_w_gate, moe_w_up, (l // 2, e), row_scale=gates[:, e:e + 1])
                f = _matmul(h, moe_w_down, (l // 2, e), 0, d // 512, 512, acc=f, name="moe_down")
        x32, xb = _layer_norm(x32, f, ln2_g[l], ln2_b[l])

        def prompt_rows(a, width):
            return a[:mp].reshape(bp, tp, width)[:, front:]

        u = p3[:, :bw]
        outs["kp"].append(prompt_rows(k_new, bw).reshape(bp, t_real, A_HEADS, A_HEAD_DIM))
        outs["vp"].append(prompt_rows(v_new, bw).reshape(bp, t_real, A_HEADS, A_HEAD_DIM))
        outs["kip"].append(prompt_rows(ki_new, IDX_DIM))
        outs["poolp"].append(prompt_rows(u, bw)[:, -POOL_PAST:])
        outs["retp"].append(ret_p)
        outs["ks"].append(k_new[mp:].reshape(bs, ts, A_HEADS, A_HEAD_DIM))
        outs["vs"].append(v_new[mp:].reshape(bs, ts, A_HEADS, A_HEAD_DIM))
        outs["kis"].append(ki_new[mp:].reshape(bs, ts, IDX_DIM))
        outs["pools"].append(u[mp:].reshape(bs, ts, bw)[:, -POOL_PAST:])
        outs["rets"].append(ret_s)

    y_prompt = x32[:mp].reshape(bp, tp, d)[:, front + N_META:]
    y_sample = x32[mp:].reshape(bs, ts, d)
    return (y_prompt, y_sample) + tuple(
        jnp.stack(outs[name]) for name in ("kp", "vp", "kip", "poolp", "retp", "ks", "vs", "kis", "pools", "rets"))
```

```python
import functools
import math

import jax
import jax.numpy as jnp
from jax import lax
from jax.experimental import pallas as pl
from jax.experimental.pallas import tpu as pltpu

F32 = jnp.float32
BF16 = jnp.bfloat16

D_MODEL = 2048
DEPTH = 2
CHUNK = 64
N_META = 16
BRANCH_WIDTH = D_MODEL // 2
A_HEADS = 8
A_HEAD_DIM = BRANCH_WIDTH // A_HEADS
IDX_HEADS = 16
IDX_DIM = 64
TOPK_MAX = 256
T5_BUCKETS = 32
T5_MAX_DIST = 128
POOL_WINDOWS = (2, 4, 8, 16)
POOL_GROUP = BRANCH_WIDTH // 4
POOL_PAST = 15
RET_HEADS = 8
RET_HEAD_DIM = BRANCH_WIDTH // RET_HEADS
ROPE_BASE = 10000.0
N_BRANCH = 3
D_FF = 11 * D_MODEL // 4
N_EXPERTS = 8
TOP_K = 2
ALPHA = (2 * DEPTH) ** 0.25
LN_EPS = 1e-5
IN_SPLITS = (BRANCH_WIDTH, BRANCH_WIDTH, BRANCH_WIDTH, IDX_HEADS * IDX_DIM, IDX_DIM, IDX_HEADS,
             BRANCH_WIDTH, BRANCH_WIDTH, BRANCH_WIDTH, BRANCH_WIDTH, BRANCH_WIDTH)
IN_WIDTH = sum(IN_SPLITS)
TAIL_OFF = 4 * BRANCH_WIDTH + IDX_DIM + IDX_HEADS

LANES = 128
SUBLANES = 8
HALO = 16
QT = 2 * CHUNK
WN = QT + 2 * CHUNK
KEY_TILE = 256
VMEM_LIMIT = 56 * 1024 * 1024
TM = 512
MOE_TILE = 256
INT_MIN = -2 ** 31
NEG = -1e30
M_INIT = -1e20
TINY = 1e-30


def _cparams(sem):
    return pltpu.CompilerParams(dimension_semantics=sem, vmem_limit_bytes=VMEM_LIMIT)


def _dot(a, b):
    return jnp.dot(a, b, preferred_element_type=F32)


def _dot_nt(a, b):
    return lax.dot_general(a, b, (((1,), (1,)), ((), ())), preferred_element_type=F32)


def _dot_tn(a, b):
    return lax.dot_general(a, b, (((0,), (0,)), ((), ())), preferred_element_type=F32)


def _ln_kernel(x_ref, g_ref, b_ref, o32_ref, o16_ref):
    x = x_ref[...]
    mu = jnp.mean(x, axis=-1, keepdims=True)
    xc = x - mu
    var = jnp.mean(xc * xc, axis=-1, keepdims=True)
    y = xc * lax.rsqrt(var + LN_EPS) * g_ref[...] + b_ref[...]
    o32_ref[...] = y
    o16_ref[...] = y.astype(BF16)


def _ln_res_kernel(x_ref, y_ref, g_ref, b_ref, o32_ref, o16_ref):
    x = ALPHA * x_ref[...] + y_ref[...]
    mu = jnp.mean(x, axis=-1, keepdims=True)
    xc = x - mu
    var = jnp.mean(xc * xc, axis=-1, keepdims=True)
    y = xc * lax.rsqrt(var + LN_EPS) * g_ref[...] + b_ref[...]
    o32_ref[...] = y
    o16_ref[...] = y.astype(BF16)


def _layer_norm(x, y, g, b, tm=256):
    m, d = x.shape
    row = pl.BlockSpec((tm, d), lambda i: (i, 0))
    vec = pl.BlockSpec((1, d), lambda i: (0, 0))
    args = (x,) if y is None else (x, y)
    return pl.pallas_call(
        _ln_kernel if y is None else _ln_res_kernel,
        out_shape=(jax.ShapeDtypeStruct((m, d), F32), jax.ShapeDtypeStruct((m, d), BF16)),
        grid=(m // tm,),
        in_specs=[row] * len(args) + [vec, vec],
        out_specs=(row, row),
        compiler_params=_cparams(("parallel",)),
        name="layer_norm",
    )(*args, g.reshape(1, d), b.reshape(1, d))


def _mm_kernel(x_ref, w_ref, o_ref, wb_ref):
    @pl.when(pl.program_id(1) == 0)
    def _():
        wb_ref[...] = w_ref[...].astype(BF16)

    o_ref[...] = _dot(x_ref[...], wb_ref[...]).astype(o_ref.dtype)


def _matmul(x, w, lead, col0, n_tiles, tn, name="matmul"):
    m, k = x.shape
    nl = len(lead)
    w_spec = pl.BlockSpec((None,) * nl + (k, tn), lambda j, i: tuple(lead) + (0, j + col0))
    x_spec = pl.BlockSpec((TM, k), lambda j, i: (i, 0))
    o_spec = pl.BlockSpec((TM, tn), lambda j, i: (i, j))
    return pl.pallas_call(
        _mm_kernel,
        out_shape=jax.ShapeDtypeStruct((m, n_tiles * tn), F32),
        grid=(n_tiles, m // TM),
        in_specs=[x_spec, w_spec],
        out_specs=o_spec,
        scratch_shapes=[pltpu.VMEM((k, tn), BF16)],
        compiler_params=_cparams(("arbitrary", "arbitrary")),
        name=name,
    )(x, w)


def _gate_up_kernel(x_ref, wg_ref, wu_ref, o_ref, wgb_ref, wub_ref):
    @pl.when(pl.program_id(1) == 0)
    def _():
        wgb_ref[...] = wg_ref[...].astype(BF16)
        wub_ref[...] = wu_ref[...].astype(BF16)

    x = x_ref[...]
    g = _dot(x, wgb_ref[...])
    u = _dot(x, wub_ref[...])
    o_ref[...] = (g * jax.nn.sigmoid(g) * u).astype(o_ref.dtype)


def _gate_up(x, wg, wu, lead, tn=512):
    m, k = x.shape
    n = wg.shape[-1]
    nl = len(lead)
    w_spec = pl.BlockSpec((None,) * nl + (k, tn), lambda j, i: tuple(lead) + (0, j))
    x_spec = pl.BlockSpec((TM, k), lambda j, i: (i, 0))
    return pl.pallas_call(
        _gate_up_kernel,
        out_shape=jax.ShapeDtypeStruct((m, n), BF16),
        grid=(n // tn, m // TM),
        in_specs=[x_spec, w_spec, w_spec],
        out_specs=pl.BlockSpec((TM, tn), lambda j, i: (i, j)),
        scratch_shapes=[pltpu.VMEM((k, tn), BF16), pltpu.VMEM((k, tn), BF16)],
        compiler_params=_cparams(("arbitrary", "arbitrary")),
        name="gate_up",
    )(x, wg, wu)


def _merge_kernel(x_ref, oa_ref, ob_ref, oc_ref, wg0_ref, wg1_ref, wg2_ref, wb0_ref, wb1_ref, wb2_ref,
                  bg0_ref, bg1_ref, bg2_ref, o_ref, wgb_ref, wbb_ref):
    wg_refs = (wg0_ref, wg1_ref, wg2_ref)
    wb_refs = (wb0_ref, wb1_ref, wb2_ref)

    @pl.when(pl.program_id(1) == 0)
    def _():
        for n in range(N_BRANCH):
            wgb_ref[n] = wg_refs[n][...].astype(BF16)
            wbb_ref[n] = wb_refs[n][...].astype(BF16)

    x = x_ref[...]
    acc = None
    for n, (o_in, bg) in enumerate(zip((oa_ref, ob_ref, oc_ref), (bg0_ref, bg1_ref, bg2_ref))):
        gate = jax.nn.sigmoid(_dot(x, wgb_ref[n]) + bg[...])
        term = gate * _dot(o_in[...], wbb_ref[n])
        acc = term if acc is None else acc + term
    o_ref[...] = acc.astype(o_ref.dtype)


def _merge(l, x, oa, ob, oc, w_branch, w_gate, b_gate, tn=256):
    m, d = x.shape
    w = oa.shape[1]
    nt = d // tn
    x_spec = pl.BlockSpec((TM, d), lambda j, i: (i, 0))
    o_in_spec = pl.BlockSpec((TM, w), lambda j, i: (i, 0))
    wg_specs = [pl.BlockSpec((None, d, tn), lambda j, i, n=n: (l, 0, n * nt + j)) for n in range(N_BRANCH)]
    wb_specs = [pl.BlockSpec((None, None, w, tn), lambda j, i, n=n: (l, n, 0, j)) for n in range(N_BRANCH)]
    bg_specs = [pl.BlockSpec((None, 1, tn), lambda j, i, n=n: (l, 0, n * nt + j)) for n in range(N_BRANCH)]
    return pl.pallas_call(
        _merge_kernel,
        out_shape=jax.ShapeDtypeStruct((m, d), BF16),
        grid=(nt, m // TM),
        in_specs=[x_spec, o_in_spec, o_in_spec, o_in_spec] + wg_specs + wb_specs + bg_specs,
        out_specs=pl.BlockSpec((TM, tn), lambda j, i: (i, j)),
        scratch_shapes=[pltpu.VMEM((N_BRANCH, d, tn), BF16), pltpu.VMEM((N_BRANCH, w, tn), BF16)],
        compiler_params=_cparams(("arbitrary", "arbitrary")),
        name="merge",
    )(x, oa, ob, oc, w_gate, w_gate, w_gate, w_branch, w_branch, w_branch,
      b_gate.reshape(DEPTH, 1, N_BRANCH * d), b_gate.reshape(DEPTH, 1, N_BRANCH * d),
      b_gate.reshape(DEPTH, 1, N_BRANCH * d))


def _split_bf16(a):
    hi = a.astype(BF16)
    lo = (a - hi.astype(F32)).astype(BF16)
    return hi, lo


def _router_kernel(x_ref, w_ref, b_ref, gate_ref, expert_ref):
    xh, xl = _split_bf16(x_ref[...])
    wh, wl = _split_bf16(w_ref[...])
    logits = _dot(xh, wh) + (_dot(xh, wl) + _dot(xl, wh)) + b_ref[...]
    lane = lax.broadcasted_iota(jnp.int32, logits.shape, 1)
    logits = jnp.where(lane < N_EXPERTS, logits, -jnp.inf)
    m1 = jnp.max(logits, axis=-1, keepdims=True)
    i1 = jnp.min(jnp.where(logits == m1, lane, LANES), axis=-1, keepdims=True)
    rest = jnp.where(lane == i1, -jnp.inf, logits)
    m2 = jnp.max(rest, axis=-1, keepdims=True)
    i2 = jnp.min(jnp.where(rest == m2, lane, LANES), axis=-1, keepdims=True)
    e = jnp.exp(m2 - m1)
    p1 = 1.0 / (1.0 + e)
    p2 = e / (1.0 + e)
    gate_ref[...] = jnp.where(lane == 0, p1, jnp.where(lane == 1, p2, 0.0))
    expert_ref[...] = jnp.where(lane == 0, i1, jnp.where(lane == 1, i2, 0))


def _router(x32, w_r, b_r, tm=256):
    m, d = x32.shape
    w_pad = jnp.pad(w_r, ((0, 0), (0, LANES - N_EXPERTS)))
    b_pad = jnp.pad(b_r, (0, LANES - N_EXPERTS)).reshape(1, LANES)
    out_spec = pl.BlockSpec((tm, LANES), lambda i: (i, 0))
    gates, experts = pl.pallas_call(
        _router_kernel,
        out_shape=(jax.ShapeDtypeStruct((m, LANES), F32), jax.ShapeDtypeStruct((m, LANES), jnp.int32)),
        grid=(m // tm,),
        in_specs=[pl.BlockSpec((tm, d), lambda i: (i, 0)), pl.BlockSpec((d, LANES), lambda i: (0, 0)),
                  pl.BlockSpec((1, LANES), lambda i: (0, 0))],
        out_specs=(out_spec, out_spec),
        compiler_params=_cparams(("parallel",)),
        name="router",
    )(x32, w_pad, b_pad)
    return gates[:, :TOP_K], experts[:, :TOP_K]


def _dispatch_plan(experts, gates, tile):
    m = experts.shape[0]
    n_assign = TOP_K * m
    n_tiles = n_assign // tile + N_EXPERTS
    e = experts.reshape(n_assign)
    onehot = (e[:, None] == jnp.arange(N_EXPERTS, dtype=jnp.int32)[None, :]).astype(jnp.int32)
    rank = jnp.sum((jnp.cumsum(onehot, axis=0) - onehot) * onehot, axis=1)
    tiles_per = (jnp.sum(onehot, axis=0) + tile - 1) // tile
    tile_end = jnp.cumsum(tiles_per)
    pos = ((tile_end - tiles_per)[e] * tile + rank).astype(jnp.int32)
    src = jnp.zeros((n_tiles * tile,), jnp.int32).at[pos].set(jnp.arange(n_assign, dtype=jnp.int32) // TOP_K)
    gate = jnp.zeros((n_tiles * tile,), F32).at[pos].set(gates.reshape(n_assign))
    n_used = tile_end[-1:]
    t_idx = jnp.minimum(jnp.arange(n_tiles, dtype=jnp.int32), n_used - 1)
    tile_expert = jnp.sum((t_idx[:, None] >= tile_end[None, :]).astype(jnp.int32), axis=1)
    return src, gate.reshape(-1, 1), pos, tile_expert.astype(jnp.int32), n_used.astype(jnp.int32)


def _issue_rows(idx_ref, first, count, stride, src_hbm, dst, sem):
    def body(i, carry):
        pltpu.make_async_copy(src_hbm.at[pl.ds(idx_ref[first + i * stride], 1), :], dst.at[pl.ds(i, 1), :],
                              sem).start()
        return carry

    lax.fori_loop(0, count, body, 0, unroll=8)


def _wait_rows(count, src_hbm, dst, sem):
    def body(i, carry):
        pltpu.make_async_copy(src_hbm.at[pl.ds(0, 1), :], dst.at[pl.ds(i, 1), :], sem).wait()
        return carry

    lax.fori_loop(0, count, body, 0, unroll=8)


def _gather_kernel(src_ref, x_hbm, o_ref, buf, sem, *, tile):
    t = pl.program_id(0)

    @pl.when(t == 0)
    def _():
        _issue_rows(src_ref, 0, tile, 1, x_hbm, buf.at[0], sem.at[0])

    @pl.when(t + 1 < pl.num_programs(0))
    def _():
        nxt = (t + 1) % 2
        _issue_rows(src_ref, (t + 1) * tile, tile, 1, x_hbm, buf.at[nxt], sem.at[nxt])

    cur = t % 2
    _wait_rows(tile, x_hbm, buf.at[cur], sem.at[cur])
    o_ref[...] = buf[cur].astype(o_ref.dtype)


def _gather_rows(x32, src, tile):
    d = x32.shape[1]
    n_rows = src.shape[0]
    return pl.pallas_call(
        functools.partial(_gather_kernel, tile=tile),
        out_shape=jax.ShapeDtypeStruct((n_rows, d), BF16),
        grid_spec=pltpu.PrefetchScalarGridSpec(
            num_scalar_prefetch=1, grid=(n_rows // tile,),
            in_specs=[pl.BlockSpec(memory_space=pl.ANY)],
            out_specs=pl.BlockSpec((tile, d), lambda t, src: (t, 0)),
            scratch_shapes=[pltpu.VMEM((2, tile, d), F32), pltpu.SemaphoreType.DMA((2,))]),
        compiler_params=_cparams(("arbitrary",)),
        name="moe_gather",
    )(src, x32)


def _combine_kernel(pos_ref, y_hbm, o_ref, buf, sem, *, tile):
    t = pl.program_id(0)

    def issue(tile_idx, slot):
        for s in range(TOP_K):
            _issue_rows(pos_ref, tile_idx * tile * TOP_K + s, tile, TOP_K, y_hbm, buf.at[slot, s], sem.at[slot])

    @pl.when(t == 0)
    def _():
        issue(0, 0)

    @pl.when(t + 1 < pl.num_programs(0))
    def _():
        issue(t + 1, (t + 1) % 2)

    cur = t % 2
    for s in range(TOP_K):
        _wait_rows(tile, y_hbm, buf.at[cur, s], sem.at[cur])
    total = buf[cur, 0]
    for s in range(1, TOP_K):
        total = total + buf[cur, s]
    o_ref[...] = total


def _combine_rows(y_sorted, pos, m, tile):
    d = y_sorted.shape[1]
    return pl.pallas_call(
        functools.partial(_combine_kernel, tile=tile),
        out_shape=jax.ShapeDtypeStruct((m, d), F32),
        grid_spec=pltpu.PrefetchScalarGridSpec(
            num_scalar_prefetch=1, grid=(m // tile,),
            in_specs=[pl.BlockSpec(memory_space=pl.ANY)],
            out_specs=pl.BlockSpec((tile, d), lambda t, pos: (t, 0)),
            scratch_shapes=[pltpu.VMEM((2, TOP_K, tile, d), F32), pltpu.SemaphoreType.DMA((2,))]),
        compiler_params=_cparams(("arbitrary",)),
        name="moe_combine",
    )(pos, y_sorted)


def _expert_changed(te_ref, t):
    return jnp.logical_or(t == 0, te_ref[t] != te_ref[jnp.maximum(t - 1, 0)])


def _moe_gate_up_kernel(te_ref, nu_ref, x_ref, wg_ref, wu_ref, o_ref, wgb_ref, wub_ref):
    t = pl.program_id(1)

    @pl.when(_expert_changed(te_ref, t))
    def _():
        wgb_ref[...] = wg_ref[...].astype(BF16)
        wub_ref[...] = wu_ref[...].astype(BF16)

    @pl.when(t < nu_ref[0])
    def _():
        x = x_ref[...]
        g = _dot(x, wgb_ref[...])
        u = _dot(x, wub_ref[...])
        o_ref[...] = (g * jax.nn.sigmoid(g) * u).astype(o_ref.dtype)

    @pl.when(t >= nu_ref[0])
    def _():
        o_ref[...] = jnp.zeros(o_ref.shape, o_ref.dtype)


def _moe_down_kernel(te_ref, nu_ref, h_ref, w_ref, g_ref, o_ref, wb_ref):
    t = pl.program_id(1)

    @pl.when(_expert_changed(te_ref, t))
    def _():
        wb_ref[...] = w_ref[...].astype(BF16)

    @pl.when(t < nu_ref[0])
    def _():
        o_ref[...] = _dot(h_ref[...], wb_ref[...]) * g_ref[...]

    @pl.when(t >= nu_ref[0])
    def _():
        o_ref[...] = jnp.zeros(o_ref.shape, o_ref.dtype)


def _moe_experts(x_sorted, gate_sorted, tile_expert, n_used, w_gate, w_up, w_down, layer, tile, tn=512):
    r, d = x_sorted.shape
    ff = w_gate.shape[-1]
    n_tiles = r // tile

    def x_rows(width):
        return pl.BlockSpec((tile, width), lambda j, t, te, nu: (jnp.minimum(t, nu[0] - 1), 0))

    def w_cols(k):
        return pl.BlockSpec((None, None, k, tn), lambda j, t, te, nu: (layer, te[t], 0, j))

    h = pl.pallas_call(
        _moe_gate_up_kernel,
        out_shape=jax.ShapeDtypeStruct((r, ff), BF16),
        grid_spec=pltpu.PrefetchScalarGridSpec(
            num_scalar_prefetch=2, grid=(ff // tn, n_tiles),
            in_specs=[x_rows(d), w_cols(d), w_cols(d)],
            out_specs=pl.BlockSpec((tile, tn), lambda j, t, te, nu: (t, j)),
            scratch_shapes=[pltpu.VMEM((d, tn), BF16), pltpu.VMEM((d, tn), BF16)]),
        compiler_params=_cparams(("arbitrary", "arbitrary")),
        name="moe_gate_up",
    )(tile_expert, n_used, x_sorted, w_gate, w_up)
    return pl.pallas_call(
        _moe_down_kernel,
        out_shape=jax.ShapeDtypeStruct((r, d), F32),
        grid_spec=pltpu.PrefetchScalarGridSpec(
            num_scalar_prefetch=2, grid=(d // tn, n_tiles),
            in_specs=[x_rows(ff), w_cols(ff),
                      pl.BlockSpec((tile, 1), lambda j, t, te, nu: (jnp.minimum(t, nu[0] - 1), 0))],
            out_specs=pl.BlockSpec((tile, tn), lambda j, t, te, nu: (t, j)),
            scratch_shapes=[pltpu.VMEM((ff, tn), BF16)]),
        compiler_params=_cparams(("arbitrary", "arbitrary")),
        name="moe_down",
    )(tile_expert, n_used, h, w_down, gate_sorted)


def _pool_kernel(u_ref, halo_ref, w_ref, sc_ref, o_ref, ext_ref, *, tb, first_valid):
    row0 = pl.program_id(1) * tb
    r_cur = row0 + lax.broadcasted_iota(jnp.int32, (tb, 1), 0)
    r_halo = row0 - HALO + lax.broadcasted_iota(jnp.int32, (HALO, 1), 0)
    u = jnp.where(r_cur >= first_valid, u_ref[...], 0.0)
    ext_ref[0:HALO, :] = jnp.where(r_halo >= first_valid, halo_ref[...], 0.0)
    ext_ref[HALO:, :] = u
    seen = (r_cur - first_valid + 1).astype(F32)
    for g, win in enumerate(POOL_WINDOWS):
        c0, c1 = g * POOL_GROUP, (g + 1) * POOL_GROUP
        s = u[:, c0:c1]
        for back in range(1, win):
            s = s + ext_ref[HALO - back:HALO - back + tb, c0:c1]
        cnt = jnp.clip(seen, 1.0, float(win))
        diff = s / cnt - u[:, c0:c1]
        y = _dot(diff.astype(BF16), w_ref[g].astype(BF16))
        o_ref[:, c0:c1] = (y * sc_ref[:, c0:c1]).astype(o_ref.dtype)


def _pool(u2d, col_block, halo2d, halo_col_block, pool_w_l, pool_scale_l, *, batch, rows, tb, row0, halo_map,
          first_valid):
    nb = rows // tb
    base = row0 // tb
    return pl.pallas_call(
        functools.partial(_pool_kernel, tb=tb, first_valid=first_valid),
        out_shape=jax.ShapeDtypeStruct((batch * rows, BRANCH_WIDTH), BF16),
        grid=(batch, nb),
        in_specs=[
            pl.BlockSpec((tb, BRANCH_WIDTH), lambda b, i: (base + b * nb + i, col_block)),
            pl.BlockSpec((HALO, BRANCH_WIDTH), lambda b, i: (halo_map(b, i), halo_col_block)),
            pl.BlockSpec((len(POOL_WINDOWS), POOL_GROUP, POOL_GROUP), lambda b, i: (0, 0, 0)),
            pl.BlockSpec((1, BRANCH_WIDTH), lambda b, i: (0, 0)),
        ],
        out_specs=pl.BlockSpec((tb, BRANCH_WIDTH), lambda b, i: (b * nb + i, 0)),
        scratch_shapes=[pltpu.VMEM((HALO + tb, BRANCH_WIDTH), F32)],
        compiler_params=_cparams(("parallel", "arbitrary")),
        name="pool",
    )(u2d, halo2d, pool_w_l, pool_scale_l.reshape(1, BRANCH_WIDTH))


def _ret_kernel(q_ref, k_ref, v_ref, g_ref, cos_ref, sin_ref, dmat_ref, cross_ref, kdec_ref, gn_ref, s0_ref,
                o_ref, s_out_ref, s_scr, *, first_valid):
    i = pl.program_id(1)

    @pl.when(i == 0)
    def _():
        s_scr[...] = s0_ref[...]

    rows = i * CHUNK + lax.broadcasted_iota(jnp.int32, (CHUNK, 1), 0)
    valid = rows >= first_valid
    cos = cos_ref[...]
    sin = sin_ref[...]
    rscale = RET_HEAD_DIM ** -0.5
    half = RET_HEAD_DIM // 2
    for h in range(RET_HEADS):
        sl = slice(h * RET_HEAD_DIM, (h + 1) * RET_HEAD_DIM)
        q = q_ref[:, sl]
        k = k_ref[:, sl]
        v = jnp.where(valid, v_ref[:, sl], 0.0).astype(BF16)
        qr = (q * cos + pltpu.roll(q, half, 1) * sin).astype(BF16)
        kr = jnp.where(valid, (k * cos + pltpu.roll(k, half, 1) * sin) * rscale, 0.0)
        state = s_scr[h]
        inner = _dot_nt(qr, kr.astype(BF16)) * dmat_ref[h]
        o = _dot(inner.astype(BF16), v) + _dot(qr, state.astype(BF16)) * cross_ref[h]
        s_scr[h] = gn_ref[h] * state + _dot_tn((kr * kdec_ref[h]).astype(BF16), v)
        mu = jnp.mean(o, axis=-1, keepdims=True)
        oc = o - mu
        var = jnp.mean(oc * oc, axis=-1, keepdims=True)
        gate = g_ref[:, sl]
        o_ref[:, sl] = (gate * jax.nn.sigmoid(gate) * (oc * lax.rsqrt(var + LN_EPS))).astype(o_ref.dtype)

    @pl.when(i == pl.num_programs(1) - 1)
    def _():
        s_out_ref[...] = s_scr[...]


def _retention(p3, s0, cos, sin, tabs, *, batch, rows, row0, first_valid):
    nb = rows // CHUNK
    base = row0 // CHUNK
    dmat, cross, kdec, gn = tabs

    def col(c):
        return pl.BlockSpec((CHUNK, BRANCH_WIDTH), lambda b, i: (base + b * nb + i, c))

    tab_rows = pl.BlockSpec((CHUNK, RET_HEAD_DIM), lambda b, i: (i, 0))

    def full(a):
        return pl.BlockSpec(a.shape, lambda b, i: (0,) * a.ndim)

    state_spec = pl.BlockSpec((None, RET_HEADS, RET_HEAD_DIM, RET_HEAD_DIM), lambda b, i: (b, 0, 0, 0))
    return pl.pallas_call(
        functools.partial(_ret_kernel, first_valid=first_valid),
        out_shape=(jax.ShapeDtypeStruct((batch * rows, BRANCH_WIDTH), BF16),
                   jax.ShapeDtypeStruct((batch, RET_HEADS, RET_HEAD_DIM, RET_HEAD_DIM), F32)),
        grid=(batch, nb),
        in_specs=[col(1), col(2), col(3), col(4), tab_rows, tab_rows, full(dmat), full(cross), full(kdec),
                  full(gn), state_spec],
        out_specs=(pl.BlockSpec((CHUNK, BRANCH_WIDTH), lambda b, i: (b * nb + i, 0)), state_spec),
        scratch_shapes=[pltpu.VMEM((RET_HEADS, RET_HEAD_DIM, RET_HEAD_DIM), F32)],
        compiler_params=_cparams(("parallel", "arbitrary")),
        name="retention",
    )(p3, p3, p3, p3, cos, sin, dmat, cross, kdec, gn, s0)


def _sortable(x):
    bits = pltpu.bitcast(x + 0.0, jnp.int32)
    return bits ^ ((bits >> 31) & 0x7FFFFFFF)


def _dsa_kernel(q_ref, qi_ref, w_ref, k_ref, vt_ref, ki_ref, bn_ref, ok_ref, o_ref,
                key_ref, keyn_ref, m_ref, den_ref, acc_ref, *, first_real, qb0, skip_below, k_sel):
    j = pl.program_id(1)

    @pl.when(j < skip_below)
    def _():
        o_ref[...] = jnp.zeros(o_ref.shape, o_ref.dtype)

    @pl.when(j >= skip_below)
    def _():
        win0 = pl.multiple_of(j * QT + (qb0 * CHUNK - (WN - QT)), LANES)
        n_far = (win0 + KEY_TILE - 1) // KEY_TILE
        near = pl.ds(win0, WN)

        def index_keys(ki, adm):
            acc = None
            for h in range(IDX_HEADS):
                term = w_ref[0, h:h + 1, :] * jnp.maximum(_dot(ki, qi_ref[0, h]), 0.0)
                acc = term if acc is None else acc + term
            return jnp.where(adm, _sortable(acc), INT_MIN)

        def far_keys(c, carry):
            c0 = pl.multiple_of(c * KEY_TILE, KEY_TILE)
            row = c0 + lax.broadcasted_iota(jnp.int32, (KEY_TILE, 1), 0)
            adm = jnp.logical_and(row >= first_real, row < win0)
            key_ref[c] = index_keys(ki_ref[0, pl.ds(c0, KEY_TILE), :], adm)
            return carry

        lax.fori_loop(0, n_far, far_keys, 0)
        row_n = win0 + lax.broadcasted_iota(jnp.int32, (WN, 1), 0)
        keyn_ref[...] = index_keys(ki_ref[0, near, :], jnp.logical_and(row_n >= first_real, ok_ref[...] != 0))

        def fold(hit):
            return jnp.sum(hit.reshape(hit.shape[0] // SUBLANES, SUBLANES, QT), axis=0)

        def count_ge(c):
            part = lax.fori_loop(0, n_far, lambda t, p: p + fold(jnp.where(key_ref[t] >= c, 1.0, 0.0)),
                                 fold(jnp.where(keyn_ref[...] >= c, 1.0, 0.0)))
            return jnp.sum(part, axis=0, keepdims=True)

        zero = jnp.zeros((1, QT), jnp.int32)
        thr0 = jnp.where(count_ge(zero) >= k_sel, zero, INT_MIN)

        def bit_step(it, thr):
            cand = thr + jnp.left_shift(jnp.int32(1), jnp.int32(30) - it)
            return jnp.where(count_ge(cand) >= k_sel, cand, thr)

        thr = lax.fori_loop(0, 31, bit_step, thr0)
        thr = jnp.maximum(thr, INT_MIN + 1)

        m_ref[...] = jnp.full(m_ref.shape, M_INIT, F32)
        den_ref[...] = jnp.zeros(den_ref.shape, F32)
        acc_ref[...] = jnp.zeros(acc_ref.shape, F32)
        scale = A_HEAD_DIM ** -0.5

        def attend(h, logits, sel, vt_tiles):
            lg = jnp.where(sel, logits, NEG)
            m_old = m_ref[h]
            m_new = jnp.maximum(m_old, jnp.max(lg, axis=0, keepdims=True))
            p = jnp.exp(lg - m_new)
            alpha = jnp.exp(m_old - m_new)
            den_ref[h] = alpha * den_ref[h] + jnp.sum(p, axis=0, keepdims=True)
            pb = p.astype(BF16)
            pv = None
            for u, vt in enumerate(vt_tiles):
                term = _dot(vt, pb[u * LANES:(u + 1) * LANES])
                pv = term if pv is None else pv + term
            acc_ref[h] = alpha * acc_ref[h] + pv
            m_ref[h] = m_new

        def far_tile(c, carry):
            c0 = pl.multiple_of(c * KEY_TILE, KEY_TILE)
            sel = key_ref[c] >= thr
            for h in range(A_HEADS):
                sl = slice(h * A_HEAD_DIM, (h + 1) * A_HEAD_DIM)
                logits = _dot(k_ref[0, pl.ds(c0, KEY_TILE), sl], q_ref[0, h]) * scale
                attend(h, logits, sel,
                       [vt_ref[0, c * (KEY_TILE // LANES) + u, sl, :] for u in range(KEY_TILE // LANES)])
            return carry

        lax.fori_loop(0, n_far, far_tile, 0)
        sel_n = keyn_ref[...] >= thr
        wt = win0 // LANES
        for h in range(A_HEADS):
            sl = slice(h * A_HEAD_DIM, (h + 1) * A_HEAD_DIM)
            logits = _dot(k_ref[0, near, sl], q_ref[0, h]) * scale + bn_ref[h]
            attend(h, logits, sel_n, [vt_ref[0, wt + u, sl, :] for u in range(WN // LANES)])
            out_t = acc_ref[h] / jnp.maximum(den_ref[h], TINY)
            o_ref[:, sl] = out_t.T.astype(o_ref.dtype)


def _dsa(q_rows, qi_rows, w_rows, k_rows, v_rows, ki_rows, bias_near, near_ok, *, first_real, qb0, skip_below, k_sel):
    batch, tq, width = q_rows.shape
    n_keys = k_rows.shape[1]
    nb = tq // QT
    q_t = jnp.transpose(q_rows.reshape(batch * nb, QT, A_HEADS, A_HEAD_DIM), (0, 2, 3, 1)).astype(BF16)
    qi_t = jnp.transpose(qi_rows.reshape(batch * nb, QT, IDX_HEADS, IDX_DIM), (0, 2, 3, 1)).astype(BF16)
    w_t = jnp.transpose(w_rows.reshape(batch * nb, QT, IDX_HEADS), (0, 2, 1)).astype(F32)
    v_t = jnp.transpose(v_rows.reshape(batch, n_keys // LANES, LANES, width), (0, 1, 3, 2)).astype(BF16)
    return pl.pallas_call(
        functools.partial(_dsa_kernel, first_real=first_real, qb0=qb0, skip_below=skip_below, k_sel=k_sel),
        out_shape=jax.ShapeDtypeStruct((batch * tq, width), BF16),
        grid=(batch, nb),
        in_specs=[
            pl.BlockSpec((1, A_HEADS, A_HEAD_DIM, QT), lambda b, j: (b * nb + j, 0, 0, 0)),
            pl.BlockSpec((1, IDX_HEADS, IDX_DIM, QT), lambda b, j: (b * nb + j, 0, 0, 0)),
            pl.BlockSpec((1, IDX_HEADS, QT), lambda b, j: (b * nb + j, 0, 0)),
            pl.BlockSpec((1, n_keys, width), lambda b, j: (b, 0, 0)),
            pl.BlockSpec((1, n_keys // LANES, width, LANES), lambda b, j: (b, 0, 0, 0)),
            pl.BlockSpec((1, n_keys, IDX_DIM), lambda b, j: (b, 0, 0)),
            pl.BlockSpec((A_HEADS, WN, QT), lambda b, j: (0, 0, 0)),
            pl.BlockSpec((WN, QT), lambda b, j: (0, 0)),
        ],
        out_specs=pl.BlockSpec((QT, width), lambda b, j: (b * nb + j, 0)),
        scratch_shapes=[pltpu.VMEM((n_keys // KEY_TILE, KEY_TILE, QT), jnp.int32),
                        pltpu.VMEM((WN, QT), jnp.int32),
                        pltpu.VMEM((A_HEADS, 1, QT), F32),
                        pltpu.VMEM((A_HEADS, 1, QT), F32),
                        pltpu.VMEM((A_HEADS, A_HEAD_DIM, QT), F32)],
        compiler_params=_cparams(("parallel", "arbitrary")),
        name="dsa",
    )(q_t, qi_t, w_t, k_rows.astype(BF16), v_t, ki_rows.astype(BF16), bias_near, near_ok)


def _t5_bucket(rel):
    half = T5_BUCKETS // 2
    exact = half // 2
    n = jnp.abs(rel)
    large = exact + (jnp.log(jnp.maximum(n, 1).astype(F32) / exact)
                     / math.log(T5_MAX_DIST / exact) * (half - exact)).astype(jnp.int32)
    large = jnp.minimum(large, half - 1)
    return jnp.where(rel > 0, half, 0) + jnp.where(n < exact, n, large)


def _bias_tables(t5_bias):
    a = jnp.arange(WN, dtype=jnp.int32)[:, None]
    t = jnp.arange(QT, dtype=jnp.int32)[None, :]
    near = jnp.transpose(t5_bias[_t5_bucket(a - (WN - QT) - t)], (2, 0, 1)).astype(F32)
    far = t5_bias[_t5_bucket(jnp.int32(-2 * CHUNK - 1))].astype(F32)
    ok = (a // CHUNK - (WN - QT) // CHUNK <= t // CHUNK).astype(jnp.int32)
    return near - far[:, None, None], ok


def _rope_tables(pos):
    half = RET_HEAD_DIM // 2
    inv = ROPE_BASE ** (-jnp.arange(half, dtype=F32) / half)
    ang = pos.astype(F32)[:, None] * inv[None, :]
    cos, sin = jnp.cos(ang), jnp.sin(ang)
    return jnp.concatenate([cos, cos], axis=-1), jnp.concatenate([-sin, sin], axis=-1)


def _decay_tables():
    n = CHUNK
    log_g = jnp.log(1.0 - 2.0 ** (-5.0 - jnp.arange(RET_HEADS, dtype=F32)))
    i = jnp.arange(n, dtype=F32)
    diff = i[:, None] - i[None, :]
    dmat = jnp.where(diff >= 0, jnp.exp(jnp.maximum(diff, 0.0)[None] * log_g[:, None, None]), 0.0)
    cross = jnp.exp((i[None, :] + 1.0) * log_g[:, None])
    kdec = jnp.exp((n - 1.0 - i)[None, :] * log_g[:, None])
    gn = jnp.exp(n * log_g)
    wide = (RET_HEADS, n, RET_HEAD_DIM)
    return (dmat, jnp.broadcast_to(cross[:, :, None], wide), jnp.broadcast_to(kdec[:, :, None], wide),
            jnp.broadcast_to(gn[:, None, None], (RET_HEADS, 1, RET_HEAD_DIM)))


def kernel(x_prompt, x_sample, cache_k, cache_v, cache_ki, cache_pool, state_ret, meta_tokens, ln_in_g, ln_in_b, w_in, t5_bias, pool_w, pool_scale, w_branch, w_gate, b_gate, w_out, ln1_g, ln1_b, ln2_g, ln2_b, ffn_w_gate, ffn_w_up, ffn_w_down, moe_w_router, moe_b_router, moe_w_gate, moe_w_up, moe_w_down):
    bp, seq, d = x_prompt.shape
    bs, ts, _ = x_sample.shape
    past = cache_k.shape[2]
    t_real = seq + N_META
    tp = -(-(t_real + 2 * CHUNK) // KEY_TILE) * KEY_TILE
    front = tp - t_real
    assert front % CHUNK == CHUNK - N_META and ts == CHUNK and past % CHUNK == 0
    mp, ms = bp * tp, bs * ts
    m = mp + ms
    assert mp % TM == 0 and ms % TM == 0
    ksel_p = min(TOPK_MAX, seq // 4)
    ksel_s = min(TOPK_MAX, (past + ts) // 4)
    s_front = (-(past + QT)) % KEY_TILE
    assert tp % QT == 0 and front >= WN - QT and (s_front + past) % QT == 0

    meta = jnp.broadcast_to(meta_tokens.astype(F32)[None], (bp, N_META, d))
    x_rows = jnp.concatenate(
        [jnp.concatenate([jnp.zeros((bp, front, d), F32), meta, x_prompt], axis=1).reshape(mp, d),
         x_sample.reshape(ms, d)], axis=0)
    x32, xb = _layer_norm(x_rows, None, ln_in_g, ln_in_b)

    bias_near, near_ok = _bias_tables(t5_bias)
    cos_p, sin_p = _rope_tables(jnp.arange(tp, dtype=jnp.int32) - (front + N_META))
    cos_s, sin_s = _rope_tables(past + jnp.arange(ts, dtype=jnp.int32))
    decay = _decay_tables()
    zero_state = jnp.zeros((bp, RET_HEADS, RET_HEAD_DIM, RET_HEAD_DIM), F32)
    w_tail = w_in[:, :, TAIL_OFF:]

    outs = {name: [] for name in ("kp", "vp", "kip", "poolp", "retp", "ks", "vs", "kis", "pools", "rets")}
    bw = BRANCH_WIDTH
    for l in range(DEPTH):
        p1 = _matmul(xb, w_in, (l,), 0, 4, bw, name="proj_attn")
        p2 = _matmul(xb, w_in, (l,), 4 * bw // LANES, 1, LANES, name="proj_idx")
        p3 = _matmul(xb, w_tail, (l,), 0, 5, bw, name="proj_tail")

        k_new, v_new, ki_new = p1[:, bw:2 * bw], p1[:, 2 * bw:3 * bw], p2[:, :IDX_DIM]
        q_a, q_i, w_i = p1[:, :bw], p1[:, 3 * bw:], p2[:, IDX_DIM:IDX_DIM + IDX_HEADS]

        def prompt_seq(a):
            return a[:mp].reshape(bp, tp, a.shape[1])

        def sample_queries(a):
            return jnp.pad(a[mp:].reshape(bs, ts, a.shape[1]), ((0, 0), (0, QT - ts), (0, 0)))

        def sample_keys(cache, new):
            width = new.shape[1]
            return jnp.pad(jnp.concatenate([cache.reshape(bs, past, width), new[mp:].reshape(bs, ts, width)], axis=1),
                           ((0, 0), (s_front, QT - ts), (0, 0)))

        oa_p = _dsa(prompt_seq(q_a), prompt_seq(q_i), prompt_seq(w_i), prompt_seq(k_new), prompt_seq(v_new),
                    prompt_seq(ki_new), bias_near, near_ok, first_real=front, qb0=0,
                    skip_below=max(1, front // QT), k_sel=ksel_p)
        oa_s = _dsa(sample_queries(q_a), sample_queries(q_i), sample_queries(w_i), sample_keys(cache_k[l], k_new),
                    sample_keys(cache_v[l], v_new), sample_keys(cache_ki[l], ki_new), bias_near, near_ok,
                    first_real=s_front, qb0=(s_front + past) // CHUNK, skip_below=0, k_sel=ksel_s)
        oa_s = oa_s.reshape(bs, QT, bw)[:, :ts].reshape(ms, bw)

        tb_p = 256
        ob_p = _pool(p3, 0, p3, 0, pool_w[l], pool_scale[l], batch=bp, rows=tp, tb=tb_p, row0=0,
                     halo_map=lambda b, i: jnp.maximum((b * tp + i * tb_p) // HALO - 1, 0), first_valid=front)
        pool_hist = jnp.pad(cache_pool[l], ((0, 0), (HALO - POOL_PAST, 0), (0, 0))).reshape(bs * HALO, bw)
        ob_s = _pool(p3, 0, pool_hist, 0, pool_w[l], pool_scale[l], batch=bs, rows=ts, tb=ts, row0=mp,
                     halo_map=lambda b, i: b, first_valid=-POOL_PAST)

        oc_p, ret_p = _retention(p3, zero_state, cos_p, sin_p, decay, batch=bp, rows=tp, row0=0, first_valid=front)
        oc_s, ret_s = _retention(p3, state_ret[l].astype(F32), cos_s, sin_s, decay, batch=bs, rows=ts, row0=mp,
                                 first_valid=0)

        merged = _merge(l, xb, jnp.concatenate([oa_p, oa_s]), jnp.concatenate([ob_p, ob_s]),
                        jnp.concatenate([oc_p, oc_s]), w_branch, w_gate, b_gate)
        y = _matmul(merged, w_out, (l,), 0, d // 512, 512, name="w_out")
        x32, xb = _layer_norm(x32, y, ln1_g[l], ln1_b[l])

        if l % 2 == 0:
            h = _gate_up(xb, ffn_w_gate, ffn_w_up, (l // 2,))
            f = _matmul(h, ffn_w_down, (l // 2,), 0, d // 512, 512, name="ffn_down")
        else:
            gates, experts = _router(x32, moe_w_router[l // 2], moe_b_router[l // 2])
            src, gate_sorted, pos, tile_expert, n_used = _dispatch_plan(experts, gates, MOE_TILE)
            x_sorted = _gather_rows(x32, src, MOE_TILE)
            y_sorted = _moe_experts(x_sorted, gate_sorted, tile_expert, n_used, moe_w_gate, moe_w_up, moe_w_down,
                                    l // 2, MOE_TILE)
            f = _combine_rows(y_sorted, pos, m, MOE_TILE)
        x32, xb = _layer_norm(x32, f, ln2_g[l], ln2_b[l])

        def prompt_rows(a, width):
            return a[:mp].reshape(bp, tp, width)[:, front:]

        u = p3[:, :bw]
        outs["kp"].append(prompt_rows(k_new, bw).reshape(bp, t_real, A_HEADS, A_HEAD_DIM))
        outs["vp"].append(prompt_rows(v_new, bw).reshape(bp, t_real, A_HEADS, A_HEAD_DIM))
        outs["kip"].append(prompt_rows(ki_new, IDX_DIM))
        outs["poolp"].append(prompt_rows(u, bw)[:, -POOL_PAST:])
        outs["retp"].append(ret_p)
        outs["ks"].append(k_new[mp:].reshape(bs, ts, A_HEADS, A_HEAD_DIM))
        outs["vs"].append(v_new[mp:].reshape(bs, ts, A_HEADS, A_HEAD_DIM))
        outs["kis"].append(ki_new[mp:].reshape(bs, ts, IDX_DIM))
        outs["pools"].append(u[mp:].reshape(bs, ts, bw)[:, -POOL_PAST:])
        outs["rets"].append(ret_s)

    y_prompt = x32[:mp].reshape(bp, tp, d)[:, front + N_META:]
    y_sample = x32[mp:].reshape(bs, ts, d)
    return (y_prompt, y_sample) + tuple(
        jnp.stack(outs[name]) for name in ("kp", "vp", "kip", "poolp", "retp", "ks", "vs", "kis", "pools", "rets"))
```

```python
import functools
import math

import jax
import jax.numpy as jnp
from jax import lax
from jax.experimental import pallas as pl
from jax.experimental.pallas import tpu as pltpu

F32 = jnp.float32
BF16 = jnp.bfloat16

D_MODEL = 2048
DEPTH = 2
CHUNK = 64
N_META = 16
BRANCH_WIDTH = D_MODEL // 2
A_HEADS = 8
A_HEAD_DIM = BRANCH_WIDTH // A_HEADS
IDX_HEADS = 16
IDX_DIM = 64
TOPK_MAX = 256
T5_BUCKETS = 32
T5_MAX_DIST = 128
POOL_WINDOWS = (2, 4, 8, 16)
POOL_GROUP = BRANCH_WIDTH // 4
POOL_PAST = 15
RET_HEADS = 8
RET_HEAD_DIM = BRANCH_WIDTH // RET_HEADS
ROPE_BASE = 10000.0
N_BRANCH = 3
D_FF = 11 * D_MODEL // 4
N_EXPERTS = 8
TOP_K = 2
ALPHA = (2 * DEPTH) ** 0.25
LN_EPS = 1e-5
IN_SPLITS = (BRANCH_WIDTH, BRANCH_WIDTH, BRANCH_WIDTH, IDX_HEADS * IDX_DIM, IDX_DIM, IDX_HEADS,
             BRANCH_WIDTH, BRANCH_WIDTH, BRANCH_WIDTH, BRANCH_WIDTH, BRANCH_WIDTH)
IN_WIDTH = sum(IN_SPLITS)
TAIL_OFF = 4 * BRANCH_WIDTH + IDX_DIM + IDX_HEADS

LANES = 128
SUBLANES = 8
HALO = 16
QT = 2 * CHUNK
WN = QT + 2 * CHUNK
KEY_TILE = 256
VMEM_LIMIT = 56 * 1024 * 1024
TM = 512
MOE_TILE = 256
INT_MIN = -2 ** 31
NEG = -1e30
M_INIT = -1e20
TINY = 1e-30


def _cparams(sem):
    return pltpu.CompilerParams(dimension_semantics=sem, vmem_limit_bytes=VMEM_LIMIT)


def _dot(a, b):
    return jnp.dot(a, b, preferred_element_type=F32)


def _dot_nt(a, b):
    return lax.dot_general(a, b, (((1,), (1,)), ((), ())), preferred_element_type=F32)


def _dot_tn(a, b):
    return lax.dot_general(a, b, (((0,), (0,)), ((), ())), preferred_element_type=F32)


def _ln_kernel(x_ref, g_ref, b_ref, o32_ref, o16_ref):
    x = x_ref[...]
    mu = jnp.mean(x, axis=-1, keepdims=True)
    xc = x - mu
    var = jnp.mean(xc * xc, axis=-1, keepdims=True)
    y = xc * lax.rsqrt(var + LN_EPS) * g_ref[...] + b_ref[...]
    o32_ref[...] = y
    o16_ref[...] = y.astype(BF16)


def _ln_res_kernel(x_ref, y_ref, g_ref, b_ref, o32_ref, o16_ref):
    x = ALPHA * x_ref[...] + y_ref[...]
    mu = jnp.mean(x, axis=-1, keepdims=True)
    xc = x - mu
    var = jnp.mean(xc * xc, axis=-1, keepdims=True)
    y = xc * lax.rsqrt(var + LN_EPS) * g_ref[...] + b_ref[...]
    o32_ref[...] = y
    o16_ref[...] = y.astype(BF16)


def _layer_norm(x, y, g, b, tm=256):
    m, d = x.shape
    row = pl.BlockSpec((tm, d), lambda i: (i, 0))
    vec = pl.BlockSpec((1, d), lambda i: (0, 0))
    args = (x,) if y is None else (x, y)
    return pl.pallas_call(
        _ln_kernel if y is None else _ln_res_kernel,
        out_shape=(jax.ShapeDtypeStruct((m, d), F32), jax.ShapeDtypeStruct((m, d), BF16)),
        grid=(m // tm,),
        in_specs=[row] * len(args) + [vec, vec],
        out_specs=(row, row),
        compiler_params=_cparams(("parallel",)),
        name="layer_norm",
    )(*args, g.reshape(1, d), b.reshape(1, d))


def _mm_kernel(x_ref, w_ref, o_ref, wb_ref):
    @pl.when(pl.program_id(1) == 0)
    def _():
        wb_ref[...] = w_ref[...].astype(BF16)

    o_ref[...] = _dot(x_ref[...], wb_ref[...]).astype(o_ref.dtype)


def _matmul(x, w, lead, col0, n_tiles, tn, name="matmul"):
    m, k = x.shape
    nl = len(lead)
    w_spec = pl.BlockSpec((None,) * nl + (k, tn), lambda j, i: tuple(lead) + (0, j + col0))
    x_spec = pl.BlockSpec((TM, k), lambda j, i: (i, 0))
    o_spec = pl.BlockSpec((TM, tn), lambda j, i: (i, j))
    return pl.pallas_call(
        _mm_kernel,
        out_shape=jax.ShapeDtypeStruct((m, n_tiles * tn), F32),
        grid=(n_tiles, m // TM),
        in_specs=[x_spec, w_spec],
        out_specs=o_spec,
        scratch_shapes=[pltpu.VMEM((k, tn), BF16)],
        compiler_params=_cparams(("arbitrary", "arbitrary")),
        name=name,
    )(x, w)


def _store_lane_tiles_t(res, out_ref):
    rows, width = res.shape
    for r in range(rows // LANES):
        for c in range(width // LANES):
            tile = res[r * LANES:(r + 1) * LANES, c * LANES:(c + 1) * LANES]
            out_ref[r, c] = tile.T.astype(out_ref.dtype)


def _proj_kernel(x_ref, w_ref, *refs, nat32, nat16, trans):
    outs, wb_ref = list(refs[:-1]), refs[-1]

    @pl.when(pl.program_id(0) == 0)
    def _():
        wb_ref[...] = w_ref[...].astype(BF16)

    res = _dot(x_ref[...], wb_ref[...])
    if nat32:
        outs.pop(0)[...] = res
    if nat16:
        outs.pop(0)[...] = res.astype(BF16)
    if trans:
        _store_lane_tiles_t(res, outs.pop(0))


def _proj(x, w, lead, col0, *, nat32=False, nat16=False, trans=False, tm=TM, name="proj"):
    m, k = x.shape
    bw = BRANCH_WIDTH
    nl = len(lead)
    shapes, specs = [], []
    row_spec = pl.BlockSpec((tm, bw), lambda i: (i, 0))
    if nat32:
        shapes.append(jax.ShapeDtypeStruct((m, bw), F32)); specs.append(row_spec)
    if nat16:
        shapes.append(jax.ShapeDtypeStruct((m, bw), BF16)); specs.append(row_spec)
    if trans:
        shapes.append(jax.ShapeDtypeStruct((m // LANES, bw // LANES, LANES, LANES), BF16))
        specs.append(pl.BlockSpec((tm // LANES, bw // LANES, LANES, LANES), lambda i: (i, 0, 0, 0)))
    return pl.pallas_call(
        functools.partial(_proj_kernel, nat32=nat32, nat16=nat16, trans=trans),
        out_shape=tuple(shapes),
        grid=(m // tm,),
        in_specs=[pl.BlockSpec((tm, k), lambda i: (i, 0)),
                  pl.BlockSpec((None,) * nl + (k, bw), lambda i: tuple(lead) + (0, col0))],
        out_specs=tuple(specs),
        scratch_shapes=[pltpu.VMEM((k, bw), BF16)],
        compiler_params=_cparams(("arbitrary",)),
        name=name,
    )(x, w)


def _proj_idx_kernel(x_ref, w_ref, nat_ref, ki_ref, wt_ref, wb_ref):
    @pl.when(pl.program_id(0) == 0)
    def _():
        wb_ref[...] = w_ref[...].astype(BF16)

    res = _dot(x_ref[...], wb_ref[...])
    nat_ref[...] = res
    ki_ref[...] = res[:, :IDX_DIM].astype(BF16)
    for r in range(res.shape[0] // LANES):
        wt_ref[r] = res[r * LANES:(r + 1) * LANES, :].T[IDX_DIM:IDX_DIM + IDX_HEADS, :]


def _proj_idx(x, w, lead, col_block, tm=256):
    m, k = x.shape
    nl = len(lead)
    return pl.pallas_call(
        _proj_idx_kernel,
        out_shape=(jax.ShapeDtypeStruct((m, LANES), F32), jax.ShapeDtypeStruct((m, IDX_DIM), BF16),
                   jax.ShapeDtypeStruct((m // LANES, IDX_HEADS, LANES), F32)),
        grid=(m // tm,),
        in_specs=[pl.BlockSpec((tm, k), lambda i: (i, 0)),
                  pl.BlockSpec((None,) * nl + (k, LANES), lambda i: tuple(lead) + (0, col_block))],
        out_specs=(pl.BlockSpec((tm, LANES), lambda i: (i, 0)), pl.BlockSpec((tm, IDX_DIM), lambda i: (i, 0)),
                   pl.BlockSpec((tm // LANES, IDX_HEADS, LANES), lambda i: (i, 0, 0))),
        scratch_shapes=[pltpu.VMEM((k, LANES), BF16)],
        compiler_params=_cparams(("arbitrary",)),
        name="proj_idx",
    )(x, w)


def _gate_up_kernel(x_ref, wg_ref, wu_ref, o_ref, wgb_ref, wub_ref):
    @pl.when(pl.program_id(1) == 0)
    def _():
        wgb_ref[...] = wg_ref[...].astype(BF16)
        wub_ref[...] = wu_ref[...].astype(BF16)

    x = x_ref[...]
    g = _dot(x, wgb_ref[...])
    u = _dot(x, wub_ref[...])
    o_ref[...] = (g * jax.nn.sigmoid(g) * u).astype(o_ref.dtype)


def _gate_up(x, wg, wu, lead, tn=512):
    m, k = x.shape
    n = wg.shape[-1]
    nl = len(lead)
    w_spec = pl.BlockSpec((None,) * nl + (k, tn), lambda j, i: tuple(lead) + (0, j))
    x_spec = pl.BlockSpec((TM, k), lambda j, i: (i, 0))
    return pl.pallas_call(
        _gate_up_kernel,
        out_shape=jax.ShapeDtypeStruct((m, n), BF16),
        grid=(n // tn, m // TM),
        in_specs=[x_spec, w_spec, w_spec],
        out_specs=pl.BlockSpec((TM, tn), lambda j, i: (i, j)),
        scratch_shapes=[pltpu.VMEM((k, tn), BF16), pltpu.VMEM((k, tn), BF16)],
        compiler_params=_cparams(("arbitrary", "arbitrary")),
        name="gate_up",
    )(x, wg, wu)


def _merge_kernel(x_ref, oa_ref, ob_ref, oc_ref, wg0_ref, wg1_ref, wg2_ref, wb0_ref, wb1_ref, wb2_ref,
                  bg0_ref, bg1_ref, bg2_ref, o_ref, wgb_ref, wbb_ref):
    wg_refs = (wg0_ref, wg1_ref, wg2_ref)
    wb_refs = (wb0_ref, wb1_ref, wb2_ref)

    @pl.when(pl.program_id(1) == 0)
    def _():
        for n in range(N_BRANCH):
            wgb_ref[n] = wg_refs[n][...].astype(BF16)
            wbb_ref[n] = wb_refs[n][...].astype(BF16)

    x = x_ref[...]
    acc = None
    for n, (o_in, bg) in enumerate(zip((oa_ref, ob_ref, oc_ref), (bg0_ref, bg1_ref, bg2_ref))):
        gate = jax.nn.sigmoid(_dot(x, wgb_ref[n]) + bg[...])
        term = gate * _dot(o_in[...], wbb_ref[n])
        acc = term if acc is None else acc + term
    o_ref[...] = acc.astype(o_ref.dtype)


def _merge(l, x, oa, ob, oc, w_branch, w_gate, b_gate, tn=256):
    m, d = x.shape
    w = oa.shape[1]
    nt = d // tn
    x_spec = pl.BlockSpec((TM, d), lambda j, i: (i, 0))
    o_in_spec = pl.BlockSpec((TM, w), lambda j, i: (i, 0))
    wg_specs = [pl.BlockSpec((None, d, tn), lambda j, i, n=n: (l, 0, n * nt + j)) for n in range(N_BRANCH)]
    wb_specs = [pl.BlockSpec((None, None, w, tn), lambda j, i, n=n: (l, n, 0, j)) for n in range(N_BRANCH)]
    bg_specs = [pl.BlockSpec((None, 1, tn), lambda j, i, n=n: (l, 0, n * nt + j)) for n in range(N_BRANCH)]
    return pl.pallas_call(
        _merge_kernel,
        out_shape=jax.ShapeDtypeStruct((m, d), BF16),
        grid=(nt, m // TM),
        in_specs=[x_spec, o_in_spec, o_in_spec, o_in_spec] + wg_specs + wb_specs + bg_specs,
        out_specs=pl.BlockSpec((TM, tn), lambda j, i: (i, j)),
        scratch_shapes=[pltpu.VMEM((N_BRANCH, d, tn), BF16), pltpu.VMEM((N_BRANCH, w, tn), BF16)],
        compiler_params=_cparams(("arbitrary", "arbitrary")),
        name="merge",
    )(x, oa, ob, oc, w_gate, w_gate, w_gate, w_branch, w_branch, w_branch,
      b_gate.reshape(DEPTH, 1, N_BRANCH * d), b_gate.reshape(DEPTH, 1, N_BRANCH * d),
      b_gate.reshape(DEPTH, 1, N_BRANCH * d))


def _split_bf16(a):
    hi = a.astype(BF16)
    lo = (a - hi.astype(F32)).astype(BF16)
    return hi, lo


def _router_kernel(x_ref, w_ref, b_ref, gate_ref, expert_ref):
    xh, xl = _split_bf16(x_ref[...])
    wh, wl = _split_bf16(w_ref[...])
    logits = _dot(xh, wh) + (_dot(xh, wl) + _dot(xl, wh)) + b_ref[...]
    lane = lax.broadcasted_iota(jnp.int32, logits.shape, 1)
    logits = jnp.where(lane < N_EXPERTS, logits, -jnp.inf)
    m1 = jnp.max(logits, axis=-1, keepdims=True)
    i1 = jnp.min(jnp.where(logits == m1, lane, LANES), axis=-1, keepdims=True)
    rest = jnp.where(lane == i1, -jnp.inf, logits)
    m2 = jnp.max(rest, axis=-1, keepdims=True)
    i2 = jnp.min(jnp.where(rest == m2, lane, LANES), axis=-1, keepdims=True)
    e = jnp.exp(m2 - m1)
    p1 = 1.0 / (1.0 + e)
    p2 = e / (1.0 + e)
    gate_ref[...] = jnp.where(lane == 0, p1, jnp.where(lane == 1, p2, 0.0))
    expert_ref[...] = jnp.where(lane == 0, i1, jnp.where(lane == 1, i2, 0))


def _router(x32, w_r, b_r, tm=256):
    m, d = x32.shape
    w_pad = jnp.pad(w_r, ((0, 0), (0, LANES - N_EXPERTS)))
    b_pad = jnp.pad(b_r, (0, LANES - N_EXPERTS)).reshape(1, LANES)
    out_spec = pl.BlockSpec((tm, LANES), lambda i: (i, 0))
    gates, experts = pl.pallas_call(
        _router_kernel,
        out_shape=(jax.ShapeDtypeStruct((m, LANES), F32), jax.ShapeDtypeStruct((m, LANES), jnp.int32)),
        grid=(m // tm,),
        in_specs=[pl.BlockSpec((tm, d), lambda i: (i, 0)), pl.BlockSpec((d, LANES), lambda i: (0, 0)),
                  pl.BlockSpec((1, LANES), lambda i: (0, 0))],
        out_specs=(out_spec, out_spec),
        compiler_params=_cparams(("parallel",)),
        name="router",
    )(x32, w_pad, b_pad)
    return gates[:, :TOP_K], experts[:, :TOP_K]


def _dispatch_plan(experts, gates, tile):
    m = experts.shape[0]
    n_assign = TOP_K * m
    n_tiles = n_assign // tile + N_EXPERTS
    e = experts.reshape(n_assign)
    onehot = (e[:, None] == jnp.arange(N_EXPERTS, dtype=jnp.int32)[None, :]).astype(jnp.int32)
    rank = jnp.sum((jnp.cumsum(onehot, axis=0) - onehot) * onehot, axis=1)
    tiles_per = (jnp.sum(onehot, axis=0) + tile - 1) // tile
    tile_end = jnp.cumsum(tiles_per)
    pos = ((tile_end - tiles_per)[e] * tile + rank).astype(jnp.int32)
    src = jnp.zeros((n_tiles * tile,), jnp.int32).at[pos].set(jnp.arange(n_assign, dtype=jnp.int32) // TOP_K)
    gate = jnp.zeros((n_tiles * tile,), F32).at[pos].set(gates.reshape(n_assign))
    n_used = tile_end[-1:]
    t_idx = jnp.minimum(jnp.arange(n_tiles, dtype=jnp.int32), n_used - 1)
    tile_expert = jnp.sum((t_idx[:, None] >= tile_end[None, :]).astype(jnp.int32), axis=1)
    return src, gate.reshape(-1, 1), pos, tile_expert.astype(jnp.int32), n_used.astype(jnp.int32)


def _issue_rows(idx_ref, first, count, stride, src_hbm, dst, sem):
    def body(i, carry):
        pltpu.make_async_copy(src_hbm.at[pl.ds(idx_ref[first + i * stride], 1), :], dst.at[pl.ds(i, 1), :],
                              sem).start()
        return carry

    lax.fori_loop(0, count, body, 0, unroll=8)


def _wait_rows(count, src_hbm, dst, sem):
    def body(i, carry):
        pltpu.make_async_copy(src_hbm.at[pl.ds(0, 1), :], dst.at[pl.ds(i, 1), :], sem).wait()
        return carry

    lax.fori_loop(0, count, body, 0, unroll=8)


def _gather_kernel(src_ref, x_hbm, o_ref, buf, sem, *, tile):
    t = pl.program_id(0)

    @pl.when(t == 0)
    def _():
        _issue_rows(src_ref, 0, tile, 1, x_hbm, buf.at[0], sem.at[0])

    @pl.when(t + 1 < pl.num_programs(0))
    def _():
        nxt = (t + 1) % 2
        _issue_rows(src_ref, (t + 1) * tile, tile, 1, x_hbm, buf.at[nxt], sem.at[nxt])

    cur = t % 2
    _wait_rows(tile, x_hbm, buf.at[cur], sem.at[cur])
    o_ref[...] = buf[cur].astype(o_ref.dtype)


def _gather_rows(x32, src, tile):
    d = x32.shape[1]
    n_rows = src.shape[0]
    return pl.pallas_call(
        functools.partial(_gather_kernel, tile=tile),
        out_shape=jax.ShapeDtypeStruct((n_rows, d), BF16),
        grid_spec=pltpu.PrefetchScalarGridSpec(
            num_scalar_prefetch=1, grid=(n_rows // tile,),
            in_specs=[pl.BlockSpec(memory_space=pl.ANY)],
            out_specs=pl.BlockSpec((tile, d), lambda t, src: (t, 0)),
            scratch_shapes=[pltpu.VMEM((2, tile, d), F32), pltpu.SemaphoreType.DMA((2,))]),
        compiler_params=_cparams(("arbitrary",)),
        name="moe_gather",
    )(src, x32)


def _combine_kernel(pos_ref, y_hbm, o_ref, buf, sem, *, tile):
    t = pl.program_id(0)

    def issue(tile_idx, slot):
        for s in range(TOP_K):
            _issue_rows(pos_ref, tile_idx * tile * TOP_K + s, tile, TOP_K, y_hbm, buf.at[slot, s], sem.at[slot])

    @pl.when(t == 0)
    def _():
        issue(0, 0)

    @pl.when(t + 1 < pl.num_programs(0))
    def _():
        issue(t + 1, (t + 1) % 2)

    cur = t % 2
    for s in range(TOP_K):
        _wait_rows(tile, y_hbm, buf.at[cur, s], sem.at[cur])
    total = buf[cur, 0]
    for s in range(1, TOP_K):
        total = total + buf[cur, s]
    o_ref[...] = total


def _combine_rows(y_sorted, pos, m, tile):
    d = y_sorted.shape[1]
    return pl.pallas_call(
        functools.partial(_combine_kernel, tile=tile),
        out_shape=jax.ShapeDtypeStruct((m, d), F32),
        grid_spec=pltpu.PrefetchScalarGridSpec(
            num_scalar_prefetch=1, grid=(m // tile,),
            in_specs=[pl.BlockSpec(memory_space=pl.ANY)],
            out_specs=pl.BlockSpec((tile, d), lambda t, pos: (t, 0)),
            scratch_shapes=[pltpu.VMEM((2, TOP_K, tile, d), F32), pltpu.SemaphoreType.DMA((2,))]),
        compiler_params=_cparams(("arbitrary",)),
        name="moe_combine",
    )(pos, y_sorted)


def _expert_changed(te_ref, t):
    return jnp.logical_or(t == 0, te_ref[t] != te_ref[jnp.maximum(t - 1, 0)])


def _moe_gate_up_kernel(te_ref, nu_ref, x_ref, wg_ref, wu_ref, o_ref, wgb_ref, wub_ref):
    t = pl.program_id(1)

    @pl.when(_expert_changed(te_ref, t))
    def _():
        wgb_ref[...] = wg_ref[...].astype(BF16)
        wub_ref[...] = wu_ref[...].astype(BF16)

    @pl.when(t < nu_ref[0])
    def _():
        x = x_ref[...]
        g = _dot(x, wgb_ref[...])
        u = _dot(x, wub_ref[...])
        o_ref[...] = (g * jax.nn.sigmoid(g) * u).astype(o_ref.dtype)

    @pl.when(t >= nu_ref[0])
    def _():
        o_ref[...] = jnp.zeros(o_ref.shape, o_ref.dtype)


def _moe_down_kernel(te_ref, nu_ref, h_ref, w_ref, g_ref, o_ref, wb_ref):
    t = pl.program_id(1)

    @pl.when(_expert_changed(te_ref, t))
    def _():
        wb_ref[...] = w_ref[...].astype(BF16)

    @pl.when(t < nu_ref[0])
    def _():
        o_ref[...] = _dot(h_ref[...], wb_ref[...]) * g_ref[...]

    @pl.when(t >= nu_ref[0])
    def _():
        o_ref[...] = jnp.zeros(o_ref.shape, o_ref.dtype)


def _moe_experts(x_sorted, gate_sorted, tile_expert, n_used, w_gate, w_up, w_down, layer, tile, tn=512):
    r, d = x_sorted.shape
    ff = w_gate.shape[-1]
    n_tiles = r // tile

    def x_rows(width):
        return pl.BlockSpec((tile, width), lambda j, t, te, nu: (jnp.minimum(t, nu[0] - 1), 0))

    def w_cols(k):
        return pl.BlockSpec((None, None, k, tn), lambda j, t, te, nu: (layer, te[t], 0, j))

    h = pl.pallas_call(
        _moe_gate_up_kernel,
        out_shape=jax.ShapeDtypeStruct((r, ff), BF16),
        grid_spec=pltpu.PrefetchScalarGridSpec(
            num_scalar_prefetch=2, grid=(ff // tn, n_tiles),
            in_specs=[x_rows(d), w_cols(d), w_cols(d)],
            out_specs=pl.BlockSpec((tile, tn), lambda j, t, te, nu: (t, j)),
            scratch_shapes=[pltpu.VMEM((d, tn), BF16), pltpu.VMEM((d, tn), BF16)]),
        compiler_params=_cparams(("arbitrary", "arbitrary")),
        name="moe_gate_up",
    )(tile_expert, n_used, x_sorted, w_gate, w_up)
    return pl.pallas_call(
        _moe_down_kernel,
        out_shape=jax.ShapeDtypeStruct((r, d), F32),
        grid_spec=pltpu.PrefetchScalarGridSpec(
            num_scalar_prefetch=2, grid=(d // tn, n_tiles),
            in_specs=[x_rows(ff), w_cols(ff),
                      pl.BlockSpec((tile, 1), lambda j, t, te, nu: (jnp.minimum(t, nu[0] - 1), 0))],
            out_specs=pl.BlockSpec((tile, tn), lambda j, t, te, nu: (t, j)),
            scratch_shapes=[pltpu.VMEM((ff, tn), BF16)]),
        compiler_params=_cparams(("arbitrary", "arbitrary")),
        name="moe_down",
    )(tile_expert, n_used, h, w_down, gate_sorted)


def _pool_kernel(u_ref, halo_ref, w_ref, sc_ref, _buf_ref, o_ref, ext_ref, *, tb, first_valid):
    row0 = pl.program_id(1) * tb
    r_cur = row0 + lax.broadcasted_iota(jnp.int32, (tb, 1), 0)
    r_halo = row0 - HALO + lax.broadcasted_iota(jnp.int32, (HALO, 1), 0)
    u = jnp.where(r_cur >= first_valid, u_ref[...], 0.0)
    ext_ref[0:HALO, :] = jnp.where(r_halo >= first_valid, halo_ref[...], 0.0)
    ext_ref[HALO:, :] = u
    seen = (r_cur - first_valid + 1).astype(F32)
    for g, win in enumerate(POOL_WINDOWS):
        c0, c1 = g * POOL_GROUP, (g + 1) * POOL_GROUP
        s = u[:, c0:c1]
        for back in range(1, win):
            s = s + ext_ref[HALO - back:HALO - back + tb, c0:c1]
        cnt = jnp.clip(seen, 1.0, float(win))
        diff = s / cnt - u[:, c0:c1]
        y = _dot(diff.astype(BF16), w_ref[g].astype(BF16))
        o_ref[:, c0:c1] = (y * sc_ref[:, c0:c1]).astype(o_ref.dtype)


def _pool(u2d, col_block, halo2d, halo_col_block, pool_w_l, pool_scale_l, out_buf, *, batch, rows, tb, row0,
          halo_map, first_valid):
    nb = rows // tb
    base = row0 // tb
    return pl.pallas_call(
        functools.partial(_pool_kernel, tb=tb, first_valid=first_valid),
        out_shape=jax.ShapeDtypeStruct(out_buf.shape, out_buf.dtype),
        grid=(batch, nb),
        in_specs=[
            pl.BlockSpec((tb, BRANCH_WIDTH), lambda b, i: (base + b * nb + i, col_block)),
            pl.BlockSpec((HALO, BRANCH_WIDTH), lambda b, i: (halo_map(b, i), halo_col_block)),
            pl.BlockSpec((len(POOL_WINDOWS), POOL_GROUP, POOL_GROUP), lambda b, i: (0, 0, 0)),
            pl.BlockSpec((1, BRANCH_WIDTH), lambda b, i: (0, 0)),
            pl.BlockSpec(memory_space=pl.ANY),
        ],
        out_specs=pl.BlockSpec((tb, BRANCH_WIDTH), lambda b, i: (base + b * nb + i, 0)),
        scratch_shapes=[pltpu.VMEM((HALO + tb, BRANCH_WIDTH), F32)],
        input_output_aliases={4: 0},
        compiler_params=_cparams(("parallel", "arbitrary")),
        name="pool",
    )(u2d, halo2d, pool_w_l, pool_scale_l.reshape(1, BRANCH_WIDTH), out_buf)


def _ret_kernel(q_ref, k_ref, v_ref, g_ref, cos_ref, sin_ref, dmat_ref, cross_ref, kdec_ref, gn_ref, s0_ref,
                _buf_ref, o_ref, s_out_ref, s_scr, *, first_valid):
    i = pl.program_id(1)

    @pl.when(i == 0)
    def _():
        s_scr[...] = s0_ref[...]

    rows = i * CHUNK + lax.broadcasted_iota(jnp.int32, (CHUNK, 1), 0)
    valid = rows >= first_valid
    cos = cos_ref[...]
    sin = sin_ref[...]
    rscale = RET_HEAD_DIM ** -0.5
    half = RET_HEAD_DIM // 2
    for h in range(RET_HEADS):
        sl = slice(h * RET_HEAD_DIM, (h + 1) * RET_HEAD_DIM)
        q = q_ref[:, sl]
        k = k_ref[:, sl]
        v = jnp.where(valid, v_ref[:, sl], 0.0).astype(BF16)
        qr = (q * cos + pltpu.roll(q, half, 1) * sin).astype(BF16)
        kr = jnp.where(valid, (k * cos + pltpu.roll(k, half, 1) * sin) * rscale, 0.0)
        state = s_scr[h]
        inner = _dot_nt(qr, kr.astype(BF16)) * dmat_ref[h]
        o = _dot(inner.astype(BF16), v) + _dot(qr, state.astype(BF16)) * cross_ref[h]
        s_scr[h] = gn_ref[h] * state + _dot_tn((kr * kdec_ref[h]).astype(BF16), v)
        mu = jnp.mean(o, axis=-1, keepdims=True)
        oc = o - mu
        var = jnp.mean(oc * oc, axis=-1, keepdims=True)
        gate = g_ref[:, sl]
        o_ref[:, sl] = (gate * jax.nn.sigmoid(gate) * (oc * lax.rsqrt(var + LN_EPS))).astype(o_ref.dtype)

    @pl.when(i == pl.num_programs(1) - 1)
    def _():
        s_out_ref[...] = s_scr[...]


def _retention(p3, s0, cos, sin, tabs, out_buf, *, batch, rows, row0, first_valid):
    nb = rows // CHUNK
    base = row0 // CHUNK
    dmat, cross, kdec, gn = tabs

    def col(c):
        return pl.BlockSpec((CHUNK, BRANCH_WIDTH), lambda b, i: (base + b * nb + i, c))

    tab_rows = pl.BlockSpec((CHUNK, RET_HEAD_DIM), lambda b, i: (i, 0))

    def full(a):
        return pl.BlockSpec(a.shape, lambda b, i: (0,) * a.ndim)

    state_spec = pl.BlockSpec((None, RET_HEADS, RET_HEAD_DIM, RET_HEAD_DIM), lambda b, i: (b, 0, 0, 0))
    return pl.pallas_call(
        functools.partial(_ret_kernel, first_valid=first_valid),
        out_shape=(jax.ShapeDtypeStruct(out_buf.shape, out_buf.dtype),
                   jax.ShapeDtypeStruct((batch, RET_HEADS, RET_HEAD_DIM, RET_HEAD_DIM), F32)),
        grid=(batch, nb),
        in_specs=[col(1), col(2), col(3), col(4), tab_rows, tab_rows, full(dmat), full(cross), full(kdec),
                  full(gn), state_spec, pl.BlockSpec(memory_space=pl.ANY)],
        out_specs=(pl.BlockSpec((CHUNK, BRANCH_WIDTH), lambda b, i: (base + b * nb + i, 0)), state_spec),
        scratch_shapes=[pltpu.VMEM((RET_HEADS, RET_HEAD_DIM, RET_HEAD_DIM), F32)],
        input_output_aliases={11: 0},
        compiler_params=_cparams(("parallel", "arbitrary")),
        name="retention",
    )(p3, p3, p3, p3, cos, sin, dmat, cross, kdec, gn, s0, out_buf)


def _sortable(x):
    bits = pltpu.bitcast(x + 0.0, jnp.int32)
    return bits ^ ((bits >> 31) & 0x7FFFFFFF)


def _dsa_kernel(q_ref, qi_ref, w_ref, k_ref, vt_ref, ki_ref, bn_ref, ok_ref, _buf_ref, o_ref,
                key_ref, keyn_ref, m_ref, den_ref, acc_ref, *, first_real, qb0, skip_below, k_sel):
    j = pl.program_id(1)

    @pl.when(j < skip_below)
    def _():
        o_ref[...] = jnp.zeros(o_ref.shape, o_ref.dtype)

    @pl.when(j >= skip_below)
    def _():
        win0 = pl.multiple_of(j * QT + (qb0 * CHUNK - (WN - QT)), LANES)
        n_far = (win0 + KEY_TILE - 1) // KEY_TILE
        near = pl.ds(win0, WN)

        def index_keys(ki, adm):
            acc = None
            for h in range(IDX_HEADS):
                term = w_ref[0, h:h + 1, :] * jnp.maximum(_dot(ki, qi_ref[0, h]), 0.0)
                acc = term if acc is None else acc + term
            return jnp.where(adm, _sortable(acc), INT_MIN)

        def far_keys(c, carry):
            c0 = pl.multiple_of(c * KEY_TILE, KEY_TILE)
            row = c0 + lax.broadcasted_iota(jnp.int32, (KEY_TILE, 1), 0)
            adm = jnp.logical_and(row >= first_real, row < win0)
            key_ref[c] = index_keys(ki_ref[pl.ds(c0, KEY_TILE), :], adm)
            return carry

        lax.fori_loop(0, n_far, far_keys, 0)
        row_n = win0 + lax.broadcasted_iota(jnp.int32, (WN, 1), 0)
        keyn_ref[...] = index_keys(ki_ref[near, :],jnp.logical_and(row_n >= first_real, ok_ref[...] != 0))

        def fold(hit):
            return jnp.sum(hit.reshape(hit.shape[0] // SUBLANES, SUBLANES, QT), axis=0)

        def count_ge(c):
            part = lax.fori_loop(0, n_far, lambda t, p: p + fold(jnp.where(key_ref[t] >= c, 1.0, 0.0)),
                                 fold(jnp.where(keyn_ref[...] >= c, 1.0, 0.0)))
            return jnp.sum(part, axis=0, keepdims=True)

        zero = jnp.zeros((1, QT), jnp.int32)
        thr0 = jnp.where(count_ge(zero) >= k_sel, zero, INT_MIN)

        def bit_step(it, thr):
            cand = thr + jnp.left_shift(jnp.int32(1), jnp.int32(30) - it)
            return jnp.where(count_ge(cand) >= k_sel, cand, thr)

        thr = lax.fori_loop(0, 31, bit_step, thr0)
        thr = jnp.maximum(thr, INT_MIN + 1)

        m_ref[...] = jnp.full(m_ref.shape, M_INIT, F32)
        den_ref[...] = jnp.zeros(den_ref.shape, F32)
        acc_ref[...] = jnp.zeros(acc_ref.shape, F32)
        scale = A_HEAD_DIM ** -0.5

        def attend(h, logits, sel, vt_tiles):
            lg = jnp.where(sel, logits, NEG)
            m_old = m_ref[h]
            m_new = jnp.maximum(m_old, jnp.max(lg, axis=0, keepdims=True))
            p = jnp.exp(lg - m_new)
            alpha = jnp.exp(m_old - m_new)
            den_ref[h] = alpha * den_ref[h] + jnp.sum(p, axis=0, keepdims=True)
            pb = p.astype(BF16)
            pv = None
            for u, vt in enumerate(vt_tiles):
                term = _dot(vt, pb[u * LANES:(u + 1) * LANES])
                pv = term if pv is None else pv + term
            acc_ref[h] = alpha * acc_ref[h] + pv
            m_ref[h] = m_new

        def far_tile(c, carry):
            c0 = pl.multiple_of(c * KEY_TILE, KEY_TILE)
            sel = key_ref[c] >= thr
            for h in range(A_HEADS):
                sl = slice(h * A_HEAD_DIM, (h + 1) * A_HEAD_DIM)
                logits = _dot(k_ref[pl.ds(c0, KEY_TILE), sl], q_ref[0, h]) * scale
                attend(h, logits, sel,
                       [vt_ref[c * (KEY_TILE // LANES) + u, sl, :] for u in range(KEY_TILE // LANES)])
            return carry

        lax.fori_loop(0, n_far, far_tile, 0)
        sel_n = keyn_ref[...] >= thr
        wt = win0 // LANES
        for h in range(A_HEADS):
            sl = slice(h * A_HEAD_DIM, (h + 1) * A_HEAD_DIM)
            logits = _dot(k_ref[near, sl], q_ref[0, h]) * scale + bn_ref[h]
            attend(h, logits, sel_n, [vt_ref[wt + u, sl, :] for u in range(WN // LANES)])
            out_t = acc_ref[h] / jnp.maximum(den_ref[h], TINY)
            o_ref[:, sl] = out_t.T[:o_ref.shape[0]].astype(o_ref.dtype)


def _dsa(q_t, qi_t, w_t, k16, v_t, ki16, bias_near, near_ok, out_buf, *, batch, nb, n_keys, out_rows, out_base,
         first_real, qb0, skip_below, k_sel):
    width = k16.shape[1]
    return pl.pallas_call(
        functools.partial(_dsa_kernel, first_real=first_real, qb0=qb0, skip_below=skip_below, k_sel=k_sel),
        out_shape=jax.ShapeDtypeStruct(out_buf.shape, out_buf.dtype),
        grid=(batch, nb),
        in_specs=[
            pl.BlockSpec((1, A_HEADS, A_HEAD_DIM, QT), lambda b, j: (b * nb + j, 0, 0, 0)),
            pl.BlockSpec((1, IDX_HEADS, IDX_DIM, QT), lambda b, j: (b * nb + j, 0, 0, 0)),
            pl.BlockSpec((1, IDX_HEADS, QT), lambda b, j: (b * nb + j, 0, 0)),
            pl.BlockSpec((n_keys, width), lambda b, j: (b, 0)),
            pl.BlockSpec((n_keys // LANES, width, LANES), lambda b, j: (b, 0, 0)),
            pl.BlockSpec((n_keys, IDX_DIM), lambda b, j: (b, 0)),
            pl.BlockSpec((A_HEADS, WN, QT), lambda b, j: (0, 0, 0)),
            pl.BlockSpec((WN, QT), lambda b, j: (0, 0)),
            pl.BlockSpec(memory_space=pl.ANY),
        ],
        out_specs=pl.BlockSpec((out_rows, width), lambda b, j: (out_base + b * nb + j, 0)),
        scratch_shapes=[pltpu.VMEM((n_keys // KEY_TILE, KEY_TILE, QT), jnp.int32),
                        pltpu.VMEM((WN, QT), jnp.int32),
                        pltpu.VMEM((A_HEADS, 1, QT), F32),
                        pltpu.VMEM((A_HEADS, 1, QT), F32),
                        pltpu.VMEM((A_HEADS, A_HEAD_DIM, QT), F32)],
        input_output_aliases={8: 0},
        compiler_params=_cparams(("parallel", "arbitrary")),
        name="dsa",
    )(q_t, qi_t, w_t, k16, v_t, ki16, bias_near, near_ok, out_buf)


def _sample_keys_kernel(ck_ref, cv_ref, cki_ref, nk_ref, nv_ref, nki_ref, k_ref, vt_ref, ki_ref):
    c = pl.program_id(1)
    n_cache = pl.num_programs(1) - 1
    rows = k_ref.shape[0]

    @pl.when(c < n_cache)
    def _():
        for h in range(A_HEADS):
            sl = slice(h * A_HEAD_DIM, (h + 1) * A_HEAD_DIM)
            k_ref[:, sl] = ck_ref[0, :, h, :].astype(BF16)
            vh = cv_ref[0, :, h, :]
            for u in range(rows // LANES):
                vt_ref[u, sl, :] = vh[u * LANES:(u + 1) * LANES, :].T.astype(BF16)
        ki_ref[...] = cki_ref[0].astype(BF16)

    @pl.when(c == n_cache)
    def _():
        k_ref[...] = jnp.zeros(k_ref.shape, BF16)
        vt_ref[...] = jnp.zeros(vt_ref.shape, BF16)
        ki_ref[...] = jnp.zeros(ki_ref.shape, BF16)
        k_ref[0:CHUNK, :] = nk_ref[...].astype(BF16)
        ki_ref[0:CHUNK, :] = nki_ref[:, :IDX_DIM].astype(BF16)
        pad = jnp.zeros((LANES - CHUNK, A_HEAD_DIM), F32)
        for h in range(A_HEADS):
            sl = slice(h * A_HEAD_DIM, (h + 1) * A_HEAD_DIM)
            vt_ref[0, sl, :] = jnp.concatenate([nv_ref[:, sl], pad], axis=0).T.astype(BF16)


def _sample_keys(l, cache_k, cache_v, cache_ki, k_new, v_new, ki_new, *, new_row0, n_keys, rows=512):
    _, bs, past = cache_k.shape[:3]
    width = k_new.shape[1]
    n_cache = past // rows
    nb = -(-n_keys // rows)
    new_blk = new_row0 // CHUNK

    def cached(tail):
        return pl.BlockSpec((None, 1, rows) + tail, lambda b, c: (l, b, jnp.minimum(c, n_cache - 1)) + (0,) * len(tail))

    def new(w):
        return pl.BlockSpec((CHUNK, w), lambda b, c: (new_blk + b, 0))

    k16, v_t, ki16 = pl.pallas_call(
        _sample_keys_kernel,
        out_shape=(jax.ShapeDtypeStruct((bs, n_keys, width), BF16),
                   jax.ShapeDtypeStruct((bs, n_keys // LANES, width, LANES), BF16),
                   jax.ShapeDtypeStruct((bs, n_keys, IDX_DIM), BF16)),
        grid=(bs, nb),
        in_specs=[cached((A_HEADS, A_HEAD_DIM)), cached((A_HEADS, A_HEAD_DIM)), cached((IDX_DIM,)),
                  new(width), new(width), new(LANES)],
        out_specs=(pl.BlockSpec((None, rows, width), lambda b, c: (b, c, 0)),
                   pl.BlockSpec((None, rows // LANES, width, LANES), lambda b, c: (b, c, 0, 0)),
                   pl.BlockSpec((None, rows, IDX_DIM), lambda b, c: (b, c, 0))),
        compiler_params=_cparams(("parallel", "arbitrary")),
        name="sample_keys",
    )(cache_k, cache_v, cache_ki, k_new, v_new, ki_new)
    return (k16.reshape(bs * n_keys, width), v_t.reshape(bs * (n_keys // LANES), width, LANES),
            ki16.reshape(bs * n_keys, IDX_DIM))


def _t5_bucket(rel):
    half = T5_BUCKETS // 2
    exact = half // 2
    n = jnp.abs(rel)
    large = exact + (jnp.log(jnp.maximum(n, 1).astype(F32) / exact)
                     / math.log(T5_MAX_DIST / exact) * (half - exact)).astype(jnp.int32)
    large = jnp.minimum(large, half - 1)
    return jnp.where(rel > 0, half, 0) + jnp.where(n < exact, n, large)


def _bias_tables(t5_bias):
    a = jnp.arange(WN, dtype=jnp.int32)[:, None]
    t = jnp.arange(QT, dtype=jnp.int32)[None, :]
    bucket = _t5_bucket(a - (WN - QT) - t)
    onehot = (bucket[:, :, None] == jnp.arange(T5_BUCKETS, dtype=jnp.int32)).astype(F32)
    near = jnp.einsum("atk,kh->hat", onehot, t5_bias.astype(F32), precision=lax.Precision.HIGHEST)
    far = t5_bias[_t5_bucket(jnp.int32(-2 * CHUNK - 1))].astype(F32)
    ok = (a // CHUNK - (WN - QT) // CHUNK <= t // CHUNK).astype(jnp.int32)
    return near - far[:, None, None], ok


def _rope_tables(pos):
    half = RET_HEAD_DIM // 2
    inv = ROPE_BASE ** (-jnp.arange(half, dtype=F32) / half)
    ang = pos.astype(F32)[:, None] * inv[None, :]
    cos, sin = jnp.cos(ang), jnp.sin(ang)
    return jnp.concatenate([cos, cos], axis=-1), jnp.concatenate([-sin, sin], axis=-1)


def _decay_tables():
    n = CHUNK
    log_g = jnp.log(1.0 - 2.0 ** (-5.0 - jnp.arange(RET_HEADS, dtype=F32)))
    i = jnp.arange(n, dtype=F32)
    diff = i[:, None] - i[None, :]
    dmat = jnp.where(diff >= 0, jnp.exp(jnp.maximum(diff, 0.0)[None] * log_g[:, None, None]), 0.0)
    cross = jnp.exp((i[None, :] + 1.0) * log_g[:, None])
    kdec = jnp.exp((n - 1.0 - i)[None, :] * log_g[:, None])
    gn = jnp.exp(n * log_g)
    wide = (RET_HEADS, n, RET_HEAD_DIM)
    return (dmat, jnp.broadcast_to(cross[:, :, None], wide), jnp.broadcast_to(kdec[:, :, None], wide),
            jnp.broadcast_to(gn[:, None, None], (RET_HEADS, 1, RET_HEAD_DIM)))


def kernel(x_prompt, x_sample, cache_k, cache_v, cache_ki, cache_pool, state_ret, meta_tokens, ln_in_g, ln_in_b, w_in, t5_bias, pool_w, pool_scale, w_branch, w_gate, b_gate, w_out, ln1_g, ln1_b, ln2_g, ln2_b, ffn_w_gate, ffn_w_up, ffn_w_down, moe_w_router, moe_b_router, moe_w_gate, moe_w_up, moe_w_down):
    bp, seq, d = x_prompt.shape
    bs, ts, _ = x_sample.shape
    past = cache_k.shape[2]
    t_real = seq + N_META
    tp = -(-(t_real + 2 * CHUNK) // KEY_TILE) * KEY_TILE
    front = tp - t_real
    assert front % CHUNK == CHUNK - N_META and ts == CHUNK and past % CHUNK == 0
    mp, ms = bp * tp, bs * ts
    m = mp + ms
    assert mp % TM == 0 and ms % TM == 0
    ksel_p = min(TOPK_MAX, seq // 4)
    ksel_s = min(TOPK_MAX, (past + ts) // 4)
    n_keys_s = -(-(past + QT) // KEY_TILE) * KEY_TILE
    assert tp % QT == 0 and front >= WN - QT and past % QT == 0 and past + QT >= WN

    meta = jnp.broadcast_to(meta_tokens.astype(F32)[None], (bp, N_META, d))
    x_rows = jnp.concatenate(
        [jnp.concatenate([jnp.zeros((bp, front, d), F32), meta, x_prompt], axis=1).reshape(mp, d),
         x_sample.reshape(ms, d)], axis=0)
    x32, xb = _layer_norm(x_rows, None, ln_in_g, ln_in_b)

    bias_near, near_ok = _bias_tables(t5_bias)
    cos_p, sin_p = _rope_tables(jnp.arange(tp, dtype=jnp.int32) - (front + N_META))
    cos_s, sin_s = _rope_tables(past + jnp.arange(ts, dtype=jnp.int32))
    decay = _decay_tables()
    zero_state = jnp.zeros((bp, RET_HEADS, RET_HEAD_DIM, RET_HEAD_DIM), F32)
    w_tail = w_in[:, :, TAIL_OFF:]

    outs = {name: [] for name in ("kp", "vp", "kip", "poolp", "retp", "ks", "vs", "kis", "pools", "rets")}
    bw = BRANCH_WIDTH
    for l in range(DEPTH):
        q16, q_t = _proj(xb, w_in, (l,), 0, nat16=True, trans=True, name="proj_q")
        k_new, k16 = _proj(xb, w_in, (l,), 1, nat32=True, nat16=True, name="proj_k")
        v_new, v_t = _proj(xb, w_in, (l,), 2, nat32=True, trans=True, name="proj_v")
        qi16, qi_t = _proj(xb, w_in, (l,), 3, nat16=True, trans=True, name="proj_qi")
        p2, ki16, w_t = _proj_idx(xb, w_in, (l,), 4 * bw // LANES)
        p3 = _matmul(xb, w_tail, (l,), 0, 5, bw, name="proj_tail")
        ki_new = p2[:, :IDX_DIM]
        qi_t = qi_t.reshape(m // LANES, IDX_HEADS, IDX_DIM, LANES)
        v_t = v_t.reshape(m // LANES, bw, LANES)

        def sample_queries_t(a16, heads):
            a = jnp.pad(a16[mp:].reshape(bs, ts, heads, bw // heads), ((0, 0), (0, QT - ts), (0, 0), (0, 0)))
            return jnp.transpose(a, (0, 2, 3, 1))

        w_t_s = jnp.transpose(jnp.pad(p2[mp:, IDX_DIM:IDX_DIM + IDX_HEADS].reshape(bs, ts, IDX_HEADS),
                                      ((0, 0), (0, QT - ts), (0, 0))), (0, 2, 1))
        k16_s, v_t_s, ki16_s = _sample_keys(l, cache_k, cache_v, cache_ki, k_new, v_new, p2, new_row0=mp,
                                            n_keys=n_keys_s)

        branch_buf = jnp.zeros((m, bw), BF16)
        oa = _dsa(q_t, qi_t, w_t, k16, v_t, ki16, bias_near, near_ok, branch_buf, batch=bp, nb=tp // QT, n_keys=tp,
                  out_rows=QT, out_base=0, first_real=front, qb0=0, skip_below=max(1, front // QT), k_sel=ksel_p)
        oa = _dsa(sample_queries_t(q16, A_HEADS), sample_queries_t(qi16, IDX_HEADS), w_t_s, k16_s, v_t_s, ki16_s,
                  bias_near, near_ok, oa, batch=bs, nb=1, n_keys=n_keys_s, out_rows=ts, out_base=mp // ts,
                  first_real=0, qb0=past // CHUNK, skip_below=0, k_sel=ksel_s)

        tb_p = 256
        ob = _pool(p3, 0, p3, 0, pool_w[l], pool_scale[l], branch_buf, batch=bp, rows=tp, tb=tb_p, row0=0,
                   halo_map=lambda b, i: jnp.maximum((b * tp + i * tb_p) // HALO - 1, 0), first_valid=front)
        pool_hist = jnp.pad(cache_pool[l], ((0, 0), (HALO - POOL_PAST, 0), (0, 0))).reshape(bs * HALO, bw)
        ob = _pool(p3, 0, pool_hist, 0, pool_w[l], pool_scale[l], ob, batch=bs, rows=ts, tb=ts, row0=mp,
                   halo_map=lambda b, i: b, first_valid=-POOL_PAST)

        oc, ret_p = _retention(p3, zero_state, cos_p, sin_p, decay, branch_buf, batch=bp, rows=tp, row0=0,
                               first_valid=front)
        oc, ret_s = _retention(p3, state_ret[l].astype(F32), cos_s, sin_s, decay, oc, batch=bs, rows=ts, row0=mp,
                               first_valid=0)

        merged = _merge(l, xb, oa, ob, oc, w_branch, w_gate, b_gate)
        y = _matmul(merged, w_out, (l,), 0, d // 512, 512, name="w_out")
        x32, xb = _layer_norm(x32, y, ln1_g[l], ln1_b[l])

        if l % 2 == 0:
            h = _gate_up(xb, ffn_w_gate, ffn_w_up, (l // 2,))
            f = _matmul(h, ffn_w_down, (l // 2,), 0, d // 512, 512, name="ffn_down")
        else:
            gates, experts = _router(x32, moe_w_router[l // 2], moe_b_router[l // 2])
            src, gate_sorted, pos, tile_expert, n_used = _dispatch_plan(experts, gates, MOE_TILE)
            x_sorted = _gather_rows(x32, src, MOE_TILE)
            y_sorted = _moe_experts(x_sorted, gate_sorted, tile_expert, n_used, moe_w_gate, moe_w_up, moe_w_down,
                                    l // 2, MOE_TILE)
            f = _combine_rows(y_sorted, pos, m, MOE_TILE)
        x32, xb = _layer_norm(x32, f, ln2_g[l], ln2_b[l])

        def prompt_rows(a, width):
            return a[:mp].reshape(bp, tp, width)[:, front:]

        u = p3[:, :bw]
        outs["kp"].append(prompt_rows(k_new, bw).reshape(bp, t_real, A_HEADS, A_HEAD_DIM))
        outs["vp"].append(prompt_rows(v_new, bw).reshape(bp, t_real, A_HEADS, A_HEAD_DIM))
        outs["kip"].append(prompt_rows(ki_new, IDX_DIM))
        outs["poolp"].append(prompt_rows(u, bw)[:, -POOL_PAST:])
        outs["retp"].append(ret_p)
        outs["ks"].append(k_new[mp:].reshape(bs, ts, A_HEADS, A_HEAD_DIM))
        outs["vs"].append(v_new[mp:].reshape(bs, ts, A_HEADS, A_HEAD_DIM))
        outs["kis"].append(ki_new[mp:].reshape(bs, ts, IDX_DIM))
        outs["pools"].append(u[mp:].reshape(bs, ts, bw)[:, -POOL_PAST:])
        outs["rets"].append(ret_s)

    y_prompt = x32[:mp].reshape(bp, tp, d)[:, front + N_META:]
    y_sample = x32[mp:].reshape(bs, ts, d)
    return (y_prompt, y_sample) + tuple(
        jnp.stack(outs[name]) for name in ("kp", "vp", "kip", "poolp", "retp", "ks", "vs", "kis", "pools", "rets"))
```

```python
import functools
import math

import jax
import jax.numpy as jnp
from jax import lax
from jax.experimental import pallas as pl
from jax.experimental.pallas import tpu as pltpu

F32 = jnp.float32
BF16 = jnp.bfloat16

D_MODEL = 2048
DEPTH = 2
CHUNK = 64
N_META = 16
BRANCH_WIDTH = D_MODEL // 2
A_HEADS = 8
A_HEAD_DIM = BRANCH_WIDTH // A_HEADS
IDX_HEADS = 16
IDX_DIM = 64
TOPK_MAX = 256
T5_BUCKETS = 32
T5_MAX_DIST = 128
POOL_WINDOWS = (2, 4, 8, 16)
POOL_GROUP = BRANCH_WIDTH // 4
POOL_PAST = 15
RET_HEADS = 8
RET_HEAD_DIM = BRANCH_WIDTH // RET_HEADS
ROPE_BASE = 10000.0
N_BRANCH = 3
D_FF = 11 * D_MODEL // 4
N_EXPERTS = 8
TOP_K = 2
ALPHA = (2 * DEPTH) ** 0.25
LN_EPS = 1e-5
IN_SPLITS = (BRANCH_WIDTH, BRANCH_WIDTH, BRANCH_WIDTH, IDX_HEADS * IDX_DIM, IDX_DIM, IDX_HEADS,
             BRANCH_WIDTH, BRANCH_WIDTH, BRANCH_WIDTH, BRANCH_WIDTH, BRANCH_WIDTH)
IN_WIDTH = sum(IN_SPLITS)
TAIL_OFF = 4 * BRANCH_WIDTH + IDX_DIM + IDX_HEADS

LANES = 128
SUBLANES = 8
HALO = 16
QT = 2 * CHUNK
WN = QT + 2 * CHUNK
KEY_TILE = 256
VMEM_LIMIT = 56 * 1024 * 1024
TM = 512
MOE_TILE = 512
INT_MIN = -2 ** 31
NEG = -1e30
M_INIT = -1e20
TINY = 1e-30


def _cparams(sem):
    return pltpu.CompilerParams(dimension_semantics=sem, vmem_limit_bytes=VMEM_LIMIT)


def _dot(a, b):
    return jnp.dot(a, b, preferred_element_type=F32)


def _dot_nt(a, b):
    return lax.dot_general(a, b, (((1,), (1,)), ((), ())), preferred_element_type=F32)


def _dot_tn(a, b):
    return lax.dot_general(a, b, (((0,), (0,)), ((), ())), preferred_element_type=F32)


def _ln_kernel(x_ref, g_ref, b_ref, o32_ref, o16_ref):
    x = x_ref[...]
    mu = jnp.mean(x, axis=-1, keepdims=True)
    xc = x - mu
    var = jnp.mean(xc * xc, axis=-1, keepdims=True)
    y = xc * lax.rsqrt(var + LN_EPS) * g_ref[...] + b_ref[...]
    o32_ref[...] = y
    o16_ref[...] = y.astype(BF16)


def _ln_res_kernel(x_ref, y_ref, g_ref, b_ref, o32_ref, o16_ref):
    x = ALPHA * x_ref[...] + y_ref[...]
    mu = jnp.mean(x, axis=-1, keepdims=True)
    xc = x - mu
    var = jnp.mean(xc * xc, axis=-1, keepdims=True)
    y = xc * lax.rsqrt(var + LN_EPS) * g_ref[...] + b_ref[...]
    o32_ref[...] = y
    o16_ref[...] = y.astype(BF16)


def _layer_norm(x, y, g, b, tm=256):
    m, d = x.shape
    row = pl.BlockSpec((tm, d), lambda i: (i, 0))
    vec = pl.BlockSpec((1, d), lambda i: (0, 0))
    args = (x,) if y is None else (x, y)
    return pl.pallas_call(
        _ln_kernel if y is None else _ln_res_kernel,
        out_shape=(jax.ShapeDtypeStruct((m, d), F32), jax.ShapeDtypeStruct((m, d), BF16)),
        grid=(m // tm,),
        in_specs=[row] * len(args) + [vec, vec],
        out_specs=(row, row),
        compiler_params=_cparams(("parallel",)),
        name="layer_norm",
    )(*args, g.reshape(1, d), b.reshape(1, d))


def _mm_kernel(x_ref, w_ref, o_ref, wb_ref, *, w_rows_are_outputs):
    @pl.when(pl.program_id(1) == 0)
    def _():
        w = w_ref[...]
        wb_ref[...] = (w.T if w_rows_are_outputs else w).astype(BF16)

    o_ref[...] = _dot(x_ref[...], wb_ref[...]).astype(o_ref.dtype)


def _matmul(x, w, lead, col0, n_tiles, tn, name="matmul", w_transposed=False):
    m, k = x.shape
    nl = len(lead)
    if w_transposed:
        w_spec = pl.BlockSpec((None,) * nl + (tn, k), lambda j, i: tuple(lead) + (j + col0, 0))
    else:
        w_spec = pl.BlockSpec((None,) * nl + (k, tn), lambda j, i: tuple(lead) + (0, j + col0))
    x_spec = pl.BlockSpec((TM, k), lambda j, i: (i, 0))
    o_spec = pl.BlockSpec((TM, tn), lambda j, i: (i, j))
    return pl.pallas_call(
        functools.partial(_mm_kernel, w_rows_are_outputs=w_transposed),
        out_shape=jax.ShapeDtypeStruct((m, n_tiles * tn), F32),
        grid=(n_tiles, m // TM),
        in_specs=[x_spec, w_spec],
        out_specs=o_spec,
        scratch_shapes=[pltpu.VMEM((k, tn), BF16)],
        compiler_params=_cparams(("arbitrary", "arbitrary")),
        name=name,
    )(x, w)


def _store_lane_tiles_t(res, out_ref):
    rows, width = res.shape
    for r in range(rows // LANES):
        for c in range(width // LANES):
            tile = res[r * LANES:(r + 1) * LANES, c * LANES:(c + 1) * LANES]
            out_ref[r, c] = tile.T.astype(out_ref.dtype)


def _proj_kernel(x_ref, w_ref, *refs, nat32, nat16, trans):
    outs, wb_ref = list(refs[:-1]), refs[-1]

    @pl.when(pl.program_id(0) == 0)
    def _():
        wb_ref[...] = w_ref[...].T.astype(BF16)

    res = _dot(x_ref[...], wb_ref[...])
    if nat32:
        outs.pop(0)[...] = res
    if nat16:
        outs.pop(0)[...] = res.astype(BF16)
    if trans:
        _store_lane_tiles_t(res, outs.pop(0))


def _proj(x, w, lead, col0, *, nat32=False, nat16=False, trans=False, tm=TM, name="proj"):
    m, k = x.shape
    bw = BRANCH_WIDTH
    nl = len(lead)
    shapes, specs = [], []
    row_spec = pl.BlockSpec((tm, bw), lambda i: (i, 0))
    if nat32:
        shapes.append(jax.ShapeDtypeStruct((m, bw), F32)); specs.append(row_spec)
    if nat16:
        shapes.append(jax.ShapeDtypeStruct((m, bw), BF16)); specs.append(row_spec)
    if trans:
        shapes.append(jax.ShapeDtypeStruct((m // LANES, bw // LANES, LANES, LANES), BF16))
        specs.append(pl.BlockSpec((tm // LANES, bw // LANES, LANES, LANES), lambda i: (i, 0, 0, 0)))
    return pl.pallas_call(
        functools.partial(_proj_kernel, nat32=nat32, nat16=nat16, trans=trans),
        out_shape=tuple(shapes),
        grid=(m // tm,),
        in_specs=[pl.BlockSpec((tm, k), lambda i: (i, 0)),
                  pl.BlockSpec((None,) * nl + (bw, k), lambda i: tuple(lead) + (col0, 0))],
        out_specs=tuple(specs),
        scratch_shapes=[pltpu.VMEM((k, bw), BF16)],
        compiler_params=_cparams(("arbitrary",)),
        name=name,
    )(x, w)


def _proj_idx_kernel(x_ref, w_ref, nat_ref, ki_ref, wt_ref, wb_ref):
    @pl.when(pl.program_id(0) == 0)
    def _():
        wb_ref[...] = w_ref[...].T.astype(BF16)

    res = _dot(x_ref[...], wb_ref[...])
    nat_ref[...] = res
    ki_ref[...] = res[:, :IDX_DIM].astype(BF16)
    for r in range(res.shape[0] // LANES):
        wt_ref[r] = res[r * LANES:(r + 1) * LANES, :].T[IDX_DIM:IDX_DIM + IDX_HEADS, :]


def _proj_idx(x, w, lead, col_block, tm=TM):
    m, k = x.shape
    nl = len(lead)
    return pl.pallas_call(
        _proj_idx_kernel,
        out_shape=(jax.ShapeDtypeStruct((m, LANES), F32), jax.ShapeDtypeStruct((m, IDX_DIM), BF16),
                   jax.ShapeDtypeStruct((m // LANES, IDX_HEADS, LANES), F32)),
        grid=(m // tm,),
        in_specs=[pl.BlockSpec((tm, k), lambda i: (i, 0)),
                  pl.BlockSpec((None,) * nl + (LANES, k), lambda i: tuple(lead) + (col_block, 0))],
        out_specs=(pl.BlockSpec((tm, LANES), lambda i: (i, 0)), pl.BlockSpec((tm, IDX_DIM), lambda i: (i, 0)),
                   pl.BlockSpec((tm // LANES, IDX_HEADS, LANES), lambda i: (i, 0, 0))),
        scratch_shapes=[pltpu.VMEM((k, LANES), BF16)],
        compiler_params=_cparams(("arbitrary",)),
        name="proj_idx",
    )(x, w)


def _gate_up_kernel(x_ref, wg_ref, wu_ref, o_ref, wgb_ref, wub_ref):
    @pl.when(pl.program_id(1) == 0)
    def _():
        wgb_ref[...] = wg_ref[...].astype(BF16)
        wub_ref[...] = wu_ref[...].astype(BF16)

    x = x_ref[...]
    g = _dot(x, wgb_ref[...])
    u = _dot(x, wub_ref[...])
    o_ref[...] = (g * jax.nn.sigmoid(g) * u).astype(o_ref.dtype)


def _gate_up(x, wg, wu, lead, tn=512):
    m, k = x.shape
    n = wg.shape[-1]
    nl = len(lead)
    w_spec = pl.BlockSpec((None,) * nl + (k, tn), lambda j, i: tuple(lead) + (0, j))
    x_spec = pl.BlockSpec((TM, k), lambda j, i: (i, 0))
    return pl.pallas_call(
        _gate_up_kernel,
        out_shape=jax.ShapeDtypeStruct((m, n), BF16),
        grid=(n // tn, m // TM),
        in_specs=[x_spec, w_spec, w_spec],
        out_specs=pl.BlockSpec((TM, tn), lambda j, i: (i, j)),
        scratch_shapes=[pltpu.VMEM((k, tn), BF16), pltpu.VMEM((k, tn), BF16)],
        compiler_params=_cparams(("arbitrary", "arbitrary")),
        name="gate_up",
    )(x, wg, wu)


def _merge_kernel(x_ref, oa_ref, ob_ref, oc_ref, wg0_ref, wg1_ref, wg2_ref, wb0_ref, wb1_ref, wb2_ref,
                  bg0_ref, bg1_ref, bg2_ref, o_ref, wgb_ref, wbb_ref):
    wg_refs = (wg0_ref, wg1_ref, wg2_ref)
    wb_refs = (wb0_ref, wb1_ref, wb2_ref)

    @pl.when(pl.program_id(1) == 0)
    def _():
        for n in range(N_BRANCH):
            wgb_ref[n] = wg_refs[n][...].astype(BF16)
            wbb_ref[n] = wb_refs[n][...].astype(BF16)

    x = x_ref[...]
    acc = None
    for n, (o_in, bg) in enumerate(zip((oa_ref, ob_ref, oc_ref), (bg0_ref, bg1_ref, bg2_ref))):
        gate = jax.nn.sigmoid(_dot(x, wgb_ref[n]) + bg[...])
        term = gate * _dot(o_in[...], wbb_ref[n])
        acc = term if acc is None else acc + term
    o_ref[...] = acc.astype(o_ref.dtype)


def _merge(l, x, oa, ob, oc, w_branch, w_gate, b_gate, tn=256):
    m, d = x.shape
    w = oa.shape[1]
    nt = d // tn
    x_spec = pl.BlockSpec((TM, d), lambda j, i: (i, 0))
    o_in_spec = pl.BlockSpec((TM, w), lambda j, i: (i, 0))
    wg_specs = [pl.BlockSpec((None, d, tn), lambda j, i, n=n: (l, 0, n * nt + j)) for n in range(N_BRANCH)]
    wb_specs = [pl.BlockSpec((None, None, w, tn), lambda j, i, n=n: (l, n, 0, j)) for n in range(N_BRANCH)]
    bg_specs = [pl.BlockSpec((None, 1, tn), lambda j, i, n=n: (l, 0, n * nt + j)) for n in range(N_BRANCH)]
    return pl.pallas_call(
        _merge_kernel,
        out_shape=jax.ShapeDtypeStruct((m, d), BF16),
        grid=(nt, m // TM),
        in_specs=[x_spec, o_in_spec, o_in_spec, o_in_spec] + wg_specs + wb_specs + bg_specs,
        out_specs=pl.BlockSpec((TM, tn), lambda j, i: (i, j)),
        scratch_shapes=[pltpu.VMEM((N_BRANCH, d, tn), BF16), pltpu.VMEM((N_BRANCH, w, tn), BF16)],
        compiler_params=_cparams(("arbitrary", "arbitrary")),
        name="merge",
    )(x, oa, ob, oc, w_gate, w_gate, w_gate, w_branch, w_branch, w_branch,
      b_gate.reshape(DEPTH, 1, N_BRANCH * d), b_gate.reshape(DEPTH, 1, N_BRANCH * d),
      b_gate.reshape(DEPTH, 1, N_BRANCH * d))


def _split_bf16(a):
    hi = a.astype(BF16)
    lo = (a - hi.astype(F32)).astype(BF16)
    return hi, lo


def _router_kernel(x_ref, w_ref, b_ref, gate_ref, expert_ref):
    xh, xl = _split_bf16(x_ref[...])
    wh, wl = _split_bf16(w_ref[...])
    logits = _dot(xh, wh) + (_dot(xh, wl) + _dot(xl, wh)) + b_ref[...]
    lane = lax.broadcasted_iota(jnp.int32, logits.shape, 1)
    logits = jnp.where(lane < N_EXPERTS, logits, -jnp.inf)
    m1 = jnp.max(logits, axis=-1, keepdims=True)
    i1 = jnp.min(jnp.where(logits == m1, lane, LANES), axis=-1, keepdims=True)
    rest = jnp.where(lane == i1, -jnp.inf, logits)
    m2 = jnp.max(rest, axis=-1, keepdims=True)
    i2 = jnp.min(jnp.where(rest == m2, lane, LANES), axis=-1, keepdims=True)
    e = jnp.exp(m2 - m1)
    p1 = 1.0 / (1.0 + e)
    p2 = e / (1.0 + e)
    gate_ref[...] = jnp.where(lane == 0, p1, jnp.where(lane == 1, p2, 0.0))
    expert_ref[...] = jnp.where(lane == 0, i1, jnp.where(lane == 1, i2, 0))


def _router(x32, w_r, b_r, tm=256):
    m, d = x32.shape
    w_pad = jnp.pad(w_r, ((0, 0), (0, LANES - N_EXPERTS)))
    b_pad = jnp.pad(b_r, (0, LANES - N_EXPERTS)).reshape(1, LANES)
    out_spec = pl.BlockSpec((tm, LANES), lambda i: (i, 0))
    gates, experts = pl.pallas_call(
        _router_kernel,
        out_shape=(jax.ShapeDtypeStruct((m, LANES), F32), jax.ShapeDtypeStruct((m, LANES), jnp.int32)),
        grid=(m // tm,),
        in_specs=[pl.BlockSpec((tm, d), lambda i: (i, 0)), pl.BlockSpec((d, LANES), lambda i: (0, 0)),
                  pl.BlockSpec((1, LANES), lambda i: (0, 0))],
        out_specs=(out_spec, out_spec),
        compiler_params=_cparams(("parallel",)),
        name="router",
    )(x32, w_pad, b_pad)
    return gates[:, :TOP_K], experts[:, :TOP_K]


def _dispatch_plan(experts, gates, tile, token_rows):
    m = experts.shape[0]
    n_assign = TOP_K * m
    n_tiles = n_assign // tile + N_EXPERTS
    e = experts.reshape(n_assign)
    live = jnp.repeat(token_rows, TOP_K)
    onehot = jnp.logical_and(e[:, None] == jnp.arange(N_EXPERTS, dtype=jnp.int32)[None, :],
                             live[:, None]).astype(jnp.int32)
    rank = jnp.sum((jnp.cumsum(onehot, axis=0) - onehot) * onehot, axis=1)
    tiles_per = (jnp.sum(onehot, axis=0) + tile - 1) // tile
    tile_end = jnp.cumsum(tiles_per)
    pos = ((tile_end - tiles_per)[e] * tile + rank).astype(jnp.int32)
    slot = jnp.where(live, pos, n_tiles * tile)
    src = jnp.zeros((n_tiles * tile,), jnp.int32).at[slot].set(jnp.arange(n_assign, dtype=jnp.int32) // TOP_K,
                                                               mode="drop")
    gate = jnp.zeros((n_tiles * tile,), F32).at[slot].set(gates.reshape(n_assign), mode="drop")
    pos = jnp.where(live, pos, 0)
    n_used = tile_end[-1:]
    t_idx = jnp.minimum(jnp.arange(n_tiles, dtype=jnp.int32), n_used - 1)
    tile_expert = jnp.sum((t_idx[:, None] >= tile_end[None, :]).astype(jnp.int32), axis=1)
    return src, gate.reshape(-1, 1), pos, tile_expert.astype(jnp.int32), n_used.astype(jnp.int32)


def _issue_rows(idx_ref, first, count, stride, src_hbm, dst, sem):
    def body(i, carry):
        pltpu.make_async_copy(src_hbm.at[pl.ds(idx_ref[first + i * stride], 1), :], dst.at[pl.ds(i, 1), :],
                              sem).start()
        return carry

    lax.fori_loop(0, count, body, 0, unroll=8)


def _wait_rows(count, src_hbm, dst, sem):
    def body(i, carry):
        pltpu.make_async_copy(src_hbm.at[pl.ds(0, 1), :], dst.at[pl.ds(i, 1), :], sem).wait()
        return carry

    lax.fori_loop(0, count, body, 0, unroll=8)


def _gather_kernel(src_ref, x_hbm, o_ref, buf, sem, *, tile):
    t = pl.program_id(0)

    @pl.when(t == 0)
    def _():
        _issue_rows(src_ref, 0, tile, 1, x_hbm, buf.at[0], sem.at[0])

    @pl.when(t + 1 < pl.num_programs(0))
    def _():
        nxt = (t + 1) % 2
        _issue_rows(src_ref, (t + 1) * tile, tile, 1, x_hbm, buf.at[nxt], sem.at[nxt])

    cur = t % 2
    _wait_rows(tile, x_hbm, buf.at[cur], sem.at[cur])
    o_ref[...] = buf[cur].astype(o_ref.dtype)


def _gather_rows(x32, src, tile):
    d = x32.shape[1]
    n_rows = src.shape[0]
    return pl.pallas_call(
        functools.partial(_gather_kernel, tile=tile),
        out_shape=jax.ShapeDtypeStruct((n_rows, d), BF16),
        grid_spec=pltpu.PrefetchScalarGridSpec(
            num_scalar_prefetch=1, grid=(n_rows // tile,),
            in_specs=[pl.BlockSpec(memory_space=pl.ANY)],
            out_specs=pl.BlockSpec((tile, d), lambda t, src: (t, 0)),
            scratch_shapes=[pltpu.VMEM((2, tile, d), F32), pltpu.SemaphoreType.DMA((2,))]),
        compiler_params=_cparams(("arbitrary",)),
        name="moe_gather",
    )(src, x32)


def _combine_kernel(pos_ref, y_hbm, o_ref, buf, sem, *, tile):
    t = pl.program_id(0)

    def issue(tile_idx, slot):
        for s in range(TOP_K):
            _issue_rows(pos_ref, tile_idx * tile * TOP_K + s, tile, TOP_K, y_hbm, buf.at[slot, s], sem.at[slot])

    @pl.when(t == 0)
    def _():
        issue(0, 0)

    @pl.when(t + 1 < pl.num_programs(0))
    def _():
        issue(t + 1, (t + 1) % 2)

    cur = t % 2
    for s in range(TOP_K):
        _wait_rows(tile, y_hbm, buf.at[cur, s], sem.at[cur])
    total = buf[cur, 0]
    for s in range(1, TOP_K):
        total = total + buf[cur, s]
    o_ref[...] = total


def _combine_rows(y_sorted, pos, m, tile):
    d = y_sorted.shape[1]
    return pl.pallas_call(
        functools.partial(_combine_kernel, tile=tile),
        out_shape=jax.ShapeDtypeStruct((m, d), F32),
        grid_spec=pltpu.PrefetchScalarGridSpec(
            num_scalar_prefetch=1, grid=(m // tile,),
            in_specs=[pl.BlockSpec(memory_space=pl.ANY)],
            out_specs=pl.BlockSpec((tile, d), lambda t, pos: (t, 0)),
            scratch_shapes=[pltpu.VMEM((2, TOP_K, tile, d), F32), pltpu.SemaphoreType.DMA((2,))]),
        compiler_params=_cparams(("arbitrary",)),
        name="moe_combine",
    )(pos, y_sorted)


def _expert_changed(te_ref, t):
    return jnp.logical_or(t == 0, te_ref[t] != te_ref[jnp.maximum(t - 1, 0)])


def _moe_gate_up_kernel(te_ref, nu_ref, x_ref, wg_ref, wu_ref, o_ref, wgb_ref, wub_ref):
    t = pl.program_id(1)

    @pl.when(_expert_changed(te_ref, t))
    def _():
        wgb_ref[...] = wg_ref[...].astype(BF16)
        wub_ref[...] = wu_ref[...].astype(BF16)

    @pl.when(t < nu_ref[0])
    def _():
        x = x_ref[...]
        g = _dot(x, wgb_ref[...])
        u = _dot(x, wub_ref[...])
        o_ref[...] = (g * jax.nn.sigmoid(g) * u).astype(o_ref.dtype)

    @pl.when(t >= nu_ref[0])
    def _():
        o_ref[...] = jnp.zeros(o_ref.shape, o_ref.dtype)


def _moe_down_kernel(te_ref, nu_ref, h_ref, w_ref, g_ref, o_ref, wb_ref):
    t = pl.program_id(1)

    @pl.when(_expert_changed(te_ref, t))
    def _():
        wb_ref[...] = w_ref[...].astype(BF16)

    @pl.when(t < nu_ref[0])
    def _():
        o_ref[...] = _dot(h_ref[...], wb_ref[...]) * g_ref[...]

    @pl.when(t >= nu_ref[0])
    def _():
        o_ref[...] = jnp.zeros(o_ref.shape, o_ref.dtype)


def _moe_experts(x_sorted, gate_sorted, tile_expert, n_used, w_gate, w_up, w_down, layer, tile, tn=512):
    r, d = x_sorted.shape
    ff = w_gate.shape[-1]
    n_tiles = r // tile

    def x_rows(width):
        return pl.BlockSpec((tile, width), lambda j, t, te, nu: (jnp.minimum(t, nu[0] - 1), 0))

    def w_cols(k):
        return pl.BlockSpec((None, None, k, tn), lambda j, t, te, nu: (layer, te[t], 0, j))

    h = pl.pallas_call(
        _moe_gate_up_kernel,
        out_shape=jax.ShapeDtypeStruct((r, ff), BF16),
        grid_spec=pltpu.PrefetchScalarGridSpec(
            num_scalar_prefetch=2, grid=(ff // tn, n_tiles),
            in_specs=[x_rows(d), w_cols(d), w_cols(d)],
            out_specs=pl.BlockSpec((tile, tn), lambda j, t, te, nu: (t, j)),
            scratch_shapes=[pltpu.VMEM((d, tn), BF16), pltpu.VMEM((d, tn), BF16)]),
        compiler_params=_cparams(("arbitrary", "arbitrary")),
        name="moe_gate_up",
    )(tile_expert, n_used, x_sorted, w_gate, w_up)
    return pl.pallas_call(
        _moe_down_kernel,
        out_shape=jax.ShapeDtypeStruct((r, d), F32),
        grid_spec=pltpu.PrefetchScalarGridSpec(
            num_scalar_prefetch=2, grid=(d // tn, n_tiles),
            in_specs=[x_rows(ff), w_cols(ff),
                      pl.BlockSpec((tile, 1), lambda j, t, te, nu: (jnp.minimum(t, nu[0] - 1), 0))],
            out_specs=pl.BlockSpec((tile, tn), lambda j, t, te, nu: (t, j)),
            scratch_shapes=[pltpu.VMEM((ff, tn), BF16)]),
        compiler_params=_cparams(("arbitrary", "arbitrary")),
        name="moe_down",
    )(tile_expert, n_used, h, w_down, gate_sorted)


def _pool_kernel(u_ref, halo_ref, w_ref, sc_ref, _buf_ref, o_ref, ext_ref, *, tb, first_valid):
    row0 = pl.program_id(1) * tb
    r_cur = row0 + lax.broadcasted_iota(jnp.int32, (tb, 1), 0)
    r_halo = row0 - HALO + lax.broadcasted_iota(jnp.int32, (HALO, 1), 0)
    u = jnp.where(r_cur >= first_valid, u_ref[...], 0.0)
    ext_ref[0:HALO, :] = jnp.where(r_halo >= first_valid, halo_ref[...], 0.0)
    ext_ref[HALO:, :] = u
    seen = (r_cur - first_valid + 1).astype(F32)
    for g, win in enumerate(POOL_WINDOWS):
        c0, c1 = g * POOL_GROUP, (g + 1) * POOL_GROUP
        s = u[:, c0:c1]
        for back in range(1, win):
            s = s + ext_ref[HALO - back:HALO - back + tb, c0:c1]
        cnt = jnp.clip(seen, 1.0, float(win))
        diff = s / cnt - u[:, c0:c1]
        y = _dot(diff.astype(BF16), w_ref[g].astype(BF16))
        o_ref[:, c0:c1] = (y * sc_ref[:, c0:c1]).astype(o_ref.dtype)


def _pool(u2d, col_block, halo2d, halo_col_block, pool_w_l, pool_scale_l, out_buf, *, batch, rows, tb, row0,
          halo_map, first_valid):
    nb = rows // tb
    base = row0 // tb
    return pl.pallas_call(
        functools.partial(_pool_kernel, tb=tb, first_valid=first_valid),
        out_shape=jax.ShapeDtypeStruct(out_buf.shape, out_buf.dtype),
        grid=(batch, nb),
        in_specs=[
            pl.BlockSpec((tb, BRANCH_WIDTH), lambda b, i: (base + b * nb + i, col_block)),
            pl.BlockSpec((HALO, BRANCH_WIDTH), lambda b, i: (halo_map(b, i), halo_col_block)),
            pl.BlockSpec((len(POOL_WINDOWS), POOL_GROUP, POOL_GROUP), lambda b, i: (0, 0, 0)),
            pl.BlockSpec((1, BRANCH_WIDTH), lambda b, i: (0, 0)),
            pl.BlockSpec(memory_space=pl.ANY),
        ],
        out_specs=pl.BlockSpec((tb, BRANCH_WIDTH), lambda b, i: (base + b * nb + i, 0)),
        scratch_shapes=[pltpu.VMEM((HALO + tb, BRANCH_WIDTH), F32)],
        input_output_aliases={4: 0},
        compiler_params=_cparams(("parallel", "arbitrary")),
        name="pool",
    )(u2d, halo2d, pool_w_l, pool_scale_l.reshape(1, BRANCH_WIDTH), out_buf)


def _ret_kernel(q_ref, k_ref, v_ref, g_ref, cos_ref, sin_ref, dmat_ref, cross_ref, kdec_ref, gn_ref, s0_ref,
                _buf_ref, o_ref, s_out_ref, s_scr, *, first_valid):
    i = pl.program_id(1)

    @pl.when(i == 0)
    def _():
        s_scr[...] = s0_ref[...]

    rows = i * CHUNK + lax.broadcasted_iota(jnp.int32, (CHUNK, 1), 0)
    valid = rows >= first_valid
    cos = cos_ref[...]
    sin = sin_ref[...]
    rscale = RET_HEAD_DIM ** -0.5
    half = RET_HEAD_DIM // 2
    for h in range(RET_HEADS):
        sl = slice(h * RET_HEAD_DIM, (h + 1) * RET_HEAD_DIM)
        q = q_ref[:, sl]
        k = k_ref[:, sl]
        v = jnp.where(valid, v_ref[:, sl], 0.0).astype(BF16)
        qr = (q * cos + pltpu.roll(q, half, 1) * sin).astype(BF16)
        kr = jnp.where(valid, (k * cos + pltpu.roll(k, half, 1) * sin) * rscale, 0.0)
        state = s_scr[h]
        inner = _dot_nt(qr, kr.astype(BF16)) * dmat_ref[h]
        o = _dot(inner.astype(BF16), v) + _dot(qr, state.astype(BF16)) * cross_ref[h]
        s_scr[h] = gn_ref[h] * state + _dot_tn((kr * kdec_ref[h]).astype(BF16), v)
        mu = jnp.mean(o, axis=-1, keepdims=True)
        oc = o - mu
        var = jnp.mean(oc * oc, axis=-1, keepdims=True)
        gate = g_ref[:, sl]
        o_ref[:, sl] = (gate * jax.nn.sigmoid(gate) * (oc * lax.rsqrt(var + LN_EPS))).astype(o_ref.dtype)

    @pl.when(i == pl.num_programs(1) - 1)
    def _():
        s_out_ref[...] = s_scr[...]


def _retention(p3, s0, cos, sin, tabs, out_buf, *, batch, rows, row0, first_valid):
    nb = rows // CHUNK
    base = row0 // CHUNK
    dmat, cross, kdec, gn = tabs

    def col(c):
        return pl.BlockSpec((CHUNK, BRANCH_WIDTH), lambda b, i: (base + b * nb + i, c))

    tab_rows = pl.BlockSpec((CHUNK, RET_HEAD_DIM), lambda b, i: (i, 0))

    def full(a):
        return pl.BlockSpec(a.shape, lambda b, i: (0,) * a.ndim)

    state_spec = pl.BlockSpec((None, RET_HEADS, RET_HEAD_DIM, RET_HEAD_DIM), lambda b, i: (b, 0, 0, 0))
    return pl.pallas_call(
        functools.partial(_ret_kernel, first_valid=first_valid),
        out_shape=(jax.ShapeDtypeStruct(out_buf.shape, out_buf.dtype),
                   jax.ShapeDtypeStruct((batch, RET_HEADS, RET_HEAD_DIM, RET_HEAD_DIM), F32)),
        grid=(batch, nb),
        in_specs=[col(1), col(2), col(3), col(4), tab_rows, tab_rows, full(dmat), full(cross), full(kdec),
                  full(gn), state_spec, pl.BlockSpec(memory_space=pl.ANY)],
        out_specs=(pl.BlockSpec((CHUNK, BRANCH_WIDTH), lambda b, i: (base + b * nb + i, 0)), state_spec),
        scratch_shapes=[pltpu.VMEM((RET_HEADS, RET_HEAD_DIM, RET_HEAD_DIM), F32)],
        input_output_aliases={11: 0},
        compiler_params=_cparams(("parallel", "arbitrary")),
        name="retention",
    )(p3, p3, p3, p3, cos, sin, dmat, cross, kdec, gn, s0, out_buf)


def _sortable(x):
    bits = pltpu.bitcast(x + 0.0, jnp.int32)
    return bits ^ ((bits >> 31) & 0x7FFFFFFF)


def _dsa_kernel(q_ref, qi_ref, w_ref, k_ref, vt_ref, ki_ref, bn_ref, ok_ref, _buf_ref, o_ref,
                key_ref, keyn_ref, m_ref, den_ref, acc_ref, *, first_real, qb0, skip_below, k_sel):
    j = pl.program_id(1)

    @pl.when(j < skip_below)
    def _():
        o_ref[...] = jnp.zeros(o_ref.shape, o_ref.dtype)

    @pl.when(j >= skip_below)
    def _():
        win0 = pl.multiple_of(j * QT + (qb0 * CHUNK - (WN - QT)), LANES)
        n_far = (win0 + KEY_TILE - 1) // KEY_TILE
        near = pl.ds(win0, WN)

        def index_keys(ki, adm):
            acc = None
            for h in range(0, IDX_HEADS, 2):
                pair = _dot(ki, jnp.concatenate([qi_ref[0, h], qi_ref[0, h + 1]], axis=1))
                term = (w_ref[0, h:h + 1, :] * jnp.maximum(pair[:, :QT], 0.0)
                        + w_ref[0, h + 1:h + 2, :] * jnp.maximum(pair[:, QT:], 0.0))
                acc = term if acc is None else acc + term
            return jnp.where(adm, _sortable(acc), INT_MIN)

        def far_keys(c, carry):
            c0 = pl.multiple_of(c * KEY_TILE, KEY_TILE)
            row = c0 + lax.broadcasted_iota(jnp.int32, (KEY_TILE, 1), 0)
            adm = jnp.logical_and(row >= first_real, row < win0)
            key_ref[c] = index_keys(ki_ref[pl.ds(c0, KEY_TILE), :], adm)
            return carry

        lax.fori_loop(0, n_far, far_keys, 0)
        row_n = win0 + lax.broadcasted_iota(jnp.int32, (WN, 1), 0)
        keyn_ref[...] = index_keys(ki_ref[near, :],jnp.logical_and(row_n >= first_real, ok_ref[...] != 0))

        def fold(hit):
            parts = hit.reshape(hit.shape[0] // SUBLANES, SUBLANES, QT)
            while parts.shape[0] > 1:
                half = parts.reshape(parts.shape[0] // 2, 2, SUBLANES, QT)
                parts = half[:, 0] + half[:, 1]
            return parts[0]

        def count_ge(c):
            part = lax.fori_loop(0, n_far, lambda t, p: p + fold(jnp.where(key_ref[t] >= c, 1.0, 0.0)),
                                 fold(jnp.where(keyn_ref[...] >= c, 1.0, 0.0)))
            return jnp.sum(part, axis=0, keepdims=True)

        zero = jnp.zeros((1, QT), jnp.int32)
        thr0 = jnp.where(count_ge(zero) >= k_sel, zero, INT_MIN)

        def bit_step(it, thr):
            cand = thr + jnp.left_shift(jnp.int32(1), jnp.int32(30) - it)
            return jnp.where(count_ge(cand) >= k_sel, cand, thr)

        thr = lax.fori_loop(0, 31, bit_step, thr0)
        thr = jnp.maximum(thr, INT_MIN + 1)

        m_ref[...] = jnp.full(m_ref.shape, M_INIT, F32)
        den_ref[...] = jnp.zeros(den_ref.shape, F32)
        acc_ref[...] = jnp.zeros(acc_ref.shape, F32)
        scale = A_HEAD_DIM ** -0.5

        def attend(h, logits, sel, vt_tiles):
            lg = jnp.where(sel, logits, NEG)
            m_old = m_ref[h]
            m_new = jnp.maximum(m_old, jnp.max(lg, axis=0, keepdims=True))
            p = jnp.exp(lg - m_new)
            alpha = jnp.exp(m_old - m_new)
            den_ref[h] = alpha * den_ref[h] + jnp.sum(p, axis=0, keepdims=True)
            pb = p.astype(BF16)
            pv = None
            for u, vt in enumerate(vt_tiles):
                term = _dot(vt, pb[u * LANES:(u + 1) * LANES])
                pv = term if pv is None else pv + term
            acc_ref[h] = alpha * acc_ref[h] + pv
            m_ref[h] = m_new

        def far_tile(c, carry):
            c0 = pl.multiple_of(c * KEY_TILE, KEY_TILE)
            sel = key_ref[c] >= thr
            for h in range(A_HEADS):
                sl = slice(h * A_HEAD_DIM, (h + 1) * A_HEAD_DIM)
                logits = _dot(k_ref[pl.ds(c0, KEY_TILE), sl], q_ref[0, h]) * scale
                attend(h, logits, sel,
                       [vt_ref[c * (KEY_TILE // LANES) + u, sl, :] for u in range(KEY_TILE // LANES)])
            return carry

        lax.fori_loop(0, n_far, far_tile, 0)
        sel_n = keyn_ref[...] >= thr
        wt = win0 // LANES
        for h in range(A_HEADS):
            sl = slice(h * A_HEAD_DIM, (h + 1) * A_HEAD_DIM)
            logits = _dot(k_ref[near, sl], q_ref[0, h]) * scale + bn_ref[h]
            attend(h, logits, sel_n, [vt_ref[wt + u, sl, :] for u in range(WN // LANES)])
            out_t = acc_ref[h] / jnp.maximum(den_ref[h], TINY)
            o_ref[:, sl] = out_t.T[:o_ref.shape[0]].astype(o_ref.dtype)


def _dsa(q_t, qi_t, w_t, k16, v_t, ki16, bias_near, near_ok, out_buf, *, batch, nb, n_keys, out_rows, out_base,
         first_real, qb0, skip_below, k_sel):
    width = k16.shape[1]
    return pl.pallas_call(
        functools.partial(_dsa_kernel, first_real=first_real, qb0=qb0, skip_below=skip_below, k_sel=k_sel),
        out_shape=jax.ShapeDtypeStruct(out_buf.shape, out_buf.dtype),
        grid=(batch, nb),
        in_specs=[
            pl.BlockSpec((1, A_HEADS, A_HEAD_DIM, QT), lambda b, j: (b * nb + j, 0, 0, 0)),
            pl.BlockSpec((1, IDX_HEADS, IDX_DIM, QT), lambda b, j: (b * nb + j, 0, 0, 0)),
            pl.BlockSpec((1, IDX_HEADS, QT), lambda b, j: (b * nb + j, 0, 0)),
            pl.BlockSpec((n_keys, width), lambda b, j: (b, 0)),
            pl.BlockSpec((n_keys // LANES, width, LANES), lambda b, j: (b, 0, 0)),
            pl.BlockSpec((n_keys, IDX_DIM), lambda b, j: (b, 0)),
            pl.BlockSpec((A_HEADS, WN, QT), lambda b, j: (0, 0, 0)),
            pl.BlockSpec((WN, QT), lambda b, j: (0, 0)),
            pl.BlockSpec(memory_space=pl.ANY),
        ],
        out_specs=pl.BlockSpec((out_rows, width), lambda b, j: (out_base + b * nb + j, 0)),
        scratch_shapes=[pltpu.VMEM((n_keys // KEY_TILE, KEY_TILE, QT), jnp.int32),
                        pltpu.VMEM((WN, QT), jnp.int32),
                        pltpu.VMEM((A_HEADS, 1, QT), F32),
                        pltpu.VMEM((A_HEADS, 1, QT), F32),
                        pltpu.VMEM((A_HEADS, A_HEAD_DIM, QT), F32)],
        input_output_aliases={8: 0},
        compiler_params=_cparams(("parallel", "arbitrary")),
        name="dsa",
    )(q_t, qi_t, w_t, k16, v_t, ki16, bias_near, near_ok, out_buf)


def _sample_keys_kernel(ck_ref, cv_ref, cki_ref, nk_ref, nv_ref, nki_ref, k_ref, vt_ref, ki_ref):
    c = pl.program_id(1)
    n_cache = pl.num_programs(1) - 1
    rows = k_ref.shape[0]

    @pl.when(c < n_cache)
    def _():
        for h in range(A_HEADS):
            sl = slice(h * A_HEAD_DIM, (h + 1) * A_HEAD_DIM)
            head_rows = pl.ds(h, rows, stride=A_HEADS)
            k_ref[:, sl] = ck_ref[0, head_rows, :].astype(BF16)
            vh = cv_ref[0, head_rows, :]
            for u in range(rows // LANES):
                vt_ref[u, sl, :] = vh[u * LANES:(u + 1) * LANES, :].T.astype(BF16)
        ki_ref[...] = cki_ref[0].astype(BF16)

    @pl.when(c == n_cache)
    def _():
        k_ref[...] = jnp.zeros(k_ref.shape, BF16)
        vt_ref[...] = jnp.zeros(vt_ref.shape, BF16)
        ki_ref[...] = jnp.zeros(ki_ref.shape, BF16)
        k_ref[0:CHUNK, :] = nk_ref[...].astype(BF16)
        ki_ref[0:CHUNK, :] = nki_ref[:, :IDX_DIM].astype(BF16)
        pad = jnp.zeros((LANES - CHUNK, A_HEAD_DIM), F32)
        for h in range(A_HEADS):
            sl = slice(h * A_HEAD_DIM, (h + 1) * A_HEAD_DIM)
            vt_ref[0, sl, :] = jnp.concatenate([nv_ref[:, sl], pad], axis=0).T.astype(BF16)


def _sample_keys(l, cache_k, cache_v, cache_ki, k_new, v_new, ki_new, *, new_row0, n_keys, rows=512):
    _, bs, past = cache_k.shape[:3]
    width = k_new.shape[1]
    n_cache = past // rows
    nb = -(-n_keys // rows)
    new_blk = new_row0 // CHUNK

    def cached(block_rows, w):
        return pl.BlockSpec((None, 1, block_rows, w), lambda b, c: (l, b, jnp.minimum(c, n_cache - 1), 0))

    def new(w):
        return pl.BlockSpec((CHUNK, w), lambda b, c: (new_blk + b, 0))

    kv_shape = cache_k.shape[:2] + (past * A_HEADS, A_HEAD_DIM)

    k16, v_t, ki16 = pl.pallas_call(
        _sample_keys_kernel,
        out_shape=(jax.ShapeDtypeStruct((bs, n_keys, width), BF16),
                   jax.ShapeDtypeStruct((bs, n_keys // LANES, width, LANES), BF16),
                   jax.ShapeDtypeStruct((bs, n_keys, IDX_DIM), BF16)),
        grid=(bs, nb),
        in_specs=[cached(rows * A_HEADS, A_HEAD_DIM), cached(rows * A_HEADS, A_HEAD_DIM), cached(rows, IDX_DIM),
                  new(width), new(width), new(LANES)],
        out_specs=(pl.BlockSpec((None, rows, width), lambda b, c: (b, c, 0)),
                   pl.BlockSpec((None, rows // LANES, width, LANES), lambda b, c: (b, c, 0, 0)),
                   pl.BlockSpec((None, rows, IDX_DIM), lambda b, c: (b, c, 0))),
        compiler_params=_cparams(("parallel", "arbitrary")),
        name="sample_keys",
    )(cache_k.reshape(kv_shape), cache_v.reshape(kv_shape), cache_ki, k_new, v_new, ki_new)
    return (k16.reshape(bs * n_keys, width), v_t.reshape(bs * (n_keys // LANES), width, LANES),
            ki16.reshape(bs * n_keys, IDX_DIM))


def _t5_bucket(rel):
    half = T5_BUCKETS // 2
    exact = half // 2
    n = jnp.abs(rel)
    large = exact + (jnp.log(jnp.maximum(n, 1).astype(F32) / exact)
                     / math.log(T5_MAX_DIST / exact) * (half - exact)).astype(jnp.int32)
    large = jnp.minimum(large, half - 1)
    return jnp.where(rel > 0, half, 0) + jnp.where(n < exact, n, large)


def _bias_tables(t5_bias):
    a = jnp.arange(WN, dtype=jnp.int32)[:, None]
    t = jnp.arange(QT, dtype=jnp.int32)[None, :]
    bucket = _t5_bucket(a - (WN - QT) - t)
    onehot = (bucket[:, :, None] == jnp.arange(T5_BUCKETS, dtype=jnp.int32)).astype(F32)
    near = jnp.einsum("atk,kh->hat", onehot, t5_bias.astype(F32), precision=lax.Precision.HIGHEST)
    far = t5_bias[_t5_bucket(jnp.int32(-2 * CHUNK - 1))].astype(F32)
    ok = (a // CHUNK - (WN - QT) // CHUNK <= t // CHUNK).astype(jnp.int32)
    return near - far[:, None, None], ok


def _rope_tables(pos):
    half = RET_HEAD_DIM // 2
    inv = ROPE_BASE ** (-jnp.arange(half, dtype=F32) / half)
    ang = pos.astype(F32)[:, None] * inv[None, :]
    cos, sin = jnp.cos(ang), jnp.sin(ang)
    return jnp.concatenate([cos, cos], axis=-1), jnp.concatenate([-sin, sin], axis=-1)


def _decay_tables():
    n = CHUNK
    log_g = jnp.log(1.0 - 2.0 ** (-5.0 - jnp.arange(RET_HEADS, dtype=F32)))
    i = jnp.arange(n, dtype=F32)
    diff = i[:, None] - i[None, :]
    dmat = jnp.where(diff >= 0, jnp.exp(jnp.maximum(diff, 0.0)[None] * log_g[:, None, None]), 0.0)
    cross = jnp.exp((i[None, :] + 1.0) * log_g[:, None])
    kdec = jnp.exp((n - 1.0 - i)[None, :] * log_g[:, None])
    gn = jnp.exp(n * log_g)
    wide = (RET_HEADS, n, RET_HEAD_DIM)
    return (dmat, jnp.broadcast_to(cross[:, :, None], wide), jnp.broadcast_to(kdec[:, :, None], wide),
            jnp.broadcast_to(gn[:, None, None], (RET_HEADS, 1, RET_HEAD_DIM)))


def kernel(x_prompt, x_sample, cache_k, cache_v, cache_ki, cache_pool, state_ret, meta_tokens, ln_in_g, ln_in_b, w_in, t5_bias, pool_w, pool_scale, w_branch, w_gate, b_gate, w_out, ln1_g, ln1_b, ln2_g, ln2_b, ffn_w_gate, ffn_w_up, ffn_w_down, moe_w_router, moe_b_router, moe_w_gate, moe_w_up, moe_w_down):
    bp, seq, d = x_prompt.shape
    bs, ts, _ = x_sample.shape
    past = cache_k.shape[2]
    t_real = seq + N_META
    tp = -(-(t_real + 2 * CHUNK) // KEY_TILE) * KEY_TILE
    front = tp - t_real
    assert front % CHUNK == CHUNK - N_META and ts == CHUNK and past % CHUNK == 0
    mp, ms = bp * tp, bs * ts
    m = mp + ms
    assert mp % TM == 0 and ms % TM == 0
    ksel_p = min(TOPK_MAX, seq // 4)
    ksel_s = min(TOPK_MAX, (past + ts) // 4)
    n_keys_s = -(-(past + QT) // KEY_TILE) * KEY_TILE
    assert tp % QT == 0 and front >= WN - QT and past % QT == 0 and past + QT >= WN

    meta = jnp.broadcast_to(meta_tokens.astype(F32)[None], (bp, N_META, d))
    x_rows = jnp.concatenate(
        [jnp.concatenate([jnp.zeros((bp, front, d), F32), meta, x_prompt], axis=1).reshape(mp, d),
         x_sample.reshape(ms, d)], axis=0)
    x32, xb = _layer_norm(x_rows, None, ln_in_g, ln_in_b)

    bias_near, near_ok = _bias_tables(t5_bias)
    cos_p, sin_p = _rope_tables(jnp.arange(tp, dtype=jnp.int32) - (front + N_META))
    cos_s, sin_s = _rope_tables(past + jnp.arange(ts, dtype=jnp.int32))
    decay = _decay_tables()
    zero_state = jnp.zeros((bp, RET_HEADS, RET_HEAD_DIM, RET_HEAD_DIM), F32)
    w_in_t = jnp.swapaxes(w_in, 1, 2)
    w_tail_t = w_in_t[:, TAIL_OFF:, :]
    row_in_seq = jnp.arange(mp, dtype=jnp.int32) % tp
    token_rows = jnp.concatenate([row_in_seq >= front, jnp.ones((ms,), jnp.bool_)])

    outs = {name: [] for name in ("kp", "vp", "kip", "poolp", "retp", "ks", "vs", "kis", "pools", "rets")}
    bw = BRANCH_WIDTH
    for l in range(DEPTH):
        q16, q_t = _proj(xb, w_in_t, (l,), 0, nat16=True, trans=True, name="proj_q")
        k_new, k16 = _proj(xb, w_in_t, (l,), 1, nat32=True, nat16=True, name="proj_k")
        v_new, v_t = _proj(xb, w_in_t, (l,), 2, nat32=True, trans=True, name="proj_v")
        qi16, qi_t = _proj(xb, w_in_t, (l,), 3, nat16=True, trans=True, name="proj_qi")
        p2, ki16, w_t = _proj_idx(xb, w_in_t, (l,), 4 * bw // LANES)
        p3 = _matmul(xb, w_tail_t, (l,), 0, 5, bw, name="proj_tail", w_transposed=True)
        ki_new = p2[:, :IDX_DIM]
        qi_t = qi_t.reshape(m // LANES, IDX_HEADS, IDX_DIM, LANES)
        v_t = v_t.reshape(m // LANES, bw, LANES)

        def sample_queries_t(a16, heads):
            a = jnp.pad(a16[mp:].reshape(bs, ts, heads, bw // heads), ((0, 0), (0, QT - ts), (0, 0), (0, 0)))
            return jnp.transpose(a, (0, 2, 3, 1))

        w_t_s = jnp.transpose(jnp.pad(p2[mp:, IDX_DIM:IDX_DIM + IDX_HEADS].reshape(bs, ts, IDX_HEADS),
                                      ((0, 0), (0, QT - ts), (0, 0))), (0, 2, 1))
        k16_s, v_t_s, ki16_s = _sample_keys(l, cache_k, cache_v, cache_ki, k_new, v_new, p2, new_row0=mp,
                                            n_keys=n_keys_s)

        branch_buf = jnp.zeros((m, bw), BF16)
        oa = _dsa(q_t, qi_t, w_t, k16, v_t, ki16, bias_near, near_ok, branch_buf, batch=bp, nb=tp // QT, n_keys=tp,
                  out_rows=QT, out_base=0, first_real=front, qb0=0, skip_below=max(1, front // QT), k_sel=ksel_p)
        oa = _dsa(sample_queries_t(q16, A_HEADS), sample_queries_t(qi16, IDX_HEADS), w_t_s, k16_s, v_t_s, ki16_s,
                  bias_near, near_ok, oa, batch=bs, nb=1, n_keys=n_keys_s, out_rows=ts, out_base=mp // ts,
                  first_real=0, qb0=past // CHUNK, skip_below=0, k_sel=ksel_s)

        tb_p = 256
        ob = _pool(p3, 0, p3, 0, pool_w[l], pool_scale[l], branch_buf, batch=bp, rows=tp, tb=tb_p, row0=0,
                   halo_map=lambda b, i: jnp.maximum((b * tp + i * tb_p) // HALO - 1, 0), first_valid=front)
        pool_hist = jnp.pad(cache_pool[l], ((0, 0), (HALO - POOL_PAST, 0), (0, 0))).reshape(bs * HALO, bw)
        ob = _pool(p3, 0, pool_hist, 0, pool_w[l], pool_scale[l], ob, batch=bs, rows=ts, tb=ts, row0=mp,
                   halo_map=lambda b, i: b, first_valid=-POOL_PAST)

        oc, ret_p = _retention(p3, zero_state, cos_p, sin_p, decay, branch_buf, batch=bp, rows=tp, row0=0,
                               first_valid=front)
        oc, ret_s = _retention(p3, state_ret[l].astype(F32), cos_s, sin_s, decay, oc, batch=bs, rows=ts, row0=mp,
                               first_valid=0)

        merged = _merge(l, xb, oa, ob, oc, w_branch, w_gate, b_gate)
        y = _matmul(merged, w_out, (l,), 0, d // 512, 512, name="w_out")
        x32, xb = _layer_norm(x32, y, ln1_g[l], ln1_b[l])

        if l % 2 == 0:
            h = _gate_up(xb, ffn_w_gate, ffn_w_up, (l // 2,))
            f = _matmul(h, ffn_w_down, (l // 2,), 0, d // 512, 512, name="ffn_down")
        else:
            gates, experts = _router(x32, moe_w_router[l // 2], moe_b_router[l // 2])
            src, gate_sorted, pos, tile_expert, n_used = _dispatch_plan(experts, gates, MOE_TILE, token_rows)
            x_sorted = _gather_rows(x32, src, MOE_TILE)
            y_sorted = _moe_experts(x_sorted, gate_sorted, tile_expert, n_used, moe_w_gate, moe_w_up, moe_w_down,
                                    l // 2, MOE_TILE)
            f = _combine_rows(y_sorted, pos, m, MOE_TILE)
        x32, xb = _layer_norm(x32, f, ln2_g[l], ln2_b[l])

        def prompt_rows(a, width):
            return a[:mp].reshape(bp, tp, width)[:, front:]

        u = p3[:, :bw]
        outs["kp"].append(prompt_rows(k_new, bw).reshape(bp, t_real, A_HEADS, A_HEAD_DIM))
        outs["vp"].append(prompt_rows(v_new, bw).reshape(bp, t_real, A_HEADS, A_HEAD_DIM))
        outs["kip"].append(prompt_rows(ki_new, IDX_DIM))
        outs["poolp"].append(prompt_rows(u, bw)[:, -POOL_PAST:])
        outs["retp"].append(ret_p)
        outs["ks"].append(k_new[mp:].reshape(bs, ts, A_HEADS, A_HEAD_DIM))
        outs["vs"].append(v_new[mp:].reshape(bs, ts, A_HEADS, A_HEAD_DIM))
        outs["kis"].append(ki_new[mp:].reshape(bs, ts, IDX_DIM))
        outs["pools"].append(u[mp:].reshape(bs, ts, bw)[:, -POOL_PAST:])
        outs["rets"].append(ret_s)

    y_prompt = x32[:mp].reshape(bp, tp, d)[:, front + N_META:]
    y_sample = x32[mp:].reshape(bs, ts, d)
    return (y_prompt, y_sample) + tuple(
        jnp.stack(outs[name]) for name in ("kp", "vp", "kip", "poolp", "retp", "ks", "vs", "kis", "pools", "rets"))
```

```python
import functools
import math

import jax
import jax.numpy as jnp
from jax import lax
from jax.experimental import pallas as pl
from jax.experimental.pallas import tpu as pltpu

F32 = jnp.float32
BF16 = jnp.bfloat16

D_MODEL = 2048
DEPTH = 2
CHUNK = 64
N_META = 16
BRANCH_WIDTH = D_MODEL // 2
A_HEADS = 8
A_HEAD_DIM = BRANCH_WIDTH // A_HEADS
IDX_HEADS = 16
IDX_DIM = 64
TOPK_MAX = 256
T5_BUCKETS = 32
T5_MAX_DIST = 128
POOL_WINDOWS = (2, 4, 8, 16)
POOL_GROUP = BRANCH_WIDTH // 4
POOL_PAST = 15
RET_HEADS = 8
RET_HEAD_DIM = BRANCH_WIDTH // RET_HEADS
ROPE_BASE = 10000.0
N_BRANCH = 3
D_FF = 11 * D_MODEL // 4
N_EXPERTS = 8
TOP_K = 2
ALPHA = (2 * DEPTH) ** 0.25
LN_EPS = 1e-5
IN_SPLITS = (BRANCH_WIDTH, BRANCH_WIDTH, BRANCH_WIDTH, IDX_HEADS * IDX_DIM, IDX_DIM, IDX_HEADS,
             BRANCH_WIDTH, BRANCH_WIDTH, BRANCH_WIDTH, BRANCH_WIDTH, BRANCH_WIDTH)
IN_WIDTH = sum(IN_SPLITS)
TAIL_OFF = 4 * BRANCH_WIDTH + IDX_DIM + IDX_HEADS

LANES = 128
SUBLANES = 8
HALO = 16
QT = 2 * CHUNK
WN = QT + 2 * CHUNK
KEY_TILE = 256
VMEM_LIMIT = 56 * 1024 * 1024
TM = 512
MOE_TILE = 512
PACKED_ROWS = 16
INT_MIN = -2 ** 31
HALF_MIN = -2 ** 15
NEG = -1e30
M_INIT = -1e20
TINY = 1e-30


def _cparams(sem):
    return pltpu.CompilerParams(dimension_semantics=sem, vmem_limit_bytes=VMEM_LIMIT)


def _dot(a, b):
    return jnp.dot(a, b, preferred_element_type=F32)


def _dot_nt(a, b):
    return lax.dot_general(a, b, (((1,), (1,)), ((), ())), preferred_element_type=F32)


def _dot_tn(a, b):
    return lax.dot_general(a, b, (((0,), (0,)), ((), ())), preferred_element_type=F32)


def _ln_kernel(x_ref, g_ref, b_ref, o32_ref, o16_ref):
    x = x_ref[...]
    mu = jnp.mean(x, axis=-1, keepdims=True)
    xc = x - mu
    var = jnp.mean(xc * xc, axis=-1, keepdims=True)
    y = xc * lax.rsqrt(var + LN_EPS) * g_ref[...] + b_ref[...]
    o32_ref[...] = y
    o16_ref[...] = y.astype(BF16)


def _ln_res_kernel(x_ref, y_ref, g_ref, b_ref, o32_ref, o16_ref):
    x = ALPHA * x_ref[...] + y_ref[...]
    mu = jnp.mean(x, axis=-1, keepdims=True)
    xc = x - mu
    var = jnp.mean(xc * xc, axis=-1, keepdims=True)
    y = xc * lax.rsqrt(var + LN_EPS) * g_ref[...] + b_ref[...]
    o32_ref[...] = y
    o16_ref[...] = y.astype(BF16)


def _layer_norm(x, y, g, b, tm=256):
    m, d = x.shape
    row = pl.BlockSpec((tm, d), lambda i: (i, 0))
    vec = pl.BlockSpec((1, d), lambda i: (0, 0))
    args = (x,) if y is None else (x, y)
    return pl.pallas_call(
        _ln_kernel if y is None else _ln_res_kernel,
        out_shape=(jax.ShapeDtypeStruct((m, d), F32), jax.ShapeDtypeStruct((m, d), BF16)),
        grid=(m // tm,),
        in_specs=[row] * len(args) + [vec, vec],
        out_specs=(row, row),
        compiler_params=_cparams(("parallel",)),
        name="layer_norm",
    )(*args, g.reshape(1, d), b.reshape(1, d))


def _mm_kernel(x_ref, w_ref, o_ref, wb_ref, *, w_rows_are_outputs):
    @pl.when(pl.program_id(1) == 0)
    def _():
        w = w_ref[...]
        wb_ref[...] = (w.T if w_rows_are_outputs else w).astype(BF16)

    o_ref[...] = _dot(x_ref[...], wb_ref[...]).astype(o_ref.dtype)


def _matmul(x, w, lead, col0, n_tiles, tn, name="matmul", w_transposed=False):
    m, k = x.shape
    nl = len(lead)
    if w_transposed:
        w_spec = pl.BlockSpec((None,) * nl + (tn, k), lambda j, i: tuple(lead) + (j + col0, 0))
    else:
        w_spec = pl.BlockSpec((None,) * nl + (k, tn), lambda j, i: tuple(lead) + (0, j + col0))
    x_spec = pl.BlockSpec((TM, k), lambda j, i: (i, 0))
    o_spec = pl.BlockSpec((TM, tn), lambda j, i: (i, j))
    return pl.pallas_call(
        functools.partial(_mm_kernel, w_rows_are_outputs=w_transposed),
        out_shape=jax.ShapeDtypeStruct((m, n_tiles * tn), F32),
        grid=(n_tiles, m // TM),
        in_specs=[x_spec, w_spec],
        out_specs=o_spec,
        scratch_shapes=[pltpu.VMEM((k, tn), BF16)],
        compiler_params=_cparams(("arbitrary", "arbitrary")),
        name=name,
    )(x, w)


def _store_lane_tiles_t(res, out_ref):
    rows, width = res.shape
    for r in range(rows // LANES):
        for c in range(width // LANES):
            tile = res[r * LANES:(r + 1) * LANES, c * LANES:(c + 1) * LANES]
            out_ref[r, c] = tile.T.astype(out_ref.dtype)


def _proj_kernel(x_ref, w_ref, *refs, nat32, nat16, trans):
    outs, wb_ref = list(refs[:-1]), refs[-1]

    @pl.when(pl.program_id(0) == 0)
    def _():
        wb_ref[...] = w_ref[...].T.astype(BF16)

    res = _dot(x_ref[...], wb_ref[...])
    if nat32:
        outs.pop(0)[...] = res
    if nat16:
        outs.pop(0)[...] = res.astype(BF16)
    if trans:
        _store_lane_tiles_t(res, outs.pop(0))


def _proj(x, w, lead, col0, *, nat32=False, nat16=False, trans=False, tm=TM, name="proj"):
    m, k = x.shape
    bw = BRANCH_WIDTH
    nl = len(lead)
    shapes, specs = [], []
    row_spec = pl.BlockSpec((tm, bw), lambda i: (i, 0))
    if nat32:
        shapes.append(jax.ShapeDtypeStruct((m, bw), F32)); specs.append(row_spec)
    if nat16:
        shapes.append(jax.ShapeDtypeStruct((m, bw), BF16)); specs.append(row_spec)
    if trans:
        shapes.append(jax.ShapeDtypeStruct((m // LANES, bw // LANES, LANES, LANES), BF16))
        specs.append(pl.BlockSpec((tm // LANES, bw // LANES, LANES, LANES), lambda i: (i, 0, 0, 0)))
    return pl.pallas_call(
        functools.partial(_proj_kernel, nat32=nat32, nat16=nat16, trans=trans),
        out_shape=tuple(shapes),
        grid=(m // tm,),
        in_specs=[pl.BlockSpec((tm, k), lambda i: (i, 0)),
                  pl.BlockSpec((None,) * nl + (bw, k), lambda i: tuple(lead) + (col0, 0))],
        out_specs=tuple(specs),
        scratch_shapes=[pltpu.VMEM((k, bw), BF16)],
        compiler_params=_cparams(("arbitrary",)),
        name=name,
    )(x, w)


def _proj_idx_kernel(x_ref, w_ref, nat_ref, ki_ref, wt_ref, wb_ref):
    @pl.when(pl.program_id(0) == 0)
    def _():
        wb_ref[...] = w_ref[...].T.astype(BF16)

    res = _dot(x_ref[...], wb_ref[...])
    nat_ref[...] = res
    ki_ref[...] = res[:, :IDX_DIM].astype(BF16)
    for r in range(res.shape[0] // LANES):
        wt_ref[r] = res[r * LANES:(r + 1) * LANES, :].T[IDX_DIM:IDX_DIM + IDX_HEADS, :]


def _proj_idx(x, w, lead, col_block, tm=TM):
    m, k = x.shape
    nl = len(lead)
    return pl.pallas_call(
        _proj_idx_kernel,
        out_shape=(jax.ShapeDtypeStruct((m, LANES), F32), jax.ShapeDtypeStruct((m, IDX_DIM), BF16),
                   jax.ShapeDtypeStruct((m // LANES, IDX_HEADS, LANES), F32)),
        grid=(m // tm,),
        in_specs=[pl.BlockSpec((tm, k), lambda i: (i, 0)),
                  pl.BlockSpec((None,) * nl + (LANES, k), lambda i: tuple(lead) + (col_block, 0))],
        out_specs=(pl.BlockSpec((tm, LANES), lambda i: (i, 0)), pl.BlockSpec((tm, IDX_DIM), lambda i: (i, 0)),
                   pl.BlockSpec((tm // LANES, IDX_HEADS, LANES), lambda i: (i, 0, 0))),
        scratch_shapes=[pltpu.VMEM((k, LANES), BF16)],
        compiler_params=_cparams(("arbitrary",)),
        name="proj_idx",
    )(x, w)


def _gate_up_kernel(x_ref, wg_ref, wu_ref, o_ref, wgb_ref, wub_ref):
    @pl.when(pl.program_id(1) == 0)
    def _():
        wgb_ref[...] = wg_ref[...].astype(BF16)
        wub_ref[...] = wu_ref[...].astype(BF16)

    x = x_ref[...]
    g = _dot(x, wgb_ref[...])
    u = _dot(x, wub_ref[...])
    o_ref[...] = (g * jax.nn.sigmoid(g) * u).astype(o_ref.dtype)


def _gate_up(x, wg, wu, lead, tn=512):
    m, k = x.shape
    n = wg.shape[-1]
    nl = len(lead)
    w_spec = pl.BlockSpec((None,) * nl + (k, tn), lambda j, i: tuple(lead) + (0, j))
    x_spec = pl.BlockSpec((TM, k), lambda j, i: (i, 0))
    return pl.pallas_call(
        _gate_up_kernel,
        out_shape=jax.ShapeDtypeStruct((m, n), BF16),
        grid=(n // tn, m // TM),
        in_specs=[x_spec, w_spec, w_spec],
        out_specs=pl.BlockSpec((TM, tn), lambda j, i: (i, j)),
        scratch_shapes=[pltpu.VMEM((k, tn), BF16), pltpu.VMEM((k, tn), BF16)],
        compiler_params=_cparams(("arbitrary", "arbitrary")),
        name="gate_up",
    )(x, wg, wu)


def _merge_kernel(x_ref, oa_ref, ob_ref, oc_ref, wg0_ref, wg1_ref, wg2_ref, wb0_ref, wb1_ref, wb2_ref,
                  bg0_ref, bg1_ref, bg2_ref, o_ref, wgb_ref, wbb_ref):
    wg_refs = (wg0_ref, wg1_ref, wg2_ref)
    wb_refs = (wb0_ref, wb1_ref, wb2_ref)

    @pl.when(pl.program_id(1) == 0)
    def _():
        for n in range(N_BRANCH):
            wgb_ref[n] = wg_refs[n][...].astype(BF16)
            wbb_ref[n] = wb_refs[n][...].astype(BF16)

    x = x_ref[...]
    acc = None
    for n, (o_in, bg) in enumerate(zip((oa_ref, ob_ref, oc_ref), (bg0_ref, bg1_ref, bg2_ref))):
        gate = jax.nn.sigmoid(_dot(x, wgb_ref[n]) + bg[...])
        term = gate * _dot(o_in[...], wbb_ref[n])
        acc = term if acc is None else acc + term
    o_ref[...] = acc.astype(o_ref.dtype)


def _merge(l, x, oa, ob, oc, w_branch, w_gate, b_gate, tn=256):
    m, d = x.shape
    w = oa.shape[1]
    nt = d // tn
    x_spec = pl.BlockSpec((TM, d), lambda j, i: (i, 0))
    o_in_spec = pl.BlockSpec((TM, w), lambda j, i: (i, 0))
    wg_specs = [pl.BlockSpec((None, d, tn), lambda j, i, n=n: (l, 0, n * nt + j)) for n in range(N_BRANCH)]
    wb_specs = [pl.BlockSpec((None, None, w, tn), lambda j, i, n=n: (l, n, 0, j)) for n in range(N_BRANCH)]
    bg_specs = [pl.BlockSpec((None, 1, tn), lambda j, i, n=n: (l, 0, n * nt + j)) for n in range(N_BRANCH)]
    return pl.pallas_call(
        _merge_kernel,
        out_shape=jax.ShapeDtypeStruct((m, d), BF16),
        grid=(nt, m // TM),
        in_specs=[x_spec, o_in_spec, o_in_spec, o_in_spec] + wg_specs + wb_specs + bg_specs,
        out_specs=pl.BlockSpec((TM, tn), lambda j, i: (i, j)),
        scratch_shapes=[pltpu.VMEM((N_BRANCH, d, tn), BF16), pltpu.VMEM((N_BRANCH, w, tn), BF16)],
        compiler_params=_cparams(("arbitrary", "arbitrary")),
        name="merge",
    )(x, oa, ob, oc, w_gate, w_gate, w_gate, w_branch, w_branch, w_branch,
      b_gate.reshape(DEPTH, 1, N_BRANCH * d), b_gate.reshape(DEPTH, 1, N_BRANCH * d),
      b_gate.reshape(DEPTH, 1, N_BRANCH * d))


def _split_bf16(a):
    hi = a.astype(BF16)
    lo = (a - hi.astype(F32)).astype(BF16)
    return hi, lo


def _router_kernel(x_ref, w_ref, b_ref, gate_ref, expert_ref):
    xh, xl = _split_bf16(x_ref[...])
    wh, wl = _split_bf16(w_ref[...])
    logits = _dot(xh, wh) + (_dot(xh, wl) + _dot(xl, wh)) + b_ref[...]
    lane = lax.broadcasted_iota(jnp.int32, logits.shape, 1)
    logits = jnp.where(lane < N_EXPERTS, logits, -jnp.inf)
    m1 = jnp.max(logits, axis=-1, keepdims=True)
    i1 = jnp.min(jnp.where(logits == m1, lane, LANES), axis=-1, keepdims=True)
    rest = jnp.where(lane == i1, -jnp.inf, logits)
    m2 = jnp.max(rest, axis=-1, keepdims=True)
    i2 = jnp.min(jnp.where(rest == m2, lane, LANES), axis=-1, keepdims=True)
    e = jnp.exp(m2 - m1)
    p1 = 1.0 / (1.0 + e)
    p2 = e / (1.0 + e)
    gate_ref[...] = jnp.where(lane == 0, p1, jnp.where(lane == 1, p2, 0.0))
    expert_ref[...] = jnp.where(lane == 0, i1, jnp.where(lane == 1, i2, 0))


def _router(x32, w_r, b_r, tm=256):
    m, d = x32.shape
    w_pad = jnp.pad(w_r, ((0, 0), (0, LANES - N_EXPERTS)))
    b_pad = jnp.pad(b_r, (0, LANES - N_EXPERTS)).reshape(1, LANES)
    out_spec = pl.BlockSpec((tm, LANES), lambda i: (i, 0))
    gates, experts = pl.pallas_call(
        _router_kernel,
        out_shape=(jax.ShapeDtypeStruct((m, LANES), F32), jax.ShapeDtypeStruct((m, LANES), jnp.int32)),
        grid=(m // tm,),
        in_specs=[pl.BlockSpec((tm, d), lambda i: (i, 0)), pl.BlockSpec((d, LANES), lambda i: (0, 0)),
                  pl.BlockSpec((1, LANES), lambda i: (0, 0))],
        out_specs=(out_spec, out_spec),
        compiler_params=_cparams(("parallel",)),
        name="router",
    )(x32, w_pad, b_pad)
    return gates[:, :TOP_K], experts[:, :TOP_K]


def _dispatch_plan(experts, tile, token_rows):
    m = experts.shape[0]
    n_assign = TOP_K * m
    n_tiles = n_assign // tile + N_EXPERTS
    e = experts.reshape(n_assign)
    live = jnp.repeat(token_rows, TOP_K)
    onehot = jnp.logical_and(e[:, None] == jnp.arange(N_EXPERTS, dtype=jnp.int32)[None, :],
                             live[:, None]).astype(jnp.int32)
    rank = jnp.sum((jnp.cumsum(onehot, axis=0) - onehot) * onehot, axis=1)
    tiles_per = (jnp.sum(onehot, axis=0) + tile - 1) // tile
    tile_end = jnp.cumsum(tiles_per)
    pos = ((tile_end - tiles_per)[e] * tile + rank).astype(jnp.int32)
    slot = jnp.where(live, pos, n_tiles * tile)
    src = jnp.zeros((n_tiles * tile,), jnp.int32).at[slot].set(jnp.arange(n_assign, dtype=jnp.int32) // TOP_K,
                                                               mode="drop")
    pos = jnp.where(live, pos, 0)
    n_used = tile_end[-1:]
    t_idx = jnp.minimum(jnp.arange(n_tiles, dtype=jnp.int32), n_used - 1)
    tile_expert = jnp.sum((t_idx[:, None] >= tile_end[None, :]).astype(jnp.int32), axis=1)
    return src, pos, tile_expert.astype(jnp.int32), n_used.astype(jnp.int32)


def _issue_rows(idx_ref, first, count, stride, src_hbm, dst, sem):
    def body(i, carry):
        pltpu.make_async_copy(src_hbm.at[pl.ds(idx_ref[first + i * stride], 1), :], dst.at[pl.ds(i, 1), :],
                              sem).start()
        return carry

    lax.fori_loop(0, count, body, 0, unroll=8)


def _wait_rows(count, src_hbm, dst, sem):
    pltpu.make_async_copy(src_hbm.at[pl.ds(0, count), :], dst, sem).wait()


def _gather_kernel(src_ref, nu_ref, x_hbm, o_ref, buf, sem, *, tile):
    t = pl.program_id(0)

    @pl.when(t == 0)
    def _():
        _issue_rows(src_ref, 0, tile, 1, x_hbm, buf.at[0], sem.at[0])

    @pl.when(t + 1 < nu_ref[0])
    def _():
        nxt = (t + 1) % 2
        _issue_rows(src_ref, (t + 1) * tile, tile, 1, x_hbm, buf.at[nxt], sem.at[nxt])

    @pl.when(t < nu_ref[0])
    def _():
        cur = t % 2
        _wait_rows(tile, x_hbm, buf.at[cur], sem.at[cur])
        o_ref[...] = buf[cur].astype(o_ref.dtype)

    @pl.when(t >= nu_ref[0])
    def _():
        o_ref[...] = jnp.zeros(o_ref.shape, o_ref.dtype)


def _gather_rows(x32, src, n_used, tile):
    d = x32.shape[1]
    n_rows = src.shape[0]
    return pl.pallas_call(
        functools.partial(_gather_kernel, tile=tile),
        out_shape=jax.ShapeDtypeStruct((n_rows, d), BF16),
        grid_spec=pltpu.PrefetchScalarGridSpec(
            num_scalar_prefetch=2, grid=(n_rows // tile,),
            in_specs=[pl.BlockSpec(memory_space=pl.ANY)],
            out_specs=pl.BlockSpec((tile, d), lambda t, src, nu: (t, 0)),
            scratch_shapes=[pltpu.VMEM((2, tile, d), F32), pltpu.SemaphoreType.DMA((2,))]),
        compiler_params=_cparams(("arbitrary",)),
        name="moe_gather",
    )(src, n_used, x32)


def _combine_kernel(pos_ref, y_hbm, g_ref, o_ref, buf, sem, *, tile):
    t = pl.program_id(0)

    def issue(tile_idx, slot):
        for s in range(TOP_K):
            _issue_rows(pos_ref, tile_idx * tile * TOP_K + s, tile, TOP_K, y_hbm, buf.at[slot, s], sem.at[slot])

    @pl.when(t == 0)
    def _():
        issue(0, 0)

    @pl.when(t + 1 < pl.num_programs(0))
    def _():
        issue(t + 1, (t + 1) % 2)

    cur = t % 2
    for s in range(TOP_K):
        _wait_rows(tile, y_hbm, buf.at[cur, s], sem.at[cur])
    gates = g_ref[...]
    total = buf[cur, 0] * gates[:, 0:1]
    for s in range(1, TOP_K):
        total = total + buf[cur, s] * gates[:, s:s + 1]
    o_ref[...] = total


def _combine_rows(y_sorted, pos, gates, tile):
    m = gates.shape[0]
    d = y_sorted.shape[1]
    return pl.pallas_call(
        functools.partial(_combine_kernel, tile=tile),
        out_shape=jax.ShapeDtypeStruct((m, d), F32),
        grid_spec=pltpu.PrefetchScalarGridSpec(
            num_scalar_prefetch=1, grid=(m // tile,),
            in_specs=[pl.BlockSpec(memory_space=pl.ANY), pl.BlockSpec((tile, TOP_K), lambda t, pos: (t, 0))],
            out_specs=pl.BlockSpec((tile, d), lambda t, pos: (t, 0)),
            scratch_shapes=[pltpu.VMEM((2, TOP_K, tile, d), F32), pltpu.SemaphoreType.DMA((2,))]),
        compiler_params=_cparams(("arbitrary",)),
        name="moe_combine",
    )(pos, y_sorted, gates)


def _expert_changed(te_ref, t):
    return jnp.logical_or(t == 0, te_ref[t] != te_ref[jnp.maximum(t - 1, 0)])


def _moe_gate_up_kernel(te_ref, nu_ref, x_ref, wg_ref, wu_ref, o_ref, wgb_ref, wub_ref):
    t = pl.program_id(1)

    @pl.when(_expert_changed(te_ref, t))
    def _():
        wgb_ref[...] = wg_ref[...].astype(BF16)
        wub_ref[...] = wu_ref[...].astype(BF16)

    @pl.when(t < nu_ref[0])
    def _():
        x = x_ref[...]
        g = _dot(x, wgb_ref[...])
        u = _dot(x, wub_ref[...])
        o_ref[...] = (g * jax.nn.sigmoid(g) * u).astype(o_ref.dtype)

    @pl.when(t >= nu_ref[0])
    def _():
        o_ref[...] = jnp.zeros(o_ref.shape, o_ref.dtype)


def _moe_down_kernel(te_ref, nu_ref, h_ref, w_ref, o_ref, wb_ref):
    t = pl.program_id(1)

    @pl.when(_expert_changed(te_ref, t))
    def _():
        wb_ref[...] = w_ref[...].astype(BF16)

    @pl.when(t < nu_ref[0])
    def _():
        o_ref[...] = _dot(h_ref[...], wb_ref[...])

    @pl.when(t >= nu_ref[0])
    def _():
        o_ref[...] = jnp.zeros(o_ref.shape, o_ref.dtype)


def _moe_experts(x_sorted, tile_expert, n_used, w_gate, w_up, w_down, layer, tile, tn=512):
    r, d = x_sorted.shape
    ff = w_gate.shape[-1]
    n_tiles = r // tile

    def x_rows(width):
        return pl.BlockSpec((tile, width), lambda j, t, te, nu: (jnp.minimum(t, nu[0] - 1), 0))

    def w_cols(k):
        return pl.BlockSpec((None, None, k, tn), lambda j, t, te, nu: (layer, te[t], 0, j))

    h = pl.pallas_call(
        _moe_gate_up_kernel,
        out_shape=jax.ShapeDtypeStruct((r, ff), BF16),
        grid_spec=pltpu.PrefetchScalarGridSpec(
            num_scalar_prefetch=2, grid=(ff // tn, n_tiles),
            in_specs=[x_rows(d), w_cols(d), w_cols(d)],
            out_specs=pl.BlockSpec((tile, tn), lambda j, t, te, nu: (t, j)),
            scratch_shapes=[pltpu.VMEM((d, tn), BF16), pltpu.VMEM((d, tn), BF16)]),
        compiler_params=_cparams(("arbitrary", "arbitrary")),
        name="moe_gate_up",
    )(tile_expert, n_used, x_sorted, w_gate, w_up)
    return pl.pallas_call(
        _moe_down_kernel,
        out_shape=jax.ShapeDtypeStruct((r, d), F32),
        grid_spec=pltpu.PrefetchScalarGridSpec(
            num_scalar_prefetch=2, grid=(d // tn, n_tiles),
            in_specs=[x_rows(ff), w_cols(ff)],
            out_specs=pl.BlockSpec((tile, tn), lambda j, t, te, nu: (t, j)),
            scratch_shapes=[pltpu.VMEM((ff, tn), BF16)]),
        compiler_params=_cparams(("arbitrary", "arbitrary")),
        name="moe_down",
    )(tile_expert, n_used, h, w_down)


def _pool_kernel(u_ref, halo_ref, w_ref, sc_ref, _buf_ref, o_ref, ext_ref, *, tb, first_valid):
    row0 = pl.program_id(1) * tb
    r_cur = row0 + lax.broadcasted_iota(jnp.int32, (tb, 1), 0)
    r_halo = row0 - HALO + lax.broadcasted_iota(jnp.int32, (HALO, 1), 0)
    u = jnp.where(r_cur >= first_valid, u_ref[...], 0.0)
    ext_ref[0:HALO, :] = jnp.where(r_halo >= first_valid, halo_ref[...], 0.0)
    ext_ref[HALO:, :] = u
    seen = (r_cur - first_valid + 1).astype(F32)
    for g, win in enumerate(POOL_WINDOWS):
        c0, c1 = g * POOL_GROUP, (g + 1) * POOL_GROUP
        s = u[:, c0:c1]
        for back in range(1, win):
            s = s + ext_ref[HALO - back:HALO - back + tb, c0:c1]
        cnt = jnp.clip(seen, 1.0, float(win))
        diff = s / cnt - u[:, c0:c1]
        y = _dot(diff.astype(BF16), w_ref[g].astype(BF16))
        o_ref[:, c0:c1] = (y * sc_ref[:, c0:c1]).astype(o_ref.dtype)


def _pool(u2d, col_block, halo2d, halo_col_block, pool_w_l, pool_scale_l, out_buf, *, batch, rows, tb, row0,
          halo_map, first_valid):
    nb = rows // tb
    base = row0 // tb
    return pl.pallas_call(
        functools.partial(_pool_kernel, tb=tb, first_valid=first_valid),
        out_shape=jax.ShapeDtypeStruct(out_buf.shape, out_buf.dtype),
        grid=(batch, nb),
        in_specs=[
            pl.BlockSpec((tb, BRANCH_WIDTH), lambda b, i: (base + b * nb + i, col_block)),
            pl.BlockSpec((HALO, BRANCH_WIDTH), lambda b, i: (halo_map(b, i), halo_col_block)),
            pl.BlockSpec((len(POOL_WINDOWS), POOL_GROUP, POOL_GROUP), lambda b, i: (0, 0, 0)),
            pl.BlockSpec((1, BRANCH_WIDTH), lambda b, i: (0, 0)),
            pl.BlockSpec(memory_space=pl.ANY),
        ],
        out_specs=pl.BlockSpec((tb, BRANCH_WIDTH), lambda b, i: (base + b * nb + i, 0)),
        scratch_shapes=[pltpu.VMEM((HALO + tb, BRANCH_WIDTH), F32)],
        input_output_aliases={4: 0},
        compiler_params=_cparams(("parallel", "arbitrary")),
        name="pool",
    )(u2d, halo2d, pool_w_l, pool_scale_l.reshape(1, BRANCH_WIDTH), out_buf)


def _ret_kernel(q_ref, k_ref, v_ref, g_ref, cos_ref, sin_ref, dmat_ref, cross_ref, kdec_ref, gn_ref, s0_ref,
                _buf_ref, o_ref, s_out_ref, s_scr, *, first_valid):
    i = pl.program_id(1)

    @pl.when(i == 0)
    def _():
        s_scr[...] = s0_ref[...]

    rows = i * CHUNK + lax.broadcasted_iota(jnp.int32, (CHUNK, 1), 0)
    valid = rows >= first_valid
    cos = cos_ref[...]
    sin = sin_ref[...]
    rscale = RET_HEAD_DIM ** -0.5
    half = RET_HEAD_DIM // 2
    for h in range(RET_HEADS):
        sl = slice(h * RET_HEAD_DIM, (h + 1) * RET_HEAD_DIM)
        q = q_ref[:, sl]
        k = k_ref[:, sl]
        v = jnp.where(valid, v_ref[:, sl], 0.0).astype(BF16)
        qr = (q * cos + pltpu.roll(q, half, 1) * sin).astype(BF16)
        kr = jnp.where(valid, (k * cos + pltpu.roll(k, half, 1) * sin) * rscale, 0.0)
        state = s_scr[h]
        inner = _dot_nt(qr, kr.astype(BF16)) * dmat_ref[h]
        o = _dot(inner.astype(BF16), v) + _dot(qr, state.astype(BF16)) * cross_ref[h]
        s_scr[h] = gn_ref[h] * state + _dot_tn((kr * kdec_ref[h]).astype(BF16), v)
        mu = jnp.mean(o, axis=-1, keepdims=True)
        oc = o - mu
        var = jnp.mean(oc * oc, axis=-1, keepdims=True)
        gate = g_ref[:, sl]
        o_ref[:, sl] = (gate * jax.nn.sigmoid(gate) * (oc * lax.rsqrt(var + LN_EPS))).astype(o_ref.dtype)

    @pl.when(i == pl.num_programs(1) - 1)
    def _():
        s_out_ref[...] = s_scr[...]


def _retention(p3, s0, cos, sin, tabs, out_buf, *, batch, rows, row0, first_valid):
    nb = rows // CHUNK
    base = row0 // CHUNK
    dmat, cross, kdec, gn = tabs

    def col(c):
        return pl.BlockSpec((CHUNK, BRANCH_WIDTH), lambda b, i: (base + b * nb + i, c))

    tab_rows = pl.BlockSpec((CHUNK, RET_HEAD_DIM), lambda b, i: (i, 0))

    def full(a):
        return pl.BlockSpec(a.shape, lambda b, i: (0,) * a.ndim)

    state_spec = pl.BlockSpec((None, RET_HEADS, RET_HEAD_DIM, RET_HEAD_DIM), lambda b, i: (b, 0, 0, 0))
    return pl.pallas_call(
        functools.partial(_ret_kernel, first_valid=first_valid),
        out_shape=(jax.ShapeDtypeStruct(out_buf.shape, out_buf.dtype),
                   jax.ShapeDtypeStruct((batch, RET_HEADS, RET_HEAD_DIM, RET_HEAD_DIM), F32)),
        grid=(batch, nb),
        in_specs=[col(1), col(2), col(3), col(4), tab_rows, tab_rows, full(dmat), full(cross), full(kdec),
                  full(gn), state_spec, pl.BlockSpec(memory_space=pl.ANY)],
        out_specs=(pl.BlockSpec((CHUNK, BRANCH_WIDTH), lambda b, i: (base + b * nb + i, 0)), state_spec),
        scratch_shapes=[pltpu.VMEM((RET_HEADS, RET_HEAD_DIM, RET_HEAD_DIM), F32)],
        input_output_aliases={11: 0},
        compiler_params=_cparams(("parallel", "arbitrary")),
        name="retention",
    )(p3, p3, p3, p3, cos, sin, dmat, cross, kdec, gn, s0, out_buf)


def _sortable(x):
    bits = pltpu.bitcast(x + 0.0, jnp.int32)
    return bits ^ ((bits >> 31) & 0x7FFFFFFF)


def _split16(key):
    return (key >> 16).astype(jnp.int16), ((key & 0xFFFF) + HALF_MIN).astype(jnp.int16)


def _dsa_kernel(q_ref, qi_ref, w_ref, k_ref, vt_ref, ki_ref, bn_ref, ok_ref, _buf_ref, o_ref,
                key_ref, keyn_ref, hi_ref, lo_ref, hin_ref, lon_ref, m_ref, den_ref, acc_ref, *,
                first_real, qb0, skip_below, k_sel):
    j = pl.program_id(1)

    @pl.when(j < skip_below)
    def _():
        o_ref[...] = jnp.zeros(o_ref.shape, o_ref.dtype)

    @pl.when(j >= skip_below)
    def _():
        win0 = pl.multiple_of(j * QT + (qb0 * CHUNK - (WN - QT)), LANES)
        n_far = (win0 + KEY_TILE - 1) // KEY_TILE
        near = pl.ds(win0, WN)

        def index_keys(ki, adm):
            acc = None
            for h in range(0, IDX_HEADS, 2):
                pair = _dot(ki, jnp.concatenate([qi_ref[0, h], qi_ref[0, h + 1]], axis=1))
                term = (w_ref[0, h:h + 1, :] * jnp.maximum(pair[:, :QT], 0.0)
                        + w_ref[0, h + 1:h + 2, :] * jnp.maximum(pair[:, QT:], 0.0))
                acc = term if acc is None else acc + term
            return jnp.where(adm, _sortable(acc), INT_MIN)

        def far_keys(c, carry):
            c0 = pl.multiple_of(c * KEY_TILE, KEY_TILE)
            row = c0 + lax.broadcasted_iota(jnp.int32, (KEY_TILE, 1), 0)
            adm = jnp.logical_and(row >= first_real, row < win0)
            key = index_keys(ki_ref[pl.ds(c0, KEY_TILE), :], adm)
            key_ref[c] = key
            hi_ref[c], lo_ref[c] = _split16(key)
            return carry

        lax.fori_loop(0, n_far, far_keys, 0)
        row_n = win0 + lax.broadcasted_iota(jnp.int32, (WN, 1), 0)
        key_n = index_keys(ki_ref[near, :], jnp.logical_and(row_n >= first_real, ok_ref[...] != 0))
        keyn_ref[...] = key_n
        hin_ref[...], lon_ref[...] = _split16(key_n)

        def fold(hit):
            parts = hit.reshape(hit.shape[0] // PACKED_ROWS, PACKED_ROWS, QT)
            while parts.shape[0] > 1:
                half = parts.reshape(parts.shape[0] // 2, 2, PACKED_ROWS, QT)
                parts = half[:, 0] + half[:, 1]
            return parts[0]

        one, none = jnp.int16(1), jnp.int16(0)

        def count(far_ref, near_ref, c, strict=False):
            c16 = c.astype(jnp.int16)
            hit = (lambda a: jnp.where(a > c16, one, none)) if strict else (lambda a: jnp.where(a >= c16, one, none))
            part = lax.fori_loop(0, n_far, lambda t, p: p + fold(hit(far_ref[t])), fold(hit(near_ref[...])))
            return jnp.sum(part.astype(jnp.int32), axis=0, keepdims=True)

        def kth_largest16(far_ref, near_ref, target):
            zero = jnp.zeros((1, QT), jnp.int32)
            thr0 = jnp.where(count(far_ref, near_ref, zero) >= target, zero, HALF_MIN)

            def bit_step(it, thr):
                cand = thr + jnp.left_shift(jnp.int32(1), jnp.int32(14) - it)
                return jnp.where(count(far_ref, near_ref, cand) >= target, cand, thr)

            return lax.fori_loop(0, 15, bit_step, thr0)

        thr_hi = kth_largest16(hi_ref, hin_ref, jnp.full((1, QT), k_sel, jnp.int32))
        still_needed = k_sel - count(hi_ref, hin_ref, thr_hi, strict=True)
        thr_hi16 = thr_hi.astype(jnp.int16)

        def keep_bucket(t, carry):
            lo_ref[t] = jnp.where(hi_ref[t] == thr_hi16, lo_ref[t], jnp.int16(HALF_MIN))
            return carry

        lax.fori_loop(0, n_far, keep_bucket, 0)
        lon_ref[...] = jnp.where(hin_ref[...] == thr_hi16, lon_ref[...], jnp.int16(HALF_MIN))
        thr_lo = kth_largest16(lo_ref, lon_ref, still_needed)
        thr = thr_hi * 65536 + (thr_lo - HALF_MIN)
        thr = jnp.maximum(thr, INT_MIN + 1)

        m_ref[...] = jnp.full(m_ref.shape, M_INIT, F32)
        den_ref[...] = jnp.zeros(den_ref.shape, F32)
        acc_ref[...] = jnp.zeros(acc_ref.shape, F32)
        scale = A_HEAD_DIM ** -0.5

        def attend(h, logits, sel, vt_tiles):
            lg = jnp.where(sel, logits, NEG)
            m_old = m_ref[h]
            m_new = jnp.maximum(m_old, jnp.max(lg, axis=0, keepdims=True))
            p = jnp.exp(lg - m_new)
            alpha = jnp.exp(m_old - m_new)
            den_ref[h] = alpha * den_ref[h] + jnp.sum(p, axis=0, keepdims=True)
            pb = p.astype(BF16)
            pv = None
            for u, vt in enumerate(vt_tiles):
                term = _dot(vt, pb[u * LANES:(u + 1) * LANES])
                pv = term if pv is None else pv + term
            acc_ref[h] = alpha * acc_ref[h] + pv
            m_ref[h] = m_new

        def far_tile(c, carry):
            c0 = pl.multiple_of(c * KEY_TILE, KEY_TILE)
            sel = key_ref[c] >= thr
            for h in range(A_HEADS):
                sl = slice(h * A_HEAD_DIM, (h + 1) * A_HEAD_DIM)
                logits = _dot(k_ref[pl.ds(c0, KEY_TILE), sl], q_ref[0, h]) * scale
                attend(h, logits, sel,
                       [vt_ref[c * (KEY_TILE // LANES) + u, sl, :] for u in range(KEY_TILE // LANES)])
            return carry

        lax.fori_loop(0, n_far, far_tile, 0)
        sel_n = keyn_ref[...] >= thr
        wt = win0 // LANES
        for h in range(A_HEADS):
            sl = slice(h * A_HEAD_DIM, (h + 1) * A_HEAD_DIM)
            logits = _dot(k_ref[near, sl], q_ref[0, h]) * scale + bn_ref[h]
            attend(h, logits, sel_n, [vt_ref[wt + u, sl, :] for u in range(WN // LANES)])
            out_t = acc_ref[h] / jnp.maximum(den_ref[h], TINY)
            o_ref[:, sl] = out_t.T[:o_ref.shape[0]].astype(o_ref.dtype)


def _dsa(q_t, qi_t, w_t, k16, v_t, ki16, bias_near, near_ok, out_buf, *, batch, nb, n_keys, out_rows, out_base,
         first_real, qb0, skip_below, k_sel):
    width = k16.shape[1]
    return pl.pallas_call(
        functools.partial(_dsa_kernel, first_real=first_real, qb0=qb0, skip_below=skip_below, k_sel=k_sel),
        out_shape=jax.ShapeDtypeStruct(out_buf.shape, out_buf.dtype),
        grid=(batch, nb),
        in_specs=[
            pl.BlockSpec((1, A_HEADS, A_HEAD_DIM, QT), lambda b, j: (b * nb + j, 0, 0, 0)),
            pl.BlockSpec((1, IDX_HEADS, IDX_DIM, QT), lambda b, j: (b * nb + j, 0, 0, 0)),
            pl.BlockSpec((1, IDX_HEADS, QT), lambda b, j: (b * nb + j, 0, 0)),
            pl.BlockSpec((n_keys, width), lambda b, j: (b, 0)),
            pl.BlockSpec((n_keys // LANES, width, LANES), lambda b, j: (b, 0, 0)),
            pl.BlockSpec((n_keys, IDX_DIM), lambda b, j: (b, 0)),
            pl.BlockSpec((A_HEADS, WN, QT), lambda b, j: (0, 0, 0)),
            pl.BlockSpec((WN, QT), lambda b, j: (0, 0)),
            pl.BlockSpec(memory_space=pl.ANY),
        ],
        out_specs=pl.BlockSpec((out_rows, width), lambda b, j: (out_base + b * nb + j, 0)),
        scratch_shapes=[pltpu.VMEM((n_keys // KEY_TILE, KEY_TILE, QT), jnp.int32),
                        pltpu.VMEM((WN, QT), jnp.int32),
                        pltpu.VMEM((n_keys // KEY_TILE, KEY_TILE, QT), jnp.int16),
                        pltpu.VMEM((n_keys // KEY_TILE, KEY_TILE, QT), jnp.int16),
                        pltpu.VMEM((WN, QT), jnp.int16),
                        pltpu.VMEM((WN, QT), jnp.int16),
                        pltpu.VMEM((A_HEADS, 1, QT), F32),
                        pltpu.VMEM((A_HEADS, 1, QT), F32),
                        pltpu.VMEM((A_HEADS, A_HEAD_DIM, QT), F32)],
        input_output_aliases={8: 0},
        compiler_params=_cparams(("parallel", "arbitrary")),
        name="dsa",
    )(q_t, qi_t, w_t, k16, v_t, ki16, bias_near, near_ok, out_buf)


def _sample_keys_kernel(ck_ref, cv_ref, cki_ref, nk_ref, nv_ref, nki_ref, k_ref, vt_ref, ki_ref):
    c = pl.program_id(1)
    n_cache = pl.num_programs(1) - 1
    rows = k_ref.shape[0]

    @pl.when(c < n_cache)
    def _():
        for h in range(A_HEADS):
            sl = slice(h * A_HEAD_DIM, (h + 1) * A_HEAD_DIM)
            head_rows = pl.ds(h, rows, stride=A_HEADS)
            k_ref[:, sl] = ck_ref[0, head_rows, :].astype(BF16)
            vh = cv_ref[0, head_rows, :]
            for u in range(rows // LANES):
                vt_ref[u, sl, :] = vh[u * LANES:(u + 1) * LANES, :].T.astype(BF16)
        ki_ref[...] = cki_ref[0].astype(BF16)

    @pl.when(c == n_cache)
    def _():
        k_ref[...] = jnp.zeros(k_ref.shape, BF16)
        vt_ref[...] = jnp.zeros(vt_ref.shape, BF16)
        ki_ref[...] = jnp.zeros(ki_ref.shape, BF16)
        k_ref[0:CHUNK, :] = nk_ref[...].astype(BF16)
        ki_ref[0:CHUNK, :] = nki_ref[:, :IDX_DIM].astype(BF16)
        pad = jnp.zeros((LANES - CHUNK, A_HEAD_DIM), F32)
        for h in range(A_HEADS):
            sl = slice(h * A_HEAD_DIM, (h + 1) * A_HEAD_DIM)
            vt_ref[0, sl, :] = jnp.concatenate([nv_ref[:, sl], pad], axis=0).T.astype(BF16)


def _sample_keys(l, cache_k, cache_v, cache_ki, k_new, v_new, ki_new, *, new_row0, n_keys, rows=512):
    _, bs, past = cache_k.shape[:3]
    width = k_new.shape[1]
    n_cache = past // rows
    nb = -(-n_keys // rows)
    new_blk = new_row0 // CHUNK

    def cached(block_rows, w):
        return pl.BlockSpec((None, 1, block_rows, w), lambda b, c: (l, b, jnp.minimum(c, n_cache - 1), 0))

    def new(w):
        return pl.BlockSpec((CHUNK, w), lambda b, c: (new_blk + b, 0))

    kv_shape = cache_k.shape[:2] + (past * A_HEADS, A_HEAD_DIM)

    k16, v_t, ki16 = pl.pallas_call(
        _sample_keys_kernel,
        out_shape=(jax.ShapeDtypeStruct((bs, n_keys, width), BF16),
                   jax.ShapeDtypeStruct((bs, n_keys // LANES, width, LANES), BF16),
                   jax.ShapeDtypeStruct((bs, n_keys, IDX_DIM), BF16)),
        grid=(bs, nb),
        in_specs=[cached(rows * A_HEADS, A_HEAD_DIM), cached(rows * A_HEADS, A_HEAD_DIM), cached(rows, IDX_DIM),
                  new(width), new(width), new(LANES)],
        out_specs=(pl.BlockSpec((None, rows, width), lambda b, c: (b, c, 0)),
                   pl.BlockSpec((None, rows // LANES, width, LANES), lambda b, c: (b, c, 0, 0)),
                   pl.BlockSpec((None, rows, IDX_DIM), lambda b, c: (b, c, 0))),
        compiler_params=_cparams(("parallel", "arbitrary")),
        name="sample_keys",
    )(cache_k.reshape(kv_shape), cache_v.reshape(kv_shape), cache_ki, k_new, v_new, ki_new)
    return (k16.reshape(bs * n_keys, width), v_t.reshape(bs * (n_keys // LANES), width, LANES),
            ki16.reshape(bs * n_keys, IDX_DIM))


def _t5_bucket(rel):
    half = T5_BUCKETS // 2
    exact = half // 2
    n = jnp.abs(rel)
    large = exact + (jnp.log(jnp.maximum(n, 1).astype(F32) / exact)
                     / math.log(T5_MAX_DIST / exact) * (half - exact)).astype(jnp.int32)
    large = jnp.minimum(large, half - 1)
    return jnp.where(rel > 0, half, 0) + jnp.where(n < exact, n, large)


def _bias_tables(t5_bias):
    a = jnp.arange(WN, dtype=jnp.int32)[:, None]
    t = jnp.arange(QT, dtype=jnp.int32)[None, :]
    bucket = _t5_bucket(a - (WN - QT) - t)
    onehot = (bucket[:, :, None] == jnp.arange(T5_BUCKETS, dtype=jnp.int32)).astype(F32)
    near = jnp.einsum("atk,kh->hat", onehot, t5_bias.astype(F32), precision=lax.Precision.HIGHEST)
    far = t5_bias[_t5_bucket(jnp.int32(-2 * CHUNK - 1))].astype(F32)
    ok = (a // CHUNK - (WN - QT) // CHUNK <= t // CHUNK).astype(jnp.int32)
    return near - far[:, None, None], ok


def _rope_tables(pos):
    half = RET_HEAD_DIM // 2
    inv = ROPE_BASE ** (-jnp.arange(half, dtype=F32) / half)
    ang = pos.astype(F32)[:, None] * inv[None, :]
    cos, sin = jnp.cos(ang), jnp.sin(ang)
    return jnp.concatenate([cos, cos], axis=-1), jnp.concatenate([-sin, sin], axis=-1)


def _decay_tables():
    n = CHUNK
    log_g = jnp.log(1.0 - 2.0 ** (-5.0 - jnp.arange(RET_HEADS, dtype=F32)))
    i = jnp.arange(n, dtype=F32)
    diff = i[:, None] - i[None, :]
    dmat = jnp.where(diff >= 0, jnp.exp(jnp.maximum(diff, 0.0)[None] * log_g[:, None, None]), 0.0)
    cross = jnp.exp((i[None, :] + 1.0) * log_g[:, None])
    kdec = jnp.exp((n - 1.0 - i)[None, :] * log_g[:, None])
    gn = jnp.exp(n * log_g)
    wide = (RET_HEADS, n, RET_HEAD_DIM)
    return (dmat, jnp.broadcast_to(cross[:, :, None], wide), jnp.broadcast_to(kdec[:, :, None], wide),
            jnp.broadcast_to(gn[:, None, None], (RET_HEADS, 1, RET_HEAD_DIM)))


def kernel(x_prompt, x_sample, cache_k, cache_v, cache_ki, cache_pool, state_ret, meta_tokens, ln_in_g, ln_in_b, w_in, t5_bias, pool_w, pool_scale, w_branch, w_gate, b_gate, w_out, ln1_g, ln1_b, ln2_g, ln2_b, ffn_w_gate, ffn_w_up, ffn_w_down, moe_w_router, moe_b_router, moe_w_gate, moe_w_up, moe_w_down):
    bp, seq, d = x_prompt.shape
    bs, ts, _ = x_sample.shape
    past = cache_k.shape[2]
    t_real = seq + N_META
    tp = -(-(t_real + 2 * CHUNK) // KEY_TILE) * KEY_TILE
    front = tp - t_real
    assert front % CHUNK == CHUNK - N_META and ts == CHUNK and past % CHUNK == 0
    mp, ms = bp * tp, bs * ts
    m = mp + ms
    assert mp % TM == 0 and ms % TM == 0
    ksel_p = min(TOPK_MAX, seq // 4)
    ksel_s = min(TOPK_MAX, (past + ts) // 4)
    n_keys_s = -(-(past + QT) // KEY_TILE) * KEY_TILE
    assert tp % QT == 0 and front >= WN - QT and past % QT == 0 and past + QT >= WN

    meta = jnp.broadcast_to(meta_tokens.astype(F32)[None], (bp, N_META, d))
    x_rows = jnp.concatenate(
        [jnp.concatenate([jnp.zeros((bp, front, d), F32), meta, x_prompt], axis=1).reshape(mp, d),
         x_sample.reshape(ms, d)], axis=0)
    x32, xb = _layer_norm(x_rows, None, ln_in_g, ln_in_b)

    bias_near, near_ok = _bias_tables(t5_bias)
    cos_p, sin_p = _rope_tables(jnp.arange(tp, dtype=jnp.int32) - (front + N_META))
    cos_s, sin_s = _rope_tables(past + jnp.arange(ts, dtype=jnp.int32))
    decay = _decay_tables()
    zero_state = jnp.zeros((bp, RET_HEADS, RET_HEAD_DIM, RET_HEAD_DIM), F32)
    w_in_t = jnp.swapaxes(w_in, 1, 2)
    w_tail_t = w_in_t[:, TAIL_OFF:, :]
    row_in_seq = jnp.arange(mp, dtype=jnp.int32) % tp
    token_rows = jnp.concatenate([row_in_seq >= front, jnp.ones((ms,), jnp.bool_)])

    outs = {name: [] for name in ("kp", "vp", "kip", "poolp", "retp", "ks", "vs", "kis", "pools", "rets")}
    bw = BRANCH_WIDTH
    for l in range(DEPTH):
        q16, q_t = _proj(xb, w_in_t, (l,), 0, nat16=True, trans=True, name="proj_q")
        k_new, k16 = _proj(xb, w_in_t, (l,), 1, nat32=True, nat16=True, name="proj_k")
        v_new, v_t = _proj(xb, w_in_t, (l,), 2, nat32=True, trans=True, name="proj_v")
        qi16, qi_t = _proj(xb, w_in_t, (l,), 3, nat16=True, trans=True, name="proj_qi")
        p2, ki16, w_t = _proj_idx(xb, w_in_t, (l,), 4 * bw // LANES)
        p3 = _matmul(xb, w_tail_t, (l,), 0, 5, bw, name="proj_tail", w_transposed=True)
        ki_new = p2[:, :IDX_DIM]
        qi_t = qi_t.reshape(m // LANES, IDX_HEADS, IDX_DIM, LANES)
        v_t = v_t.reshape(m // LANES, bw, LANES)

        def sample_queries_t(a16, heads):
            a = jnp.pad(a16[mp:].reshape(bs, ts, heads, bw // heads), ((0, 0), (0, QT - ts), (0, 0), (0, 0)))
            return jnp.transpose(a, (0, 2, 3, 1))

        w_t_s = jnp.transpose(jnp.pad(p2[mp:, IDX_DIM:IDX_DIM + IDX_HEADS].reshape(bs, ts, IDX_HEADS),
                                      ((0, 0), (0, QT - ts), (0, 0))), (0, 2, 1))
        k16_s, v_t_s, ki16_s = _sample_keys(l, cache_k, cache_v, cache_ki, k_new, v_new, p2, new_row0=mp,
                                            n_keys=n_keys_s)

        branch_buf = jnp.zeros((m, bw), BF16)
        oa = _dsa(q_t, qi_t, w_t, k16, v_t, ki16, bias_near, near_ok, branch_buf, batch=bp, nb=tp // QT, n_keys=tp,
                  out_rows=QT, out_base=0, first_real=front, qb0=0, skip_below=max(1, front // QT), k_sel=ksel_p)
        oa = _dsa(sample_queries_t(q16, A_HEADS), sample_queries_t(qi16, IDX_HEADS), w_t_s, k16_s, v_t_s, ki16_s,
                  bias_near, near_ok, oa, batch=bs, nb=1, n_keys=n_keys_s, out_rows=ts, out_base=mp // ts,
                  first_real=0, qb0=past // CHUNK, skip_below=0, k_sel=ksel_s)

        tb_p = 256
        ob = _pool(p3, 0, p3, 0, pool_w[l], pool_scale[l], branch_buf, batch=bp, rows=tp, tb=tb_p, row0=0,
                   halo_map=lambda b, i: jnp.maximum((b * tp + i * tb_p) // HALO - 1, 0), first_valid=front)
        pool_hist = jnp.pad(cache_pool[l], ((0, 0), (HALO - POOL_PAST, 0), (0, 0))).reshape(bs * HALO, bw)
        ob = _pool(p3, 0, pool_hist, 0, pool_w[l], pool_scale[l], ob, batch=bs, rows=ts, tb=ts, row0=mp,
                   halo_map=lambda b, i: b, first_valid=-POOL_PAST)

        oc, ret_p = _retention(p3, zero_state, cos_p, sin_p, decay, branch_buf, batch=bp, rows=tp, row0=0,
                               first_valid=front)
        oc, ret_s = _retention(p3, state_ret[l].astype(F32), cos_s, sin_s, decay, oc, batch=bs, rows=ts, row0=mp,
                               first_valid=0)

        merged = _merge(l, xb, oa, ob, oc, w_branch, w_gate, b_gate)
        y = _matmul(merged, w_out, (l,), 0, d // 512, 512, name="w_out")
        x32, xb = _layer_norm(x32, y, ln1_g[l], ln1_b[l])

        if l % 2 == 0:
            h = _gate_up(xb, ffn_w_gate, ffn_w_up, (l // 2,))
            f = _matmul(h, ffn_w_down, (l // 2,), 0, d // 512, 512, name="ffn_down")
        else:
            gates, experts = _router(x32, moe_w_router[l // 2], moe_b_router[l // 2])
            src, pos, tile_expert, n_used = _dispatch_plan(experts, MOE_TILE, token_rows)
            x_sorted = _gather_rows(x32, src, n_used, MOE_TILE)
            y_sorted = _moe_experts(x_sorted, tile_expert, n_used, moe_w_gate, moe_w_up, moe_w_down, l // 2, MOE_TILE)
            f = _combine_rows(y_sorted, pos, gates, MOE_TILE)
        x32, xb = _layer_norm(x32, f, ln2_g[l], ln2_b[l])

        def prompt_rows(a, width):
            return a[:mp].reshape(bp, tp, width)[:, front:]

        u = p3[:, :bw]
        outs["kp"].append(prompt_rows(k_new, bw).reshape(bp, t_real, A_HEADS, A_HEAD_DIM))
        outs["vp"].append(prompt_rows(v_new, bw).reshape(bp, t_real, A_HEADS, A_HEAD_DIM))
        outs["kip"].append(prompt_rows(ki_new, IDX_DIM))
        outs["poolp"].append(prompt_rows(u, bw)[:, -POOL_PAST:])
        outs["retp"].append(ret_p)
        outs["ks"].append(k_new[mp:].reshape(bs, ts, A_HEADS, A_HEAD_DIM))
        outs["vs"].append(v_new[mp:].reshape(bs, ts, A_HEADS, A_HEAD_DIM))
        outs["kis"].append(ki_new[mp:].reshape(bs, ts, IDX_DIM))
        outs["pools"].append(u[mp:].reshape(bs, ts, bw)[:, -POOL_PAST:])
        outs["rets"].append(ret_s)

    y_prompt = x32[:mp].reshape(bp, tp, d)[:, front + N_META:]
    y_sample = x32[mp:].reshape(bs, ts, d)
    return (y_prompt, y_sample) + tuple(
        jnp.stack(outs[name]) for name in ("kp", "vp", "kip", "poolp", "retp", "ks", "vs", "kis", "pools", "rets"))
```

```python
import functools
import math

import jax
import jax.numpy as jnp
from jax import lax
from jax.experimental import pallas as pl
from jax.experimental.pallas import tpu as pltpu

F32 = jnp.float32
BF16 = jnp.bfloat16

D_MODEL = 2048
DEPTH = 2
CHUNK = 64
N_META = 16
BRANCH_WIDTH = D_MODEL // 2
A_HEADS = 8
A_HEAD_DIM = BRANCH_WIDTH // A_HEADS
IDX_HEADS = 16
IDX_DIM = 64
TOPK_MAX = 256
T5_BUCKETS = 32
T5_MAX_DIST = 128
POOL_WINDOWS = (2, 4, 8, 16)
POOL_GROUP = BRANCH_WIDTH // 4
POOL_PAST = 15
RET_HEADS = 8
RET_HEAD_DIM = BRANCH_WIDTH // RET_HEADS
ROPE_BASE = 10000.0
N_BRANCH = 3
D_FF = 11 * D_MODEL // 4
N_EXPERTS = 8
TOP_K = 2
ALPHA = (2 * DEPTH) ** 0.25
LN_EPS = 1e-5
IN_SPLITS = (BRANCH_WIDTH, BRANCH_WIDTH, BRANCH_WIDTH, IDX_HEADS * IDX_DIM, IDX_DIM, IDX_HEADS,
             BRANCH_WIDTH, BRANCH_WIDTH, BRANCH_WIDTH, BRANCH_WIDTH, BRANCH_WIDTH)
IN_WIDTH = sum(IN_SPLITS)
TAIL_OFF = 4 * BRANCH_WIDTH + IDX_DIM + IDX_HEADS

LANES = 128
SUBLANES = 8
HALO = 16
QT = 2 * CHUNK
WN = QT + 2 * CHUNK
KEY_TILE = 256
VMEM_LIMIT = 56 * 1024 * 1024
TM = 512
MOE_TILE = 512
LN_TILE = 256
INT_MIN = -2 ** 31
NEG = -1e30
M_INIT = -1e20
TINY = 1e-30
ATTN_LOG2_SCALE = A_HEAD_DIM ** -0.5 * math.log2(math.e)


def _cparams(sem):
    return pltpu.CompilerParams(dimension_semantics=sem, vmem_limit_bytes=VMEM_LIMIT)


def _dot(a, b):
    return jnp.dot(a, b, preferred_element_type=F32)


def _dot_nt(a, b):
    return lax.dot_general(a, b, (((1,), (1,)), ((), ())), preferred_element_type=F32)


def _dot_tn(a, b):
    return lax.dot_general(a, b, (((0,), (0,)), ((), ())), preferred_element_type=F32)


def _ln(x, g_ref, b_ref):
    mu = jnp.mean(x, axis=-1, keepdims=True)
    xc = x - mu
    var = jnp.mean(xc * xc, axis=-1, keepdims=True)
    return xc * lax.rsqrt(var + LN_EPS) * g_ref[...] + b_ref[...]


def _ln_res_kernel(x_ref, y_ref, g_ref, b_ref, o32_ref, o16_ref):
    y = _ln(ALPHA * x_ref[...] + y_ref[...], g_ref, b_ref)
    o32_ref[...] = y
    o16_ref[...] = y.astype(BF16)


def _layer_norm(x, y, g, b, tm=LN_TILE):
    m, d = x.shape
    row = pl.BlockSpec((tm, d), lambda i: (i, 0))
    vec = pl.BlockSpec((1, d), lambda i: (0, 0))
    return pl.pallas_call(
        _ln_res_kernel,
        out_shape=(jax.ShapeDtypeStruct((m, d), F32), jax.ShapeDtypeStruct((m, d), BF16)),
        grid=(m // tm,),
        in_specs=[row, row, vec, vec],
        out_specs=(row, row),
        compiler_params=_cparams(("parallel",)),
        name="layer_norm",
    )(x, y, g.reshape(1, d), b.reshape(1, d))


def _seq_tile_maps(bp, seq, tm):
    tiles_per_seq = seq // tm + 1
    n_prompt_tiles = bp * tiles_per_seq

    def prompt_map(i):
        in_prompt = i < n_prompt_tiles
        b = jnp.minimum(i // tiles_per_seq, bp - 1)
        r = jnp.where(in_prompt, jnp.maximum(i % tiles_per_seq - 1, 0), seq // tm - 1)
        return (b, r, 0)

    def sample_map(i):
        return (jnp.maximum(i - n_prompt_tiles, 0), 0)

    return tiles_per_seq, n_prompt_tiles, prompt_map, sample_map


def _ln_in_kernel(head_ref, xp_ref, xs_ref, g_ref, b_ref, o32_ref, o16_ref, *, tiles_per_seq, n_prompt_tiles):
    i = pl.program_id(0)
    x = jnp.where(i >= n_prompt_tiles, xs_ref[...],
                  jnp.where(i % tiles_per_seq == 0, head_ref[...], xp_ref[...]))
    y = _ln(x, g_ref, b_ref)
    o32_ref[...] = y
    o16_ref[...] = y.astype(BF16)


def _input_norm(head, x_prompt, x_sample, g, b, tm=LN_TILE):
    bp, seq, d = x_prompt.shape
    ms = x_sample.shape[0]
    tiles_per_seq, n_prompt_tiles, prompt_map, sample_map = _seq_tile_maps(bp, seq, tm)
    m = n_prompt_tiles * tm + ms
    row = pl.BlockSpec((tm, d), lambda i: (i, 0))
    vec = pl.BlockSpec((1, d), lambda i: (0, 0))
    return pl.pallas_call(
        functools.partial(_ln_in_kernel, tiles_per_seq=tiles_per_seq, n_prompt_tiles=n_prompt_tiles),
        out_shape=(jax.ShapeDtypeStruct((m, d), F32), jax.ShapeDtypeStruct((m, d), BF16)),
        grid=(m // tm,),
        in_specs=[pl.BlockSpec((tm, d), lambda i: (0, 0)), pl.BlockSpec((None, tm, d), prompt_map),
                  pl.BlockSpec((tm, d), sample_map), vec, vec],
        out_specs=(row, row),
        compiler_params=_cparams(("parallel",)),
        name="input_norm",
    )(head, x_prompt, x_sample, g.reshape(1, d), b.reshape(1, d))


def _ln_out_kernel(x_ref, y_ref, g_ref, b_ref, yp_ref, ys_ref, *, tiles_per_seq, n_prompt_tiles):
    i = pl.program_id(0)
    out = _ln(ALPHA * x_ref[...] + y_ref[...], g_ref, b_ref)

    @pl.when(jnp.logical_and(i < n_prompt_tiles, i % tiles_per_seq != 0))
    def _():
        yp_ref[...] = out

    @pl.when(i >= n_prompt_tiles)
    def _():
        ys_ref[...] = out


def _output_norm(x, y, g, b, bp, seq, tm=LN_TILE):
    m, d = x.shape
    tiles_per_seq, n_prompt_tiles, prompt_map, sample_map = _seq_tile_maps(bp, seq, tm)
    ms = m - n_prompt_tiles * tm
    row = pl.BlockSpec((tm, d), lambda i: (i, 0))
    vec = pl.BlockSpec((1, d), lambda i: (0, 0))
    return pl.pallas_call(
        functools.partial(_ln_out_kernel, tiles_per_seq=tiles_per_seq, n_prompt_tiles=n_prompt_tiles),
        out_shape=(jax.ShapeDtypeStruct((bp, seq, d), F32), jax.ShapeDtypeStruct((ms, d), F32)),
        grid=(m // tm,),
        in_specs=[row, row, vec, vec],
        out_specs=(pl.BlockSpec((None, tm, d), prompt_map), pl.BlockSpec((tm, d), sample_map)),
        compiler_params=_cparams(("arbitrary",)),
        name="output_norm",
    )(x, y, g.reshape(1, d), b.reshape(1, d))


def _mm_kernel(x_ref, w_ref, o_ref, wb_ref, *, w_rows_are_outputs):
    @pl.when(pl.program_id(1) == 0)
    def _():
        if w_rows_are_outputs:
            wb_ref[...] = w_ref[...].reshape(w_ref.shape[-2:]).T.astype(BF16)
        else:
            wb_ref[...] = w_ref[...].astype(BF16)

    o_ref[...] = _dot(x_ref[...], wb_ref[...]).astype(o_ref.dtype)


def _matmul(x, w, lead, col0, n_tiles, tn, name="matmul", w_transposed=False):
    m, k = x.shape
    nl = len(lead)
    if w_transposed:
        w_spec = pl.BlockSpec((pl.Element(1),) * nl + (pl.Element(tn), pl.Element(k)),
                              lambda j, i: tuple(lead) + (pl.multiple_of(j * tn + col0, SUBLANES), 0))
    else:
        w_spec = pl.BlockSpec((None,) * nl + (k, tn), lambda j, i: tuple(lead) + (0, j + col0))
    x_spec = pl.BlockSpec((TM, k), lambda j, i: (i, 0))
    o_spec = pl.BlockSpec((TM, tn), lambda j, i: (i, j))
    return pl.pallas_call(
        functools.partial(_mm_kernel, w_rows_are_outputs=w_transposed),
        out_shape=jax.ShapeDtypeStruct((m, n_tiles * tn), F32),
        grid=(n_tiles, m // TM),
        in_specs=[x_spec, w_spec],
        out_specs=o_spec,
        scratch_shapes=[pltpu.VMEM((k, tn), BF16)],
        compiler_params=_cparams(("arbitrary", "arbitrary")),
        name=name,
    )(x, w)


def _matmul_norm_kernel(a_ref, w_ref, x_ref, g_ref, b_ref, o32_ref, o16_ref, wb_ref):
    @pl.when(pl.program_id(0) == 0)
    def _():
        wb_ref[...] = w_ref[...].astype(BF16)

    y = _ln(ALPHA * x_ref[...] + _dot(a_ref[...], wb_ref[...]), g_ref, b_ref)
    o32_ref[...] = y
    o16_ref[...] = y.astype(BF16)


def _matmul_norm(a, w, lead, x, g, b, tm=LN_TILE):
    m, k = a.shape
    d = x.shape[1]
    nl = len(lead)
    row = pl.BlockSpec((tm, d), lambda i: (i, 0))
    vec = pl.BlockSpec((1, d), lambda i: (0, 0))
    return pl.pallas_call(
        _matmul_norm_kernel,
        out_shape=(jax.ShapeDtypeStruct((m, d), F32), jax.ShapeDtypeStruct((m, d), BF16)),
        grid=(m // tm,),
        in_specs=[pl.BlockSpec((tm, k), lambda i: (i, 0)),
                  pl.BlockSpec((None,) * nl + (k, d), lambda i: tuple(lead) + (0, 0), pipeline_mode=pl.Buffered(1)),
                  row, vec, vec],
        out_specs=(row, row),
        scratch_shapes=[pltpu.VMEM((k, d), BF16)],
        compiler_params=_cparams(("arbitrary",)),
        name="matmul_norm",
    )(a, w, x, g.reshape(1, d), b.reshape(1, d))


def _store_lane_tiles_t(res, out_ref):
    rows, width = res.shape
    for r in range(rows // LANES):
        for c in range(width // LANES):
            tile = res[r * LANES:(r + 1) * LANES, c * LANES:(c + 1) * LANES]
            out_ref[r, c] = tile.T.astype(out_ref.dtype)


def _proj_kernel(x_ref, w_ref, *refs, nat32, nat16, trans, out_scale):
    outs, wb_ref = list(refs[:-1]), refs[-1]

    @pl.when(pl.program_id(0) == 0)
    def _():
        wb_ref[...] = w_ref[...].T.astype(BF16)

    res = _dot(x_ref[...], wb_ref[...])
    if out_scale is not None:
        res = res * out_scale
    if nat32:
        outs.pop(0)[...] = res
    if nat16:
        outs.pop(0)[...] = res.astype(BF16)
    if trans:
        _store_lane_tiles_t(res, outs.pop(0))


def _proj(x, w, lead, col0, *, nat32=False, nat16=False, trans=False, out_scale=None, tm=TM, name="proj"):
    m, k = x.shape
    bw = BRANCH_WIDTH
    nl = len(lead)
    shapes, specs = [], []
    row_spec = pl.BlockSpec((tm, bw), lambda i: (i, 0))
    if nat32:
        shapes.append(jax.ShapeDtypeStruct((m, bw), F32)); specs.append(row_spec)
    if nat16:
        shapes.append(jax.ShapeDtypeStruct((m, bw), BF16)); specs.append(row_spec)
    if trans:
        shapes.append(jax.ShapeDtypeStruct((m // LANES, bw // LANES, LANES, LANES), BF16))
        specs.append(pl.BlockSpec((tm // LANES, bw // LANES, LANES, LANES), lambda i: (i, 0, 0, 0)))
    return pl.pallas_call(
        functools.partial(_proj_kernel, nat32=nat32, nat16=nat16, trans=trans, out_scale=out_scale),
        out_shape=tuple(shapes),
        grid=(m // tm,),
        in_specs=[pl.BlockSpec((tm, k), lambda i: (i, 0)),
                  pl.BlockSpec((None,) * nl + (bw, k), lambda i: tuple(lead) + (col0, 0))],
        out_specs=tuple(specs),
        scratch_shapes=[pltpu.VMEM((k, bw), BF16)],
        compiler_params=_cparams(("arbitrary",)),
        name=name,
    )(x, w)


def _proj_idx_kernel(x_ref, w_ref, nat_ref, ki_ref, wt_ref, wb_ref):
    @pl.when(pl.program_id(0) == 0)
    def _():
        wb_ref[...] = w_ref[...].T.astype(BF16)

    res = _dot(x_ref[...], wb_ref[...])
    nat_ref[...] = res
    ki_ref[...] = res[:, :IDX_DIM].astype(BF16)
    for r in range(res.shape[0] // LANES):
        wt_ref[r] = res[r * LANES:(r + 1) * LANES, :].T[IDX_DIM:IDX_DIM + IDX_HEADS, :]


def _proj_idx(x, w, lead, col_block, tm=TM):
    m, k = x.shape
    nl = len(lead)
    return pl.pallas_call(
        _proj_idx_kernel,
        out_shape=(jax.ShapeDtypeStruct((m, LANES), F32), jax.ShapeDtypeStruct((m, IDX_DIM), BF16),
                   jax.ShapeDtypeStruct((m // LANES, IDX_HEADS, LANES), F32)),
        grid=(m // tm,),
        in_specs=[pl.BlockSpec((tm, k), lambda i: (i, 0)),
                  pl.BlockSpec((None,) * nl + (LANES, k), lambda i: tuple(lead) + (col_block, 0))],
        out_specs=(pl.BlockSpec((tm, LANES), lambda i: (i, 0)), pl.BlockSpec((tm, IDX_DIM), lambda i: (i, 0)),
                   pl.BlockSpec((tm // LANES, IDX_HEADS, LANES), lambda i: (i, 0, 0))),
        scratch_shapes=[pltpu.VMEM((k, LANES), BF16)],
        compiler_params=_cparams(("arbitrary",)),
        name="proj_idx",
    )(x, w)


def _gate_up_kernel(x_ref, wg_ref, wu_ref, o_ref, wgb_ref, wub_ref):
    @pl.when(pl.program_id(1) == 0)
    def _():
        wgb_ref[...] = wg_ref[...].astype(BF16)
        wub_ref[...] = wu_ref[...].astype(BF16)

    x = x_ref[...]
    g = _dot(x, wgb_ref[...])
    u = _dot(x, wub_ref[...])
    o_ref[...] = (g * jax.nn.sigmoid(g) * u).astype(o_ref.dtype)


def _gate_up(x, wg, wu, lead, tn=512):
    m, k = x.shape
    n = wg.shape[-1]
    nl = len(lead)
    w_spec = pl.BlockSpec((None,) * nl + (k, tn), lambda j, i: tuple(lead) + (0, j))
    x_spec = pl.BlockSpec((TM, k), lambda j, i: (i, 0))
    return pl.pallas_call(
        _gate_up_kernel,
        out_shape=jax.ShapeDtypeStruct((m, n), BF16),
        grid=(n // tn, m // TM),
        in_specs=[x_spec, w_spec, w_spec],
        out_specs=pl.BlockSpec((TM, tn), lambda j, i: (i, j)),
        scratch_shapes=[pltpu.VMEM((k, tn), BF16), pltpu.VMEM((k, tn), BF16)],
        compiler_params=_cparams(("arbitrary", "arbitrary")),
        name="gate_up",
    )(x, wg, wu)


def _merge_kernel(x_ref, oa_ref, ob_ref, oc_ref, wg0_ref, wg1_ref, wg2_ref, wb0_ref, wb1_ref, wb2_ref,
                  bg0_ref, bg1_ref, bg2_ref, o_ref, wgb_ref, wbb_ref):
    wg_refs = (wg0_ref, wg1_ref, wg2_ref)
    wb_refs = (wb0_ref, wb1_ref, wb2_ref)

    @pl.when(pl.program_id(1) == 0)
    def _():
        for n in range(N_BRANCH):
            wgb_ref[n] = wg_refs[n][...].astype(BF16)
            wbb_ref[n] = wb_refs[n][...].astype(BF16)

    x = x_ref[...]
    acc = None
    for n, (o_in, bg) in enumerate(zip((oa_ref, ob_ref, oc_ref), (bg0_ref, bg1_ref, bg2_ref))):
        gate = jax.nn.sigmoid(_dot(x, wgb_ref[n]) + bg[...])
        term = gate * _dot(o_in[...], wbb_ref[n])
        acc = term if acc is None else acc + term
    o_ref[...] = acc.astype(o_ref.dtype)


def _merge(l, x, oa, ob, oc, w_branch, w_gate, b_gate, tn=256):
    m, d = x.shape
    w = oa.shape[1]
    nt = d // tn
    x_spec = pl.BlockSpec((TM, d), lambda j, i: (i, 0))
    o_in_spec = pl.BlockSpec((TM, w), lambda j, i: (i, 0))
    wg_specs = [pl.BlockSpec((None, d, tn), lambda j, i, n=n: (l, 0, n * nt + j)) for n in range(N_BRANCH)]
    wb_specs = [pl.BlockSpec((None, None, w, tn), lambda j, i, n=n: (l, n, 0, j)) for n in range(N_BRANCH)]
    bg_specs = [pl.BlockSpec((None, 1, tn), lambda j, i, n=n: (l, 0, n * nt + j)) for n in range(N_BRANCH)]
    return pl.pallas_call(
        _merge_kernel,
        out_shape=jax.ShapeDtypeStruct((m, d), BF16),
        grid=(nt, m // TM),
        in_specs=[x_spec, o_in_spec, o_in_spec, o_in_spec] + wg_specs + wb_specs + bg_specs,
        out_specs=pl.BlockSpec((TM, tn), lambda j, i: (i, j)),
        scratch_shapes=[pltpu.VMEM((N_BRANCH, d, tn), BF16), pltpu.VMEM((N_BRANCH, w, tn), BF16)],
        compiler_params=_cparams(("arbitrary", "arbitrary")),
        name="merge",
    )(x, oa, ob, oc, w_gate, w_gate, w_gate, w_branch, w_branch, w_branch,
      b_gate.reshape(DEPTH, 1, N_BRANCH * d), b_gate.reshape(DEPTH, 1, N_BRANCH * d),
      b_gate.reshape(DEPTH, 1, N_BRANCH * d))


def _split_bf16(a):
    hi = a.astype(BF16)
    lo = (a - hi.astype(F32)).astype(BF16)
    return hi, lo


def _router_kernel(x_ref, w_ref, b_ref, gate_ref, expert_ref):
    xh, xl = _split_bf16(x_ref[...])
    wh, wl = _split_bf16(w_ref[...])
    logits = _dot(xh, wh) + (_dot(xh, wl) + _dot(xl, wh)) + b_ref[...]
    lane = lax.broadcasted_iota(jnp.int32, logits.shape, 1)
    logits = jnp.where(lane < N_EXPERTS, logits, -jnp.inf)
    m1 = jnp.max(logits, axis=-1, keepdims=True)
    i1 = jnp.min(jnp.where(logits == m1, lane, LANES), axis=-1, keepdims=True)
    rest = jnp.where(lane == i1, -jnp.inf, logits)
    m2 = jnp.max(rest, axis=-1, keepdims=True)
    i2 = jnp.min(jnp.where(rest == m2, lane, LANES), axis=-1, keepdims=True)
    e = jnp.exp(m2 - m1)
    p1 = 1.0 / (1.0 + e)
    p2 = e / (1.0 + e)
    gate_ref[...] = jnp.where(lane == 0, p1, jnp.where(lane == 1, p2, 0.0))
    expert_ref[...] = jnp.where(lane == 0, i1, jnp.where(lane == 1, i2, 0))


def _router(x32, w_r, b_r, tm=256):
    m, d = x32.shape
    w_pad = jnp.pad(w_r, ((0, 0), (0, LANES - N_EXPERTS)))
    b_pad = jnp.pad(b_r, (0, LANES - N_EXPERTS)).reshape(1, LANES)
    out_spec = pl.BlockSpec((tm, LANES), lambda i: (i, 0))
    gates, experts = pl.pallas_call(
        _router_kernel,
        out_shape=(jax.ShapeDtypeStruct((m, LANES), F32), jax.ShapeDtypeStruct((m, LANES), jnp.int32)),
        grid=(m // tm,),
        in_specs=[pl.BlockSpec((tm, d), lambda i: (i, 0)), pl.BlockSpec((d, LANES), lambda i: (0, 0)),
                  pl.BlockSpec((1, LANES), lambda i: (0, 0))],
        out_specs=(out_spec, out_spec),
        compiler_params=_cparams(("parallel",)),
        name="router",
    )(x32, w_pad, b_pad)
    return gates[:, :TOP_K], experts[:, :TOP_K]


def _dispatch_plan(experts, tile, token_rows):
    m = experts.shape[0]
    n_assign = TOP_K * m
    n_tiles = n_assign // tile + N_EXPERTS
    e = experts.reshape(n_assign)
    live = jnp.repeat(token_rows, TOP_K)
    onehot = jnp.logical_and(e[:, None] == jnp.arange(N_EXPERTS, dtype=jnp.int32)[None, :],
                             live[:, None]).astype(jnp.int32)
    rank = jnp.sum((jnp.cumsum(onehot, axis=0) - onehot) * onehot, axis=1)
    tiles_per = (jnp.sum(onehot, axis=0) + tile - 1) // tile
    tile_end = jnp.cumsum(tiles_per)
    pos = ((tile_end - tiles_per)[e] * tile + rank).astype(jnp.int32)
    slot = jnp.where(live, pos, n_tiles * tile)
    src = jnp.zeros((n_tiles * tile,), jnp.int32).at[slot].set(jnp.arange(n_assign, dtype=jnp.int32) // TOP_K,
                                                               mode="drop")
    pos = jnp.where(live, pos, 0)
    n_used = tile_end[-1:]
    t_idx = jnp.minimum(jnp.arange(n_tiles, dtype=jnp.int32), n_used - 1)
    tile_expert = jnp.sum((t_idx[:, None] >= tile_end[None, :]).astype(jnp.int32), axis=1)
    return src, pos, tile_expert.astype(jnp.int32), n_used.astype(jnp.int32)


def _issue_rows(idx_ref, first, count, stride, src_hbm, dst, sem):
    def body(i, carry):
        pltpu.make_async_copy(src_hbm.at[pl.ds(idx_ref[first + i * stride], 1), :], dst.at[pl.ds(i, 1), :],
                              sem).start()
        return carry

    lax.fori_loop(0, count, body, 0, unroll=8)


def _wait_rows(count, src_hbm, dst, sem):
    pltpu.make_async_copy(src_hbm.at[pl.ds(0, count), :], dst, sem).wait()


def _gather_kernel(src_ref, nu_ref, x_hbm, o_ref, buf, sem, *, tile):
    t = pl.program_id(0)

    @pl.when(t == 0)
    def _():
        _issue_rows(src_ref, 0, tile, 1, x_hbm, buf.at[0], sem.at[0])

    @pl.when(t + 1 < nu_ref[0])
    def _():
        nxt = (t + 1) % 2
        _issue_rows(src_ref, (t + 1) * tile, tile, 1, x_hbm, buf.at[nxt], sem.at[nxt])

    @pl.when(t < nu_ref[0])
    def _():
        cur = t % 2
        _wait_rows(tile, x_hbm, buf.at[cur], sem.at[cur])
        o_ref[...] = buf[cur].astype(o_ref.dtype)

    @pl.when(t >= nu_ref[0])
    def _():
        o_ref[...] = jnp.zeros(o_ref.shape, o_ref.dtype)


def _gather_rows(x32, src, n_used, tile):
    d = x32.shape[1]
    n_rows = src.shape[0]
    return pl.pallas_call(
        functools.partial(_gather_kernel, tile=tile),
        out_shape=jax.ShapeDtypeStruct((n_rows, d), BF16),
        grid_spec=pltpu.PrefetchScalarGridSpec(
            num_scalar_prefetch=2, grid=(n_rows // tile,),
            in_specs=[pl.BlockSpec(memory_space=pl.ANY)],
            out_specs=pl.BlockSpec((tile, d), lambda t, src, nu: (t, 0)),
            scratch_shapes=[pltpu.VMEM((2, tile, d), F32), pltpu.SemaphoreType.DMA((2,))]),
        compiler_params=_cparams(("arbitrary",)),
        name="moe_gather",
    )(src, n_used, x32)


def _combine_kernel(pos_ref, y_hbm, g_ref, o_ref, buf, sem, *, tile):
    t = pl.program_id(0)

    def issue(tile_idx, slot):
        for s in range(TOP_K):
            _issue_rows(pos_ref, tile_idx * tile * TOP_K + s, tile, TOP_K, y_hbm, buf.at[slot, s], sem.at[slot])

    @pl.when(t == 0)
    def _():
        issue(0, 0)

    @pl.when(t + 1 < pl.num_programs(0))
    def _():
        issue(t + 1, (t + 1) % 2)

    cur = t % 2
    for s in range(TOP_K):
        _wait_rows(tile, y_hbm, buf.at[cur, s], sem.at[cur])
    gates = g_ref[...]
    total = buf[cur, 0] * gates[:, 0:1]
    for s in range(1, TOP_K):
        total = total + buf[cur, s] * gates[:, s:s + 1]
    o_ref[...] = total


def _combine_rows(y_sorted, pos, gates, tile):
    m = gates.shape[0]
    d = y_sorted.shape[1]
    return pl.pallas_call(
        functools.partial(_combine_kernel, tile=tile),
        out_shape=jax.ShapeDtypeStruct((m, d), F32),
        grid_spec=pltpu.PrefetchScalarGridSpec(
            num_scalar_prefetch=1, grid=(m // tile,),
            in_specs=[pl.BlockSpec(memory_space=pl.ANY), pl.BlockSpec((tile, TOP_K), lambda t, pos: (t, 0))],
            out_specs=pl.BlockSpec((tile, d), lambda t, pos: (t, 0)),
            scratch_shapes=[pltpu.VMEM((2, TOP_K, tile, d), F32), pltpu.SemaphoreType.DMA((2,))]),
        compiler_params=_cparams(("arbitrary",)),
        name="moe_combine",
    )(pos, y_sorted, gates)


def _expert_changed(te_ref, t):
    return jnp.logical_or(t == 0, te_ref[t] != te_ref[jnp.maximum(t - 1, 0)])


def _moe_gate_up_kernel(te_ref, nu_ref, x_ref, wg_ref, wu_ref, o_ref, wgb_ref, wub_ref):
    t = pl.program_id(1)

    @pl.when(_expert_changed(te_ref, t))
    def _():
        wgb_ref[...] = wg_ref[...].astype(BF16)
        wub_ref[...] = wu_ref[...].astype(BF16)

    @pl.when(t < nu_ref[0])
    def _():
        x = x_ref[...]
        g = _dot(x, wgb_ref[...])
        u = _dot(x, wub_ref[...])
        o_ref[...] = (g * jax.nn.sigmoid(g) * u).astype(o_ref.dtype)

    @pl.when(t >= nu_ref[0])
    def _():
        o_ref[...] = jnp.zeros(o_ref.shape, o_ref.dtype)


def _moe_down_kernel(te_ref, nu_ref, h_ref, w_ref, o_ref, wb_ref):
    t = pl.program_id(1)

    @pl.when(_expert_changed(te_ref, t))
    def _():
        wb_ref[...] = w_ref[...].astype(BF16)

    @pl.when(t < nu_ref[0])
    def _():
        o_ref[...] = _dot(h_ref[...], wb_ref[...])

    @pl.when(t >= nu_ref[0])
    def _():
        o_ref[...] = jnp.zeros(o_ref.shape, o_ref.dtype)


def _moe_experts(x_sorted, tile_expert, n_used, w_gate, w_up, w_down, layer, tile, tn=512):
    r, d = x_sorted.shape
    ff = w_gate.shape[-1]
    n_tiles = r // tile

    def x_rows(width):
        return pl.BlockSpec((tile, width), lambda j, t, te, nu: (jnp.minimum(t, nu[0] - 1), 0))

    def w_cols(k):
        return pl.BlockSpec((None, None, k, tn), lambda j, t, te, nu: (layer, te[t], 0, j))

    h = pl.pallas_call(
        _moe_gate_up_kernel,
        out_shape=jax.ShapeDtypeStruct((r, ff), BF16),
        grid_spec=pltpu.PrefetchScalarGridSpec(
            num_scalar_prefetch=2, grid=(ff // tn, n_tiles),
            in_specs=[x_rows(d), w_cols(d), w_cols(d)],
            out_specs=pl.BlockSpec((tile, tn), lambda j, t, te, nu: (t, j)),
            scratch_shapes=[pltpu.VMEM((d, tn), BF16), pltpu.VMEM((d, tn), BF16)]),
        compiler_params=_cparams(("arbitrary", "arbitrary")),
        name="moe_gate_up",
    )(tile_expert, n_used, x_sorted, w_gate, w_up)
    return pl.pallas_call(
        _moe_down_kernel,
        out_shape=jax.ShapeDtypeStruct((r, d), F32),
        grid_spec=pltpu.PrefetchScalarGridSpec(
            num_scalar_prefetch=2, grid=(d // tn, n_tiles),
            in_specs=[x_rows(ff), w_cols(ff)],
            out_specs=pl.BlockSpec((tile, tn), lambda j, t, te, nu: (t, j)),
            scratch_shapes=[pltpu.VMEM((ff, tn), BF16)]),
        compiler_params=_cparams(("arbitrary", "arbitrary")),
        name="moe_down",
    )(tile_expert, n_used, h, w_down)


def _pool_kernel(u_ref, halo_ref, w_ref, sc_ref, _buf_ref, o_ref, ext_ref, *, tb, first_valid):
    row0 = pl.program_id(1) * tb
    r_cur = row0 + lax.broadcasted_iota(jnp.int32, (tb, 1), 0)
    r_halo = row0 - HALO + lax.broadcasted_iota(jnp.int32, (HALO, 1), 0)
    u = jnp.where(r_cur >= first_valid, u_ref[...], 0.0)
    ext_ref[0:HALO, :] = jnp.where(r_halo >= first_valid, halo_ref[...], 0.0)
    ext_ref[HALO:, :] = u
    seen = (r_cur - first_valid + 1).astype(F32)
    for g, win in enumerate(POOL_WINDOWS):
        c0, c1 = g * POOL_GROUP, (g + 1) * POOL_GROUP
        s = u[:, c0:c1]
        for back in range(1, win):
            s = s + ext_ref[HALO - back:HALO - back + tb, c0:c1]
        cnt = jnp.clip(seen, 1.0, float(win))
        diff = s / cnt - u[:, c0:c1]
        y = _dot(diff.astype(BF16), w_ref[g].astype(BF16))
        o_ref[:, c0:c1] = (y * sc_ref[:, c0:c1]).astype(o_ref.dtype)


def _pool(u2d, col_block, halo2d, halo_col_block, pool_w_l, pool_scale_l, out_buf, *, batch, rows, tb, row0,
          halo_map, first_valid):
    nb = rows // tb
    base = row0 // tb
    return pl.pallas_call(
        functools.partial(_pool_kernel, tb=tb, first_valid=first_valid),
        out_shape=jax.ShapeDtypeStruct(out_buf.shape, out_buf.dtype),
        grid=(batch, nb),
        in_specs=[
            pl.BlockSpec((tb, BRANCH_WIDTH), lambda b, i: (base + b * nb + i, col_block)),
            pl.BlockSpec((HALO, BRANCH_WIDTH), lambda b, i: (halo_map(b, i), halo_col_block)),
            pl.BlockSpec((len(POOL_WINDOWS), POOL_GROUP, POOL_GROUP), lambda b, i: (0, 0, 0)),
            pl.BlockSpec((1, BRANCH_WIDTH), lambda b, i: (0, 0)),
            pl.BlockSpec(memory_space=pl.ANY),
        ],
        out_specs=pl.BlockSpec((tb, BRANCH_WIDTH), lambda b, i: (base + b * nb + i, 0)),
        scratch_shapes=[pltpu.VMEM((HALO + tb, BRANCH_WIDTH), F32)],
        input_output_aliases={4: 0},
        compiler_params=_cparams(("parallel", "arbitrary")),
        name="pool",
    )(u2d, halo2d, pool_w_l, pool_scale_l.reshape(1, BRANCH_WIDTH), out_buf)


def _ret_kernel(q_ref, k_ref, v_ref, g_ref, cos_ref, sin_ref, dmat_ref, cross_ref, kdec_ref, gn_ref, s0_ref,
                _buf_ref, o_ref, s_out_ref, s_scr, *, first_valid):
    i = pl.program_id(1)

    @pl.when(i == 0)
    def _():
        s_scr[...] = s0_ref[...]

    rows = i * CHUNK + lax.broadcasted_iota(jnp.int32, (CHUNK, 1), 0)
    valid = rows >= first_valid
    cos = cos_ref[...]
    sin = sin_ref[...]
    rscale = RET_HEAD_DIM ** -0.5
    half = RET_HEAD_DIM // 2
    for h in range(RET_HEADS):
        sl = slice(h * RET_HEAD_DIM, (h + 1) * RET_HEAD_DIM)
        q = q_ref[:, sl]
        k = k_ref[:, sl]
        v = jnp.where(valid, v_ref[:, sl], 0.0).astype(BF16)
        qr = (q * cos + pltpu.roll(q, half, 1) * sin).astype(BF16)
        kr = jnp.where(valid, (k * cos + pltpu.roll(k, half, 1) * sin) * rscale, 0.0)
        state = s_scr[h]
        inner = _dot_nt(qr, kr.astype(BF16)) * dmat_ref[h]
        o = _dot(inner.astype(BF16), v) + _dot(qr, state.astype(BF16)) * cross_ref[h]
        s_scr[h] = gn_ref[h] * state + _dot_tn((kr * kdec_ref[h]).astype(BF16), v)
        mu = jnp.mean(o, axis=-1, keepdims=True)
        oc = o - mu
        var = jnp.mean(oc * oc, axis=-1, keepdims=True)
        gate = g_ref[:, sl]
        o_ref[:, sl] = (gate * jax.nn.sigmoid(gate) * (oc * lax.rsqrt(var + LN_EPS))).astype(o_ref.dtype)

    @pl.when(i == pl.num_programs(1) - 1)
    def _():
        s_out_ref[...] = s_scr[...]


def _retention(p3, s0, cos, sin, tabs, out_buf, *, batch, rows, row0, first_valid):
    nb = rows // CHUNK
    base = row0 // CHUNK
    dmat, cross, kdec, gn = tabs

    def col(c):
        return pl.BlockSpec((CHUNK, BRANCH_WIDTH), lambda b, i: (base + b * nb + i, c))

    tab_rows = pl.BlockSpec((CHUNK, RET_HEAD_DIM), lambda b, i: (i, 0))

    def full(a):
        return pl.BlockSpec(a.shape, lambda b, i: (0,) * a.ndim)

    state_spec = pl.BlockSpec((None, RET_HEADS, RET_HEAD_DIM, RET_HEAD_DIM), lambda b, i: (b, 0, 0, 0))
    return pl.pallas_call(
        functools.partial(_ret_kernel, first_valid=first_valid),
        out_shape=(jax.ShapeDtypeStruct(out_buf.shape, out_buf.dtype),
                   jax.ShapeDtypeStruct((batch, RET_HEADS, RET_HEAD_DIM, RET_HEAD_DIM), F32)),
        grid=(batch, nb),
        in_specs=[col(1), col(2), col(3), col(4), tab_rows, tab_rows, full(dmat), full(cross), full(kdec),
                  full(gn), state_spec, pl.BlockSpec(memory_space=pl.ANY)],
        out_specs=(pl.BlockSpec((CHUNK, BRANCH_WIDTH), lambda b, i: (base + b * nb + i, 0)), state_spec),
        scratch_shapes=[pltpu.VMEM((RET_HEADS, RET_HEAD_DIM, RET_HEAD_DIM), F32)],
        input_output_aliases={11: 0},
        compiler_params=_cparams(("parallel", "arbitrary")),
        name="retention",
    )(p3, p3, p3, p3, cos, sin, dmat, cross, kdec, gn, s0, out_buf)


def _sortable(x):
    bits = pltpu.bitcast(x + 0.0, jnp.int32)
    return bits ^ ((bits >> 31) & 0x7FFFFFFF)


def _dsa_kernel(q_ref, qi_ref, w_ref, k_ref, vt_ref, ki_ref, bn_ref, ok_ref, _buf_ref, o_ref,
                key_ref, keyn_ref, m_ref, den_ref, acc_ref, *, first_real, qb0, skip_below, k_sel):
    j = pl.program_id(1)

    @pl.when(j < skip_below)
    def _():
        o_ref[...] = jnp.zeros(o_ref.shape, o_ref.dtype)

    @pl.when(j >= skip_below)
    def _():
        win0 = pl.multiple_of(j * QT + (qb0 * CHUNK - (WN - QT)), LANES)
        n_far = (win0 + KEY_TILE - 1) // KEY_TILE
        near = pl.ds(win0, WN)

        def index_keys(ki, adm):
            acc = None
            for h in range(0, IDX_HEADS, 2):
                pair = _dot(ki, jnp.concatenate([qi_ref[0, h], qi_ref[0, h + 1]], axis=1))
                term = (w_ref[0, h:h + 1, :] * jnp.maximum(pair[:, :QT], 0.0)
                        + w_ref[0, h + 1:h + 2, :] * jnp.maximum(pair[:, QT:], 0.0))
                acc = term if acc is None else acc + term
            return jnp.where(adm, _sortable(acc), INT_MIN)

        def far_keys(c, carry):
            c0 = pl.multiple_of(c * KEY_TILE, KEY_TILE)
            row = c0 + lax.broadcasted_iota(jnp.int32, (KEY_TILE, 1), 0)
            adm = jnp.logical_and(row >= first_real, row < win0)
            key_ref[c] = index_keys(ki_ref[pl.ds(c0, KEY_TILE), :], adm)
            return carry

        lax.fori_loop(0, n_far, far_keys, 0)
        row_n = win0 + lax.broadcasted_iota(jnp.int32, (WN, 1), 0)
        keyn_ref[...] = index_keys(ki_ref[near, :], jnp.logical_and(row_n >= first_real, ok_ref[...] != 0))

        def fold(hit):
            parts = hit.reshape(hit.shape[0] // SUBLANES, SUBLANES, QT)
            while parts.shape[0] > 1:
                half = parts.reshape(parts.shape[0] // 2, 2, SUBLANES, QT)
                parts = half[:, 0] + half[:, 1]
            return parts[0]

        def count_ge(c):
            part = lax.fori_loop(0, n_far, lambda t, p: p + fold(jnp.where(key_ref[t] >= c, 1.0, 0.0)),
                                 fold(jnp.where(keyn_ref[...] >= c, 1.0, 0.0)))
            return jnp.sum(part, axis=0, keepdims=True)

        zero = jnp.zeros((1, QT), jnp.int32)
        thr0 = jnp.where(count_ge(zero) >= k_sel, zero, INT_MIN)

        def bit_step(it, thr):
            cand = thr + jnp.left_shift(jnp.int32(1), jnp.int32(30) - it)
            return jnp.where(count_ge(cand) >= k_sel, cand, thr)

        thr = lax.fori_loop(0, 31, bit_step, thr0)
        thr = jnp.maximum(thr, INT_MIN + 1)

        surplus = jnp.maximum(count_ge(thr) - k_sel, 0.0)

        @pl.when(jnp.max(surplus) > 0.0)
        def _():
            ties_kept = count_ge(thr) - count_ge(thr + 1) - surplus
            lower = (lax.broadcasted_iota(jnp.int32, (KEY_TILE, KEY_TILE), 0)
                     >= lax.broadcasted_iota(jnp.int32, (KEY_TILE, KEY_TILE), 1)).astype(BF16)

            def drop_late_ties(key, seen):
                tie = key == thr
                place = seen + _dot(lower[:key.shape[0], :key.shape[0]], jnp.where(tie, 1.0, 0.0).astype(BF16))
                key = jnp.where(jnp.logical_and(tie, place > ties_kept), thr - 1, key)
                return key, place[key.shape[0] - 1:, :]

            def far_ties(t, seen):
                key_ref[t], seen = drop_late_ties(key_ref[t], seen)
                return seen

            seen = lax.fori_loop(0, n_far, far_ties, jnp.zeros((1, QT), F32))
            keyn_ref[...], _ = drop_late_ties(keyn_ref[...], seen)

        m_ref[...] = jnp.full(m_ref.shape, M_INIT, F32)
        den_ref[...] = jnp.zeros(den_ref.shape, F32)
        acc_ref[...] = jnp.zeros(acc_ref.shape, F32)

        def attend(h, logits, sel, vt_tiles):
            lg = jnp.where(sel, logits, NEG)
            m_old = m_ref[h]
            m_new = jnp.maximum(m_old, jnp.max(lg, axis=0, keepdims=True))
            p = jnp.exp2(lg - m_new)
            alpha = jnp.exp2(m_old - m_new)
            den_ref[h] = alpha * den_ref[h] + jnp.sum(p, axis=0, keepdims=True)
            pb = p.astype(BF16)
            pv = None
            for u, vt in enumerate(vt_tiles):
                term = _dot(vt, pb[u * LANES:(u + 1) * LANES])
                pv = term if pv is None else pv + term
            acc_ref[h] = alpha * acc_ref[h] + pv
            m_ref[h] = m_new

        def far_tile(c, carry):
            c0 = pl.multiple_of(c * KEY_TILE, KEY_TILE)
            sel = key_ref[c] >= thr
            for h in range(A_HEADS):
                sl = slice(h * A_HEAD_DIM, (h + 1) * A_HEAD_DIM)
                logits = _dot(k_ref[pl.ds(c0, KEY_TILE), sl], q_ref[0, h])
                attend(h, logits, sel,
                       [vt_ref[c * (KEY_TILE // LANES) + u, sl, :] for u in range(KEY_TILE // LANES)])
            return carry

        lax.fori_loop(0, n_far, far_tile, 0)
        sel_n = keyn_ref[...] >= thr
        wt = win0 // LANES
        for h in range(A_HEADS):
            sl = slice(h * A_HEAD_DIM, (h + 1) * A_HEAD_DIM)
            logits = _dot(k_ref[near, sl], q_ref[0, h]) + bn_ref[h]
            attend(h, logits, sel_n, [vt_ref[wt + u, sl, :] for u in range(WN // LANES)])
            out_t = acc_ref[h] / jnp.maximum(den_ref[h], TINY)
            o_ref[:, sl] = out_t.T[:o_ref.shape[0]].astype(o_ref.dtype)


def _dsa(q_t, qi_t, w_t, k16, v_t, ki16, bias_near, near_ok, out_buf, *, batch, nb, n_keys, out_rows, out_base,
         first_real, qb0, skip_below, k_sel):
    width = k16.shape[1]
    return pl.pallas_call(
        functools.partial(_dsa_kernel, first_real=first_real, qb0=qb0, skip_below=skip_below, k_sel=k_sel),
        out_shape=jax.ShapeDtypeStruct(out_buf.shape, out_buf.dtype),
        grid=(batch, nb),
        in_specs=[
            pl.BlockSpec((1, A_HEADS, A_HEAD_DIM, QT), lambda b, j: (b * nb + j, 0, 0, 0)),
            pl.BlockSpec((1, IDX_HEADS, IDX_DIM, QT), lambda b, j: (b * nb + j, 0, 0, 0)),
            pl.BlockSpec((1, IDX_HEADS, QT), lambda b, j: (b * nb + j, 0, 0)),
            pl.BlockSpec((n_keys, width), lambda b, j: (b, 0)),
            pl.BlockSpec((n_keys // LANES, width, LANES), lambda b, j: (b, 0, 0)),
            pl.BlockSpec((n_keys, IDX_DIM), lambda b, j: (b, 0)),
            pl.BlockSpec((A_HEADS, WN, QT), lambda b, j: (0, 0, 0)),
            pl.BlockSpec((WN, QT), lambda b, j: (0, 0)),
            pl.BlockSpec(memory_space=pl.ANY),
        ],
        out_specs=pl.BlockSpec((out_rows, width), lambda b, j: (out_base + b * nb + j, 0)),
        scratch_shapes=[pltpu.VMEM((n_keys // KEY_TILE, KEY_TILE, QT), jnp.int32),
                        pltpu.VMEM((WN, QT), jnp.int32),
                        pltpu.VMEM((A_HEADS, 1, QT), F32),
                        pltpu.VMEM((A_HEADS, 1, QT), F32),
                        pltpu.VMEM((A_HEADS, A_HEAD_DIM, QT), F32)],
        input_output_aliases={8: 0},
        compiler_params=_cparams(("parallel", "arbitrary")),
        name="dsa",
    )(q_t, qi_t, w_t, k16, v_t, ki16, bias_near, near_ok, out_buf)


def _sample_keys_kernel(ck_ref, cv_ref, cki_ref, nk_ref, nv_ref, nki_ref, k_ref, vt_ref, ki_ref):
    c = pl.program_id(1)
    n_cache = pl.num_programs(1) - 1
    rows = k_ref.shape[0]

    @pl.when(c < n_cache)
    def _():
        for h in range(A_HEADS):
            sl = slice(h * A_HEAD_DIM, (h + 1) * A_HEAD_DIM)
            head_rows = pl.ds(h, rows, stride=A_HEADS)
            k_ref[:, sl] = ck_ref[0, head_rows, :].astype(BF16)
            vh = cv_ref[0, head_rows, :]
            for u in range(rows // LANES):
                vt_ref[u, sl, :] = vh[u * LANES:(u + 1) * LANES, :].T.astype(BF16)
        ki_ref[...] = cki_ref[0].astype(BF16)

    @pl.when(c == n_cache)
    def _():
        k_ref[...] = jnp.zeros(k_ref.shape, BF16)
        vt_ref[...] = jnp.zeros(vt_ref.shape, BF16)
        ki_ref[...] = jnp.zeros(ki_ref.shape, BF16)
        k_ref[0:CHUNK, :] = nk_ref[...].astype(BF16)
        ki_ref[0:CHUNK, :] = nki_ref[:, :IDX_DIM].astype(BF16)
        pad = jnp.zeros((LANES - CHUNK, A_HEAD_DIM), F32)
        for h in range(A_HEADS):
            sl = slice(h * A_HEAD_DIM, (h + 1) * A_HEAD_DIM)
            vt_ref[0, sl, :] = jnp.concatenate([nv_ref[:, sl], pad], axis=0).T.astype(BF16)


def _sample_keys(l, cache_k, cache_v, cache_ki, k_new, v_new, ki_new, *, new_row0, n_keys, rows=512):
    _, bs, past = cache_k.shape[:3]
    width = k_new.shape[1]
    n_cache = past // rows
    nb = -(-n_keys // rows)
    new_blk = new_row0 // CHUNK

    def cached(block_rows, w):
        return pl.BlockSpec((None, 1, block_rows, w), lambda b, c: (l, b, jnp.minimum(c, n_cache - 1), 0))

    def new(w):
        return pl.BlockSpec((CHUNK, w), lambda b, c: (new_blk + b, 0))

    kv_shape = cache_k.shape[:2] + (past * A_HEADS, A_HEAD_DIM)

    k16, v_t, ki16 = pl.pallas_call(
        _sample_keys_kernel,
        out_shape=(jax.ShapeDtypeStruct((bs, n_keys, width), BF16),
                   jax.ShapeDtypeStruct((bs, n_keys // LANES, width, LANES), BF16),
                   jax.ShapeDtypeStruct((bs, n_keys, IDX_DIM), BF16)),
        grid=(bs, nb),
        in_specs=[cached(rows * A_HEADS, A_HEAD_DIM), cached(rows * A_HEADS, A_HEAD_DIM), cached(rows, IDX_DIM),
                  new(width), new(width), new(LANES)],
        out_specs=(pl.BlockSpec((None, rows, width), lambda b, c: (b, c, 0)),
                   pl.BlockSpec((None, rows // LANES, width, LANES), lambda b, c: (b, c, 0, 0)),
                   pl.BlockSpec((None, rows, IDX_DIM), lambda b, c: (b, c, 0))),
        compiler_params=_cparams(("parallel", "arbitrary")),
        name="sample_keys",
    )(cache_k.reshape(kv_shape), cache_v.reshape(kv_shape), cache_ki, k_new, v_new, ki_new)
    return (k16.reshape(bs * n_keys, width), v_t.reshape(bs * (n_keys // LANES), width, LANES),
            ki16.reshape(bs * n_keys, IDX_DIM))


def _t5_bucket(rel):
    half = T5_BUCKETS // 2
    exact = half // 2
    n = jnp.abs(rel)
    large = exact + (jnp.log(jnp.maximum(n, 1).astype(F32) / exact)
                     / math.log(T5_MAX_DIST / exact) * (half - exact)).astype(jnp.int32)
    large = jnp.minimum(large, half - 1)
    return jnp.where(rel > 0, half, 0) + jnp.where(n < exact, n, large)


def _bias_tables(t5_bias):
    a = jnp.arange(WN, dtype=jnp.int32)[:, None]
    t = jnp.arange(QT, dtype=jnp.int32)[None, :]
    bucket = _t5_bucket(a - (WN - QT) - t)
    onehot = (bucket[:, :, None] == jnp.arange(T5_BUCKETS, dtype=jnp.int32)).astype(F32)
    near = jnp.einsum("atk,kh->hat", onehot, t5_bias.astype(F32), precision=lax.Precision.HIGHEST)
    far = t5_bias[_t5_bucket(jnp.int32(-2 * CHUNK - 1))].astype(F32)
    ok = (a // CHUNK - (WN - QT) // CHUNK <= t // CHUNK).astype(jnp.int32)
    return (near - far[:, None, None]) * math.log2(math.e), ok


def _rope_tables(pos):
    half = RET_HEAD_DIM // 2
    inv = ROPE_BASE ** (-jnp.arange(half, dtype=F32) / half)
    ang = pos.astype(F32)[:, None] * inv[None, :]
    cos, sin = jnp.cos(ang), jnp.sin(ang)
    return jnp.concatenate([cos, cos], axis=-1), jnp.concatenate([-sin, sin], axis=-1)


def _decay_tables():
    n = CHUNK
    log_g = jnp.log(1.0 - 2.0 ** (-5.0 - jnp.arange(RET_HEADS, dtype=F32)))
    i = jnp.arange(n, dtype=F32)
    diff = i[:, None] - i[None, :]
    dmat = jnp.where(diff >= 0, jnp.exp(jnp.maximum(diff, 0.0)[None] * log_g[:, None, None]), 0.0)
    cross = jnp.exp((i[None, :] + 1.0) * log_g[:, None])
    kdec = jnp.exp((n - 1.0 - i)[None, :] * log_g[:, None])
    gn = jnp.exp(n * log_g)
    wide = (RET_HEADS, n, RET_HEAD_DIM)
    return (dmat, jnp.broadcast_to(cross[:, :, None], wide), jnp.broadcast_to(kdec[:, :, None], wide),
            jnp.broadcast_to(gn[:, None, None], (RET_HEADS, 1, RET_HEAD_DIM)))


def kernel(x_prompt, x_sample, cache_k, cache_v, cache_ki, cache_pool, state_ret, meta_tokens, ln_in_g, ln_in_b, w_in, t5_bias, pool_w, pool_scale, w_branch, w_gate, b_gate, w_out, ln1_g, ln1_b, ln2_g, ln2_b, ffn_w_gate, ffn_w_up, ffn_w_down, moe_w_router, moe_b_router, moe_w_gate, moe_w_up, moe_w_down):
    bp, seq, d = x_prompt.shape
    bs, ts, _ = x_sample.shape
    past = cache_k.shape[2]
    t_real = seq + N_META
    tp = -(-(t_real + 2 * CHUNK) // KEY_TILE) * KEY_TILE
    front = tp - t_real
    assert front % CHUNK == CHUNK - N_META and ts == CHUNK and past % CHUNK == 0
    mp, ms = bp * tp, bs * ts
    m = mp + ms
    assert mp % TM == 0 and ms % TM == 0
    ksel_p = min(TOPK_MAX, seq // 4)
    ksel_s = min(TOPK_MAX, (past + ts) // 4)
    n_keys_s = -(-(past + QT) // KEY_TILE) * KEY_TILE
    assert tp % QT == 0 and front >= WN - QT and past % QT == 0 and past + QT >= WN

    assert front + N_META == LN_TILE and seq % LN_TILE == 0 and ms % LN_TILE == 0
    head = jnp.concatenate([jnp.zeros((front, d), F32), meta_tokens.astype(F32)], axis=0)
    x32, xb = _input_norm(head, x_prompt, x_sample.reshape(ms, d), ln_in_g, ln_in_b)

    bias_near, near_ok = _bias_tables(t5_bias)
    cos_p, sin_p = _rope_tables(jnp.arange(tp, dtype=jnp.int32) - (front + N_META))
    cos_s, sin_s = _rope_tables(past + jnp.arange(ts, dtype=jnp.int32))
    decay = _decay_tables()
    zero_state = jnp.zeros((bp, RET_HEADS, RET_HEAD_DIM, RET_HEAD_DIM), F32)
    w_in_t = jnp.swapaxes(w_in, 1, 2)
    assert TAIL_OFF % SUBLANES == 0
    row_in_seq = jnp.arange(mp, dtype=jnp.int32) % tp
    token_rows = jnp.concatenate([row_in_seq >= front, jnp.ones((ms,), jnp.bool_)])

    outs = {name: [] for name in ("kp", "vp", "kip", "poolp", "retp", "ks", "vs", "kis", "pools", "rets")}
    bw = BRANCH_WIDTH
    for l in range(DEPTH):
        q16, q_t = _proj(xb, w_in_t, (l,), 0, nat16=True, trans=True, out_scale=ATTN_LOG2_SCALE, name="proj_q")
        k_new, k16 = _proj(xb, w_in_t, (l,), 1, nat32=True, nat16=True, name="proj_k")
        v_new, v_t = _proj(xb, w_in_t, (l,), 2, nat32=True, trans=True, name="proj_v")
        qi16, qi_t = _proj(xb, w_in_t, (l,), 3, nat16=True, trans=True, name="proj_qi")
        p2, ki16, w_t = _proj_idx(xb, w_in_t, (l,), 4 * bw // LANES)
        p3 = _matmul(xb, w_in_t, (l,), TAIL_OFF, 5, bw, name="proj_tail", w_transposed=True)
        ki_new = p2[:, :IDX_DIM]
        qi_t = qi_t.reshape(m // LANES, IDX_HEADS, IDX_DIM, LANES)
        v_t = v_t.reshape(m // LANES, bw, LANES)

        def sample_queries_t(a16, heads):
            a = jnp.pad(a16[mp:].reshape(bs, ts, heads, bw // heads), ((0, 0), (0, QT - ts), (0, 0), (0, 0)))
            return jnp.transpose(a, (0, 2, 3, 1))

        w_t_s = jnp.transpose(jnp.pad(p2[mp:, IDX_DIM:IDX_DIM + IDX_HEADS].reshape(bs, ts, IDX_HEADS),
                                      ((0, 0), (0, QT - ts), (0, 0))), (0, 2, 1))
        k16_s, v_t_s, ki16_s = _sample_keys(l, cache_k, cache_v, cache_ki, k_new, v_new, p2, new_row0=mp,
                                            n_keys=n_keys_s)

        branch_buf = jnp.zeros((m, bw), BF16)
        oa = _dsa(q_t, qi_t, w_t, k16, v_t, ki16, bias_near, near_ok, branch_buf, batch=bp, nb=tp // QT, n_keys=tp,
                  out_rows=QT, out_base=0, first_real=front, qb0=0, skip_below=max(1, front // QT), k_sel=ksel_p)
        oa = _dsa(sample_queries_t(q16, A_HEADS), sample_queries_t(qi16, IDX_HEADS), w_t_s, k16_s, v_t_s, ki16_s,
                  bias_near, near_ok, oa, batch=bs, nb=1, n_keys=n_keys_s, out_rows=ts, out_base=mp // ts,
                  first_real=0, qb0=past // CHUNK, skip_below=0, k_sel=ksel_s)

        tb_p = 256
        ob = _pool(p3, 0, p3, 0, pool_w[l], pool_scale[l], branch_buf, batch=bp, rows=tp, tb=tb_p, row0=0,
                   halo_map=lambda b, i: jnp.maximum((b * tp + i * tb_p) // HALO - 1, 0), first_valid=front)
        pool_hist = jnp.pad(cache_pool[l], ((0, 0), (HALO - POOL_PAST, 0), (0, 0))).reshape(bs * HALO, bw)
        ob = _pool(p3, 0, pool_hist, 0, pool_w[l], pool_scale[l], ob, batch=bs, rows=ts, tb=ts, row0=mp,
                   halo_map=lambda b, i: b, first_valid=-POOL_PAST)

        oc, ret_p = _retention(p3, zero_state, cos_p, sin_p, decay, branch_buf, batch=bp, rows=tp, row0=0,
                               first_valid=front)
        oc, ret_s = _retention(p3, state_ret[l].astype(F32), cos_s, sin_s, decay, oc, batch=bs, rows=ts, row0=mp,
                               first_valid=0)

        merged = _merge(l, xb, oa, ob, oc, w_branch, w_gate, b_gate)
        x32, xb = _matmul_norm(merged, w_out, (l,), x32, ln1_g[l], ln1_b[l])

        if l % 2 == 0:
            h = _gate_up(xb, ffn_w_gate, ffn_w_up, (l // 2,))
            f = _matmul(h, ffn_w_down, (l // 2,), 0, d // 512, 512, name="ffn_down")
        else:
            gates, experts = _router(x32, moe_w_router[l // 2], moe_b_router[l // 2])
            src, pos, tile_expert, n_used = _dispatch_plan(experts, MOE_TILE, token_rows)
            x_sorted = _gather_rows(x32, src, n_used, MOE_TILE)
            y_sorted = _moe_experts(x_sorted, tile_expert, n_used, moe_w_gate, moe_w_up, moe_w_down, l // 2, MOE_TILE)
            f = _combine_rows(y_sorted, pos, gates, MOE_TILE)
        if l + 1 < DEPTH:
            x32, xb = _layer_norm(x32, f, ln2_g[l], ln2_b[l])
        else:
            y_prompt, y_sample = _output_norm(x32, f, ln2_g[l], ln2_b[l], bp, seq)

        def prompt_rows(a, width):
            return a[:mp].reshape(bp, tp, width)[:, front:]

        u = p3[:, :bw]
        outs["kp"].append(prompt_rows(k_new, bw).reshape(bp, t_real, A_HEADS, A_HEAD_DIM))
        outs["vp"].append(prompt_rows(v_new, bw).reshape(bp, t_real, A_HEADS, A_HEAD_DIM))
        outs["kip"].append(prompt_rows(ki_new, IDX_DIM))
        outs["poolp"].append(prompt_rows(u, bw)[:, -POOL_PAST:])
        outs["retp"].append(ret_p)
        outs["ks"].append(k_new[mp:].reshape(bs, ts, A_HEADS, A_HEAD_DIM))
        outs["vs"].append(v_new[mp:].reshape(bs, ts, A_HEADS, A_HEAD_DIM))
        outs["kis"].append(ki_new[mp:].reshape(bs, ts, IDX_DIM))
        outs["pools"].append(u[mp:].reshape(bs, ts, bw)[:, -POOL_PAST:])
        outs["rets"].append(ret_s)

    return (y_prompt, y_sample.reshape(bs, ts, d)) + tuple(
        jnp.stack(outs[name]) for name in ("kp", "vp", "kip", "poolp", "retp", "ks", "vs", "kis", "pools", "rets"))
```

```python
import functools
import math

import jax
import jax.numpy as jnp
from jax import lax
from jax.experimental import pallas as pl
from jax.experimental.pallas import tpu as pltpu

F32 = jnp.float32
BF16 = jnp.bfloat16

D_MODEL = 2048
DEPTH = 2
CHUNK = 64
N_META = 16
BRANCH_WIDTH = D_MODEL // 2
A_HEADS = 8
A_HEAD_DIM = BRANCH_WIDTH // A_HEADS
IDX_HEADS = 16
IDX_DIM = 64
TOPK_MAX = 256
T5_BUCKETS = 32
T5_MAX_DIST = 128
POOL_WINDOWS = (2, 4, 8, 16)
POOL_GROUP = BRANCH_WIDTH // 4
POOL_PAST = 15
RET_HEADS = 8
RET_HEAD_DIM = BRANCH_WIDTH // RET_HEADS
ROPE_BASE = 10000.0
N_BRANCH = 3
D_FF = 11 * D_MODEL // 4
N_EXPERTS = 8
TOP_K = 2
ALPHA = (2 * DEPTH) ** 0.25
LN_EPS = 1e-5
IN_SPLITS = (BRANCH_WIDTH, BRANCH_WIDTH, BRANCH_WIDTH, IDX_HEADS * IDX_DIM, IDX_DIM, IDX_HEADS,
             BRANCH_WIDTH, BRANCH_WIDTH, BRANCH_WIDTH, BRANCH_WIDTH, BRANCH_WIDTH)
IN_WIDTH = sum(IN_SPLITS)
TAIL_OFF = 4 * BRANCH_WIDTH + IDX_DIM + IDX_HEADS

LANES = 128
SUBLANES = 8
HALO = 16
QT = 2 * CHUNK
WN = QT + 2 * CHUNK
KEY_TILE = 256
VMEM_LIMIT = 56 * 1024 * 1024
TM = 512
MOE_TILE = 512
LN_TILE = 256
INT_MIN = -2 ** 31
NEG = -1e30
M_INIT = -1e20
TINY = 1e-30
ATTN_LOG2_SCALE = A_HEAD_DIM ** -0.5 * math.log2(math.e)


def _cparams(sem):
    return pltpu.CompilerParams(dimension_semantics=sem, vmem_limit_bytes=VMEM_LIMIT)


def _dot(a, b):
    return jnp.dot(a, b, preferred_element_type=F32)


def _dot_nt(a, b):
    return lax.dot_general(a, b, (((1,), (1,)), ((), ())), preferred_element_type=F32)


def _dot_tn(a, b):
    return lax.dot_general(a, b, (((0,), (0,)), ((), ())), preferred_element_type=F32)


def _ln(x, g_ref, b_ref):
    mu = jnp.mean(x, axis=-1, keepdims=True)
    xc = x - mu
    var = jnp.mean(xc * xc, axis=-1, keepdims=True)
    return xc * lax.rsqrt(var + LN_EPS) * g_ref[...] + b_ref[...]


def _ln_res_kernel(x_ref, y_ref, g_ref, b_ref, o32_ref, o16_ref):
    y = _ln(ALPHA * x_ref[...] + y_ref[...], g_ref, b_ref)
    o32_ref[...] = y
    o16_ref[...] = y.astype(BF16)


def _layer_norm(x, y, g, b, tm=LN_TILE):
    m, d = x.shape
    row = pl.BlockSpec((tm, d), lambda i: (i, 0))
    vec = pl.BlockSpec((1, d), lambda i: (0, 0))
    return pl.pallas_call(
        _ln_res_kernel,
        out_shape=(jax.ShapeDtypeStruct((m, d), F32), jax.ShapeDtypeStruct((m, d), BF16)),
        grid=(m // tm,),
        in_specs=[row, row, vec, vec],
        out_specs=(row, row),
        compiler_params=_cparams(("parallel",)),
        name="layer_norm",
    )(x, y, g.reshape(1, d), b.reshape(1, d))


def _seq_tile_maps(bp, seq, tm):
    tiles_per_seq = seq // tm + 1
    n_prompt_tiles = bp * tiles_per_seq

    def prompt_map(i):
        in_prompt = i < n_prompt_tiles
        b = jnp.minimum(i // tiles_per_seq, bp - 1)
        r = jnp.where(in_prompt, jnp.maximum(i % tiles_per_seq - 1, 0), seq // tm - 1)
        return (b, r, 0)

    def sample_map(i):
        return (jnp.maximum(i - n_prompt_tiles, 0), 0)

    return tiles_per_seq, n_prompt_tiles, prompt_map, sample_map


def _ln_in_kernel(head_ref, xp_ref, xs_ref, g_ref, b_ref, o32_ref, o16_ref, *, tiles_per_seq, n_prompt_tiles):
    i = pl.program_id(0)
    x = jnp.where(i >= n_prompt_tiles, xs_ref[...],
                  jnp.where(i % tiles_per_seq == 0, head_ref[...], xp_ref[...]))
    y = _ln(x, g_ref, b_ref)
    o32_ref[...] = y
    o16_ref[...] = y.astype(BF16)


def _input_norm(head, x_prompt, x_sample, g, b, tm=LN_TILE):
    bp, seq, d = x_prompt.shape
    ms = x_sample.shape[0]
    tiles_per_seq, n_prompt_tiles, prompt_map, sample_map = _seq_tile_maps(bp, seq, tm)
    m = n_prompt_tiles * tm + ms
    row = pl.BlockSpec((tm, d), lambda i: (i, 0))
    vec = pl.BlockSpec((1, d), lambda i: (0, 0))
    return pl.pallas_call(
        functools.partial(_ln_in_kernel, tiles_per_seq=tiles_per_seq, n_prompt_tiles=n_prompt_tiles),
        out_shape=(jax.ShapeDtypeStruct((m, d), F32), jax.ShapeDtypeStruct((m, d), BF16)),
        grid=(m // tm,),
        in_specs=[pl.BlockSpec((tm, d), lambda i: (0, 0)), pl.BlockSpec((None, tm, d), prompt_map),
                  pl.BlockSpec((tm, d), sample_map), vec, vec],
        out_specs=(row, row),
        compiler_params=_cparams(("parallel",)),
        name="input_norm",
    )(head, x_prompt, x_sample, g.reshape(1, d), b.reshape(1, d))


def _ln_out_kernel(x_ref, y_ref, g_ref, b_ref, yp_ref, ys_ref, *, tiles_per_seq, n_prompt_tiles):
    i = pl.program_id(0)
    out = _ln(ALPHA * x_ref[...] + y_ref[...], g_ref, b_ref)

    @pl.when(jnp.logical_and(i < n_prompt_tiles, i % tiles_per_seq != 0))
    def _():
        yp_ref[...] = out

    @pl.when(i >= n_prompt_tiles)
    def _():
        ys_ref[...] = out


def _output_norm(x, y, g, b, bp, seq, tm=LN_TILE):
    m, d = x.shape
    tiles_per_seq, n_prompt_tiles, prompt_map, sample_map = _seq_tile_maps(bp, seq, tm)
    ms = m - n_prompt_tiles * tm
    row = pl.BlockSpec((tm, d), lambda i: (i, 0))
    vec = pl.BlockSpec((1, d), lambda i: (0, 0))
    return pl.pallas_call(
        functools.partial(_ln_out_kernel, tiles_per_seq=tiles_per_seq, n_prompt_tiles=n_prompt_tiles),
        out_shape=(jax.ShapeDtypeStruct((bp, seq, d), F32), jax.ShapeDtypeStruct((ms, d), F32)),
        grid=(m // tm,),
        in_specs=[row, row, vec, vec],
        out_specs=(pl.BlockSpec((None, tm, d), prompt_map), pl.BlockSpec((tm, d), sample_map)),
        compiler_params=_cparams(("arbitrary",)),
        name="output_norm",
    )(x, y, g.reshape(1, d), b.reshape(1, d))


def _mm_kernel(x_ref, w_ref, o_ref, wb_ref, *, w_rows_are_outputs):
    @pl.when(pl.program_id(1) == 0)
    def _():
        if w_rows_are_outputs:
            wb_ref[...] = w_ref[...].reshape(w_ref.shape[-2:]).T.astype(BF16)
        else:
            wb_ref[...] = w_ref[...].astype(BF16)

    o_ref[...] = _dot(x_ref[...], wb_ref[...]).astype(o_ref.dtype)


def _matmul(x, w, lead, col0, n_tiles, tn, name="matmul", w_transposed=False):
    m, k = x.shape
    nl = len(lead)
    if w_transposed:
        w_spec = pl.BlockSpec((pl.Element(1),) * nl + (pl.Element(tn), pl.Element(k)),
                              lambda j, i: tuple(lead) + (pl.multiple_of(j * tn + col0, SUBLANES), 0))
    else:
        w_spec = pl.BlockSpec((None,) * nl + (k, tn), lambda j, i: tuple(lead) + (0, j + col0))
    x_spec = pl.BlockSpec((TM, k), lambda j, i: (i, 0))
    o_spec = pl.BlockSpec((TM, tn), lambda j, i: (i, j))
    return pl.pallas_call(
        functools.partial(_mm_kernel, w_rows_are_outputs=w_transposed),
        out_shape=jax.ShapeDtypeStruct((m, n_tiles * tn), F32),
        grid=(n_tiles, m // TM),
        in_specs=[x_spec, w_spec],
        out_specs=o_spec,
        scratch_shapes=[pltpu.VMEM((k, tn), BF16)],
        compiler_params=_cparams(("arbitrary", "arbitrary")),
        name=name,
    )(x, w)


def _matmul_norm_kernel(a_ref, w_ref, x_ref, g_ref, b_ref, o32_ref, o16_ref, wb_ref):
    @pl.when(pl.program_id(0) == 0)
    def _():
        wb_ref[...] = w_ref[...].astype(BF16)

    y = _ln(ALPHA * x_ref[...] + _dot(a_ref[...], wb_ref[...]), g_ref, b_ref)
    o32_ref[...] = y
    o16_ref[...] = y.astype(BF16)


def _matmul_norm(a, w, lead, x, g, b, tm=LN_TILE):
    m, k = a.shape
    d = x.shape[1]
    nl = len(lead)
    row = pl.BlockSpec((tm, d), lambda i: (i, 0))
    vec = pl.BlockSpec((1, d), lambda i: (0, 0))
    return pl.pallas_call(
        _matmul_norm_kernel,
        out_shape=(jax.ShapeDtypeStruct((m, d), F32), jax.ShapeDtypeStruct((m, d), BF16)),
        grid=(m // tm,),
        in_specs=[pl.BlockSpec((tm, k), lambda i: (i, 0)),
                  pl.BlockSpec((None,) * nl + (k, d), lambda i: tuple(lead) + (0, 0), pipeline_mode=pl.Buffered(1)),
                  row, vec, vec],
        out_specs=(row, row),
        scratch_shapes=[pltpu.VMEM((k, d), BF16)],
        compiler_params=_cparams(("arbitrary",)),
        name="matmul_norm",
    )(a, w, x, g.reshape(1, d), b.reshape(1, d))


def _store_lane_tiles_t(res, out_ref):
    rows, width = res.shape
    for r in range(rows // LANES):
        for c in range(width // LANES):
            tile = res[r * LANES:(r + 1) * LANES, c * LANES:(c + 1) * LANES]
            out_ref[r, c] = tile.T.astype(out_ref.dtype)


def _proj_kernel(x_ref, w_ref, *refs, nat32, nat16, trans, out_scale):
    outs, wb_ref = list(refs[:-1]), refs[-1]

    @pl.when(pl.program_id(0) == 0)
    def _():
        wb_ref[...] = w_ref[...].T.astype(BF16)

    res = _dot(x_ref[...], wb_ref[...])
    if out_scale is not None:
        res = res * out_scale
    if nat32:
        outs.pop(0)[...] = res
    if nat16:
        outs.pop(0)[...] = res.astype(BF16)
    if trans:
        _store_lane_tiles_t(res, outs.pop(0))


def _proj(x, w, lead, col0, *, nat32=False, nat16=False, trans=False, out_scale=None, tm=TM, name="proj"):
    m, k = x.shape
    bw = BRANCH_WIDTH
    nl = len(lead)
    shapes, specs = [], []
    row_spec = pl.BlockSpec((tm, bw), lambda i: (i, 0))
    if nat32:
        shapes.append(jax.ShapeDtypeStruct((m, bw), F32)); specs.append(row_spec)
    if nat16:
        shapes.append(jax.ShapeDtypeStruct((m, bw), BF16)); specs.append(row_spec)
    if trans:
        shapes.append(jax.ShapeDtypeStruct((m // LANES, bw // LANES, LANES, LANES), BF16))
        specs.append(pl.BlockSpec((tm // LANES, bw // LANES, LANES, LANES), lambda i: (i, 0, 0, 0)))
    return pl.pallas_call(
        functools.partial(_proj_kernel, nat32=nat32, nat16=nat16, trans=trans, out_scale=out_scale),
        out_shape=tuple(shapes),
        grid=(m // tm,),
        in_specs=[pl.BlockSpec((tm, k), lambda i: (i, 0)),
                  pl.BlockSpec((None,) * nl + (bw, k), lambda i: tuple(lead) + (col0, 0))],
        out_specs=tuple(specs),
        scratch_shapes=[pltpu.VMEM((k, bw), BF16)],
        compiler_params=_cparams(("arbitrary",)),
        name=name,
    )(x, w)


def _proj_idx_kernel(x_ref, w_ref, nat_ref, ki_ref, wt_ref, wb_ref):
    @pl.when(pl.program_id(0) == 0)
    def _():
        wb_ref[...] = w_ref[...].T.astype(BF16)

    res = _dot(x_ref[...], wb_ref[...])
    nat_ref[...] = res
    ki_ref[...] = res[:, :IDX_DIM].astype(BF16)
    for r in range(res.shape[0] // LANES):
        wt_ref[r] = res[r * LANES:(r + 1) * LANES, :].T[IDX_DIM:IDX_DIM + IDX_HEADS, :]


def _proj_idx(x, w, lead, col_block, tm=TM):
    m, k = x.shape
    nl = len(lead)
    return pl.pallas_call(
        _proj_idx_kernel,
        out_shape=(jax.ShapeDtypeStruct((m, LANES), F32), jax.ShapeDtypeStruct((m, IDX_DIM), BF16),
                   jax.ShapeDtypeStruct((m // LANES, IDX_HEADS, LANES), F32)),
        grid=(m // tm,),
        in_specs=[pl.BlockSpec((tm, k), lambda i: (i, 0)),
                  pl.BlockSpec((None,) * nl + (LANES, k), lambda i: tuple(lead) + (col_block, 0))],
        out_specs=(pl.BlockSpec((tm, LANES), lambda i: (i, 0)), pl.BlockSpec((tm, IDX_DIM), lambda i: (i, 0)),
                   pl.BlockSpec((tm // LANES, IDX_HEADS, LANES), lambda i: (i, 0, 0))),
        scratch_shapes=[pltpu.VMEM((k, LANES), BF16)],
        compiler_params=_cparams(("arbitrary",)),
        name="proj_idx",
    )(x, w)


def _gate_up_kernel(x_ref, wg_ref, wu_ref, o_ref, wgb_ref, wub_ref):
    @pl.when(pl.program_id(1) == 0)
    def _():
        wgb_ref[...] = wg_ref[...].astype(BF16)
        wub_ref[...] = wu_ref[...].astype(BF16)

    x = x_ref[...]
    g = _dot(x, wgb_ref[...])
    u = _dot(x, wub_ref[...])
    o_ref[...] = (g * jax.nn.sigmoid(g) * u).astype(o_ref.dtype)


def _gate_up(x, wg, wu, lead, tn=512):
    m, k = x.shape
    n = wg.shape[-1]
    nl = len(lead)
    w_spec = pl.BlockSpec((None,) * nl + (k, tn), lambda j, i: tuple(lead) + (0, j))
    x_spec = pl.BlockSpec((TM, k), lambda j, i: (i, 0))
    return pl.pallas_call(
        _gate_up_kernel,
        out_shape=jax.ShapeDtypeStruct((m, n), BF16),
        grid=(n // tn, m // TM),
        in_specs=[x_spec, w_spec, w_spec],
        out_specs=pl.BlockSpec((TM, tn), lambda j, i: (i, j)),
        scratch_shapes=[pltpu.VMEM((k, tn), BF16), pltpu.VMEM((k, tn), BF16)],
        compiler_params=_cparams(("arbitrary", "arbitrary")),
        name="gate_up",
    )(x, wg, wu)


def _merge_kernel(x_ref, oa_ref, ob_ref, oc_ref, wg0_ref, wg1_ref, wg2_ref, wb0_ref, wb1_ref, wb2_ref,
                  bg0_ref, bg1_ref, bg2_ref, o_ref, wgb_ref, wbb_ref):
    wg_refs = (wg0_ref, wg1_ref, wg2_ref)
    wb_refs = (wb0_ref, wb1_ref, wb2_ref)

    @pl.when(pl.program_id(1) == 0)
    def _():
        for n in range(N_BRANCH):
            wgb_ref[n] = wg_refs[n][...].astype(BF16)
            wbb_ref[n] = wb_refs[n][...].astype(BF16)

    x = x_ref[...]
    acc = None
    for n, (o_in, bg) in enumerate(zip((oa_ref, ob_ref, oc_ref), (bg0_ref, bg1_ref, bg2_ref))):
        gate = jax.nn.sigmoid(_dot(x, wgb_ref[n]) + bg[...])
        term = gate * _dot(o_in[...], wbb_ref[n])
        acc = term if acc is None else acc + term
    o_ref[...] = acc.astype(o_ref.dtype)


def _merge(l, x, oa, ob, oc, w_branch, w_gate, b_gate, tn=256):
    m, d = x.shape
    w = oa.shape[1]
    nt = d // tn
    x_spec = pl.BlockSpec((TM, d), lambda j, i: (i, 0))
    o_in_spec = pl.BlockSpec((TM, w), lambda j, i: (i, 0))
    wg_specs = [pl.BlockSpec((None, d, tn), lambda j, i, n=n: (l, 0, n * nt + j)) for n in range(N_BRANCH)]
    wb_specs = [pl.BlockSpec((None, None, w, tn), lambda j, i, n=n: (l, n, 0, j)) for n in range(N_BRANCH)]
    bg_specs = [pl.BlockSpec((None, 1, tn), lambda j, i, n=n: (l, 0, n * nt + j)) for n in range(N_BRANCH)]
    return pl.pallas_call(
        _merge_kernel,
        out_shape=jax.ShapeDtypeStruct((m, d), BF16),
        grid=(nt, m // TM),
        in_specs=[x_spec, o_in_spec, o_in_spec, o_in_spec] + wg_specs + wb_specs + bg_specs,
        out_specs=pl.BlockSpec((TM, tn), lambda j, i: (i, j)),
        scratch_shapes=[pltpu.VMEM((N_BRANCH, d, tn), BF16), pltpu.VMEM((N_BRANCH, w, tn), BF16)],
        compiler_params=_cparams(("arbitrary", "arbitrary")),
        name="merge",
    )(x, oa, ob, oc, w_gate, w_gate, w_gate, w_branch, w_branch, w_branch,
      b_gate.reshape(DEPTH, 1, N_BRANCH * d), b_gate.reshape(DEPTH, 1, N_BRANCH * d),
      b_gate.reshape(DEPTH, 1, N_BRANCH * d))


def _split_bf16(a):
    hi = a.astype(BF16)
    lo = (a - hi.astype(F32)).astype(BF16)
    return hi, lo


def _router_kernel(x_ref, w_ref, b_ref, gate_ref, expert_ref):
    xh, xl = _split_bf16(x_ref[...])
    wh, wl = _split_bf16(w_ref[...])
    logits = _dot(xh, wh) + (_dot(xh, wl) + _dot(xl, wh)) + b_ref[...]
    lane = lax.broadcasted_iota(jnp.int32, logits.shape, 1)
    logits = jnp.where(lane < N_EXPERTS, logits, -jnp.inf)
    m1 = jnp.max(logits, axis=-1, keepdims=True)
    i1 = jnp.min(jnp.where(logits == m1, lane, LANES), axis=-1, keepdims=True)
    rest = jnp.where(lane == i1, -jnp.inf, logits)
    m2 = jnp.max(rest, axis=-1, keepdims=True)
    i2 = jnp.min(jnp.where(rest == m2, lane, LANES), axis=-1, keepdims=True)
    e = jnp.exp(m2 - m1)
    p1 = 1.0 / (1.0 + e)
    p2 = e / (1.0 + e)
    gate_ref[...] = jnp.where(lane == 0, p1, jnp.where(lane == 1, p2, 0.0))
    expert_ref[...] = jnp.where(lane == 0, i1, jnp.where(lane == 1, i2, 0))


def _router(x32, w_r, b_r, tm=256):
    m, d = x32.shape
    w_pad = jnp.pad(w_r, ((0, 0), (0, LANES - N_EXPERTS)))
    b_pad = jnp.pad(b_r, (0, LANES - N_EXPERTS)).reshape(1, LANES)
    out_spec = pl.BlockSpec((tm, LANES), lambda i: (i, 0))
    gates, experts = pl.pallas_call(
        _router_kernel,
        out_shape=(jax.ShapeDtypeStruct((m, LANES), F32), jax.ShapeDtypeStruct((m, LANES), jnp.int32)),
        grid=(m // tm,),
        in_specs=[pl.BlockSpec((tm, d), lambda i: (i, 0)), pl.BlockSpec((d, LANES), lambda i: (0, 0)),
                  pl.BlockSpec((1, LANES), lambda i: (0, 0))],
        out_specs=(out_spec, out_spec),
        compiler_params=_cparams(("parallel",)),
        name="router",
    )(x32, w_pad, b_pad)
    return gates[:, :TOP_K], experts[:, :TOP_K]


def _dispatch_plan(experts, tile, token_rows):
    m = experts.shape[0]
    n_assign = TOP_K * m
    n_tiles = n_assign // tile + N_EXPERTS
    e = experts.reshape(n_assign)
    live = jnp.repeat(token_rows, TOP_K)
    onehot = jnp.logical_and(e[:, None] == jnp.arange(N_EXPERTS, dtype=jnp.int32)[None, :],
                             live[:, None]).astype(jnp.int32)
    rank = jnp.sum((jnp.cumsum(onehot, axis=0) - onehot) * onehot, axis=1)
    tiles_per = (jnp.sum(onehot, axis=0) + tile - 1) // tile
    tile_end = jnp.cumsum(tiles_per)
    pos = ((tile_end - tiles_per)[e] * tile + rank).astype(jnp.int32)
    slot = jnp.where(live, pos, n_tiles * tile)
    src = jnp.zeros((n_tiles * tile,), jnp.int32).at[slot].set(jnp.arange(n_assign, dtype=jnp.int32) // TOP_K,
                                                               mode="drop")
    pos = jnp.where(live, pos, 0)
    n_used = tile_end[-1:]
    t_idx = jnp.minimum(jnp.arange(n_tiles, dtype=jnp.int32), n_used - 1)
    tile_expert = jnp.sum((t_idx[:, None] >= tile_end[None, :]).astype(jnp.int32), axis=1)
    return src, pos, tile_expert.astype(jnp.int32), n_used.astype(jnp.int32)


def _issue_rows(idx_ref, first, count, stride, src_hbm, dst, sem):
    def body(i, carry):
        pltpu.make_async_copy(src_hbm.at[pl.ds(idx_ref[first + i * stride], 1), :], dst.at[pl.ds(i, 1), :],
                              sem).start()
        return carry

    lax.fori_loop(0, count, body, 0, unroll=8)


def _wait_rows(count, src_hbm, dst, sem):
    pltpu.make_async_copy(src_hbm.at[pl.ds(0, count), :], dst, sem).wait()


def _gather_kernel(src_ref, nu_ref, x_hbm, o_ref, buf, sem, *, tile):
    t = pl.program_id(0)

    @pl.when(t == 0)
    def _():
        _issue_rows(src_ref, 0, tile, 1, x_hbm, buf.at[0], sem.at[0])

    @pl.when(t + 1 < nu_ref[0])
    def _():
        nxt = (t + 1) % 2
        _issue_rows(src_ref, (t + 1) * tile, tile, 1, x_hbm, buf.at[nxt], sem.at[nxt])

    @pl.when(t < nu_ref[0])
    def _():
        cur = t % 2
        _wait_rows(tile, x_hbm, buf.at[cur], sem.at[cur])
        o_ref[...] = buf[cur].astype(o_ref.dtype)

    @pl.when(t >= nu_ref[0])
    def _():
        o_ref[...] = jnp.zeros(o_ref.shape, o_ref.dtype)


def _gather_rows(x32, src, n_used, tile):
    d = x32.shape[1]
    n_rows = src.shape[0]
    return pl.pallas_call(
        functools.partial(_gather_kernel, tile=tile),
        out_shape=jax.ShapeDtypeStruct((n_rows, d), BF16),
        grid_spec=pltpu.PrefetchScalarGridSpec(
            num_scalar_prefetch=2, grid=(n_rows // tile,),
            in_specs=[pl.BlockSpec(memory_space=pl.ANY)],
            out_specs=pl.BlockSpec((tile, d), lambda t, src, nu: (t, 0)),
            scratch_shapes=[pltpu.VMEM((2, tile, d), F32), pltpu.SemaphoreType.DMA((2,))]),
        compiler_params=_cparams(("arbitrary",)),
        name="moe_gather",
    )(src, n_used, x32)


def _combine_kernel(pos_ref, y_hbm, g_ref, o_ref, buf, sem, *, tile):
    t = pl.program_id(0)

    def issue(tile_idx, slot):
        for s in range(TOP_K):
            _issue_rows(pos_ref, tile_idx * tile * TOP_K + s, tile, TOP_K, y_hbm, buf.at[slot, s], sem.at[slot])

    @pl.when(t == 0)
    def _():
        issue(0, 0)

    @pl.when(t + 1 < pl.num_programs(0))
    def _():
        issue(t + 1, (t + 1) % 2)

    cur = t % 2
    for s in range(TOP_K):
        _wait_rows(tile, y_hbm, buf.at[cur, s], sem.at[cur])
    gates = g_ref[...]
    total = buf[cur, 0] * gates[:, 0:1]
    for s in range(1, TOP_K):
        total = total + buf[cur, s] * gates[:, s:s + 1]
    o_ref[...] = total


def _combine_rows(y_sorted, pos, gates, tile):
    m = gates.shape[0]
    d = y_sorted.shape[1]
    return pl.pallas_call(
        functools.partial(_combine_kernel, tile=tile),
        out_shape=jax.ShapeDtypeStruct((m, d), F32),
        grid_spec=pltpu.PrefetchScalarGridSpec(
            num_scalar_prefetch=1, grid=(m // tile,),
            in_specs=[pl.BlockSpec(memory_space=pl.ANY), pl.BlockSpec((tile, TOP_K), lambda t, pos: (t, 0))],
            out_specs=pl.BlockSpec((tile, d), lambda t, pos: (t, 0)),
            scratch_shapes=[pltpu.VMEM((2, TOP_K, tile, d), F32), pltpu.SemaphoreType.DMA((2,))]),
        compiler_params=_cparams(("arbitrary",)),
        name="moe_combine",
    )(pos, y_sorted, gates)


def _expert_changed(te_ref, t):
    return jnp.logical_or(t == 0, te_ref[t] != te_ref[jnp.maximum(t - 1, 0)])


def _moe_gate_up_kernel(te_ref, nu_ref, x_ref, wg_ref, wu_ref, o_ref, wgb_ref, wub_ref):
    t = pl.program_id(1)

    @pl.when(_expert_changed(te_ref, t))
    def _():
        wgb_ref[...] = wg_ref[...].astype(BF16)
        wub_ref[...] = wu_ref[...].astype(BF16)

    @pl.when(t < nu_ref[0])
    def _():
        x = x_ref[...]
        g = _dot(x, wgb_ref[...])
        u = _dot(x, wub_ref[...])
        o_ref[...] = (g * jax.nn.sigmoid(g) * u).astype(o_ref.dtype)

    @pl.when(t >= nu_ref[0])
    def _():
        o_ref[...] = jnp.zeros(o_ref.shape, o_ref.dtype)


def _moe_down_kernel(te_ref, nu_ref, h_ref, w_ref, o_ref, wb_ref):
    t = pl.program_id(1)

    @pl.when(_expert_changed(te_ref, t))
    def _():
        wb_ref[...] = w_ref[...].astype(BF16)

    @pl.when(t < nu_ref[0])
    def _():
        o_ref[...] = _dot(h_ref[...], wb_ref[...])

    @pl.when(t >= nu_ref[0])
    def _():
        o_ref[...] = jnp.zeros(o_ref.shape, o_ref.dtype)


def _moe_experts(x_sorted, tile_expert, n_used, w_gate, w_up, w_down, layer, tile, tn=512):
    r, d = x_sorted.shape
    ff = w_gate.shape[-1]
    n_tiles = r // tile

    def x_rows(width):
        return pl.BlockSpec((tile, width), lambda j, t, te, nu: (jnp.minimum(t, nu[0] - 1), 0))

    def w_cols(k):
        return pl.BlockSpec((None, None, k, tn), lambda j, t, te, nu: (layer, te[t], 0, j))

    h = pl.pallas_call(
        _moe_gate_up_kernel,
        out_shape=jax.ShapeDtypeStruct((r, ff), BF16),
        grid_spec=pltpu.PrefetchScalarGridSpec(
            num_scalar_prefetch=2, grid=(ff // tn, n_tiles),
            in_specs=[x_rows(d), w_cols(d), w_cols(d)],
            out_specs=pl.BlockSpec((tile, tn), lambda j, t, te, nu: (t, j)),
            scratch_shapes=[pltpu.VMEM((d, tn), BF16), pltpu.VMEM((d, tn), BF16)]),
        compiler_params=_cparams(("arbitrary", "arbitrary")),
        name="moe_gate_up",
    )(tile_expert, n_used, x_sorted, w_gate, w_up)
    return pl.pallas_call(
        _moe_down_kernel,
        out_shape=jax.ShapeDtypeStruct((r, d), F32),
        grid_spec=pltpu.PrefetchScalarGridSpec(
            num_scalar_prefetch=2, grid=(d // tn, n_tiles),
            in_specs=[x_rows(ff), w_cols(ff)],
            out_specs=pl.BlockSpec((tile, tn), lambda j, t, te, nu: (t, j)),
            scratch_shapes=[pltpu.VMEM((ff, tn), BF16)]),
        compiler_params=_cparams(("arbitrary", "arbitrary")),
        name="moe_down",
    )(tile_expert, n_used, h, w_down)


def _pool_kernel(u_ref, halo_ref, w_ref, sc_ref, _buf_ref, o_ref, ext_ref, *, tb, first_valid):
    row0 = pl.program_id(1) * tb
    r_cur = row0 + lax.broadcasted_iota(jnp.int32, (tb, 1), 0)
    r_halo = row0 - HALO + lax.broadcasted_iota(jnp.int32, (HALO, 1), 0)
    u = jnp.where(r_cur >= first_valid, u_ref[...], 0.0)
    ext_ref[0:HALO, :] = jnp.where(r_halo >= first_valid, halo_ref[...], 0.0)
    ext_ref[HALO:, :] = u
    seen = (r_cur - first_valid + 1).astype(F32)
    for g, win in enumerate(POOL_WINDOWS):
        c0, c1 = g * POOL_GROUP, (g + 1) * POOL_GROUP
        s = u[:, c0:c1]
        for back in range(1, win):
            s = s + ext_ref[HALO - back:HALO - back + tb, c0:c1]
        cnt = jnp.clip(seen, 1.0, float(win))
        diff = s / cnt - u[:, c0:c1]
        y = _dot(diff.astype(BF16), w_ref[g].astype(BF16))
        o_ref[:, c0:c1] = (y * sc_ref[:, c0:c1]).astype(o_ref.dtype)


def _pool(u2d, col_block, halo2d, halo_col_block, pool_w_l, pool_scale_l, out_buf, *, batch, rows, tb, row0,
          halo_map, first_valid):
    nb = rows // tb
    base = row0 // tb
    return pl.pallas_call(
        functools.partial(_pool_kernel, tb=tb, first_valid=first_valid),
        out_shape=jax.ShapeDtypeStruct(out_buf.shape, out_buf.dtype),
        grid=(batch, nb),
        in_specs=[
            pl.BlockSpec((tb, BRANCH_WIDTH), lambda b, i: (base + b * nb + i, col_block)),
            pl.BlockSpec((HALO, BRANCH_WIDTH), lambda b, i: (halo_map(b, i), halo_col_block)),
            pl.BlockSpec((len(POOL_WINDOWS), POOL_GROUP, POOL_GROUP), lambda b, i: (0, 0, 0)),
            pl.BlockSpec((1, BRANCH_WIDTH), lambda b, i: (0, 0)),
            pl.BlockSpec(memory_space=pl.ANY),
        ],
        out_specs=pl.BlockSpec((tb, BRANCH_WIDTH), lambda b, i: (base + b * nb + i, 0)),
        scratch_shapes=[pltpu.VMEM((HALO + tb, BRANCH_WIDTH), F32)],
        input_output_aliases={4: 0},
        compiler_params=_cparams(("parallel", "arbitrary")),
        name="pool",
    )(u2d, halo2d, pool_w_l, pool_scale_l.reshape(1, BRANCH_WIDTH), out_buf)


def _ret_kernel(q_ref, k_ref, v_ref, g_ref, cos_ref, sin_ref, dmat_ref, cross_ref, kdec_ref, gn_ref, s0_ref,
                _buf_ref, o_ref, s_out_ref, s_scr, *, first_valid):
    i = pl.program_id(1)

    @pl.when(i == 0)
    def _():
        s_scr[...] = s0_ref[...]

    rows = i * CHUNK + lax.broadcasted_iota(jnp.int32, (CHUNK, 1), 0)
    valid = rows >= first_valid
    cos = cos_ref[...]
    sin = sin_ref[...]
    rscale = RET_HEAD_DIM ** -0.5
    half = RET_HEAD_DIM // 2
    heads = range(RET_HEADS)
    sls = [slice(h * RET_HEAD_DIM, (h + 1) * RET_HEAD_DIM) for h in heads]

    def rotate(x):
        return x * cos + pltpu.roll(x, half, 1) * sin

    qr = [rotate(q_ref[:, sl]).astype(BF16) for sl in sls]
    kr = [jnp.where(valid, rotate(k_ref[:, sl]) * rscale, 0.0) for sl in sls]
    v = [jnp.where(valid, v_ref[:, sl], 0.0).astype(BF16) for sl in sls]
    state = [s_scr[h] for h in heads]
    inner = [_dot_nt(qr[h], kr[h].astype(BF16)) * dmat_ref[h] for h in heads]
    carried = [_dot(qr[h], state[h].astype(BF16)) * cross_ref[h] for h in heads]
    update = [_dot_tn((kr[h] * kdec_ref[h]).astype(BF16), v[h]) for h in heads]
    o = [_dot(inner[h].astype(BF16), v[h]) + carried[h] for h in heads]
    for h in heads:
        s_scr[h] = gn_ref[h] * state[h] + update[h]
    for h in heads:
        mu = jnp.mean(o[h], axis=-1, keepdims=True)
        oc = o[h] - mu
        var = jnp.mean(oc * oc, axis=-1, keepdims=True)
        gate = g_ref[:, sls[h]]
        o_ref[:, sls[h]] = (gate * jax.nn.sigmoid(gate) * (oc * lax.rsqrt(var + LN_EPS))).astype(o_ref.dtype)

    @pl.when(i == pl.num_programs(1) - 1)
    def _():
        s_out_ref[...] = s_scr[...]


def _retention(p3, s0, cos, sin, tabs, out_buf, *, batch, rows, row0, first_valid):
    nb = rows // CHUNK
    base = row0 // CHUNK
    dmat, cross, kdec, gn = tabs

    def col(c):
        return pl.BlockSpec((CHUNK, BRANCH_WIDTH), lambda b, i: (base + b * nb + i, c))

    tab_rows = pl.BlockSpec((CHUNK, RET_HEAD_DIM), lambda b, i: (i, 0))

    def full(a):
        return pl.BlockSpec(a.shape, lambda b, i: (0,) * a.ndim)

    state_spec = pl.BlockSpec((None, RET_HEADS, RET_HEAD_DIM, RET_HEAD_DIM), lambda b, i: (b, 0, 0, 0))
    return pl.pallas_call(
        functools.partial(_ret_kernel, first_valid=first_valid),
        out_shape=(jax.ShapeDtypeStruct(out_buf.shape, out_buf.dtype),
                   jax.ShapeDtypeStruct((batch, RET_HEADS, RET_HEAD_DIM, RET_HEAD_DIM), F32)),
        grid=(batch, nb),
        in_specs=[col(1), col(2), col(3), col(4), tab_rows, tab_rows, full(dmat), full(cross), full(kdec),
                  full(gn), state_spec, pl.BlockSpec(memory_space=pl.ANY)],
        out_specs=(pl.BlockSpec((CHUNK, BRANCH_WIDTH), lambda b, i: (base + b * nb + i, 0)), state_spec),
        scratch_shapes=[pltpu.VMEM((RET_HEADS, RET_HEAD_DIM, RET_HEAD_DIM), F32)],
        input_output_aliases={11: 0},
        compiler_params=_cparams(("parallel", "arbitrary")),
        name="retention",
    )(p3, p3, p3, p3, cos, sin, dmat, cross, kdec, gn, s0, out_buf)


def _sortable(x):
    bits = pltpu.bitcast(x + 0.0, jnp.int32)
    return bits ^ ((bits >> 31) & 0x7FFFFFFF)


def _dsa_kernel(q_ref, qi_ref, w_ref, k_ref, vt_ref, ki_ref, bn_ref, ok_ref, _buf_ref, o_ref,
                key_ref, keyn_ref, m_ref, den_ref, acc_ref, *, first_real, qb0, skip_below, k_sel):
    j = pl.program_id(1)

    @pl.when(j < skip_below)
    def _():
        o_ref[...] = jnp.zeros(o_ref.shape, o_ref.dtype)

    @pl.when(j >= skip_below)
    def _():
        win0 = pl.multiple_of(j * QT + (qb0 * CHUNK - (WN - QT)), LANES)
        n_far = (win0 + KEY_TILE - 1) // KEY_TILE
        near = pl.ds(win0, WN)

        def index_keys(ki, adm):
            acc = None
            for h in range(0, IDX_HEADS, 2):
                pair = _dot(ki, jnp.concatenate([qi_ref[0, h], qi_ref[0, h + 1]], axis=1))
                term = (w_ref[0, h:h + 1, :] * jnp.maximum(pair[:, :QT], 0.0)
                        + w_ref[0, h + 1:h + 2, :] * jnp.maximum(pair[:, QT:], 0.0))
                acc = term if acc is None else acc + term
            return jnp.where(adm, _sortable(acc), INT_MIN)

        def far_keys(c, carry):
            c0 = pl.multiple_of(c * KEY_TILE, KEY_TILE)
            row = c0 + lax.broadcasted_iota(jnp.int32, (KEY_TILE, 1), 0)
            adm = jnp.logical_and(row >= first_real, row < win0)
            key_ref[c] = index_keys(ki_ref[pl.ds(c0, KEY_TILE), :], adm)
            return carry

        lax.fori_loop(0, n_far, far_keys, 0)
        row_n = win0 + lax.broadcasted_iota(jnp.int32, (WN, 1), 0)
        keyn_ref[...] = index_keys(ki_ref[near, :], jnp.logical_and(row_n >= first_real, ok_ref[...] != 0))

        def fold(hit):
            parts = hit.reshape(hit.shape[0] // SUBLANES, SUBLANES, QT)
            while parts.shape[0] > 1:
                half = parts.reshape(parts.shape[0] // 2, 2, SUBLANES, QT)
                parts = half[:, 0] + half[:, 1]
            return parts[0]

        def count_ge(c):
            part = lax.fori_loop(0, n_far, lambda t, p: p + fold(jnp.where(key_ref[t] >= c, 1.0, 0.0)),
                                 fold(jnp.where(keyn_ref[...] >= c, 1.0, 0.0)))
            return jnp.sum(part, axis=0, keepdims=True)

        zero = jnp.zeros((1, QT), jnp.int32)
        n_zero = count_ge(zero)
        start = (jnp.where(n_zero >= k_sel, zero, INT_MIN), jnp.where(n_zero >= k_sel, n_zero, 0.0))

        def bit_step(it, carry):
            thr, n_at = carry
            cand = thr + jnp.left_shift(jnp.int32(1), jnp.int32(30) - it)
            n_cand = count_ge(cand)
            return jnp.where(n_cand >= k_sel, cand, thr), jnp.where(n_cand >= k_sel, n_cand, n_at)

        thr, n_at = lax.fori_loop(0, 31, bit_step, start)
        thr = jnp.maximum(thr, INT_MIN + 1)

        surplus = jnp.maximum(n_at - k_sel, 0.0)

        @pl.when(jnp.max(surplus) > 0.0)
        def _():
            ties_kept = n_at - count_ge(thr + 1) - surplus
            lower = (lax.broadcasted_iota(jnp.int32, (KEY_TILE, KEY_TILE), 0)
                     >= lax.broadcasted_iota(jnp.int32, (KEY_TILE, KEY_TILE), 1)).astype(BF16)

            def drop_late_ties(key, seen):
                tie = key == thr
                place = seen + _dot(lower[:key.shape[0], :key.shape[0]], jnp.where(tie, 1.0, 0.0).astype(BF16))
                key = jnp.where(jnp.logical_and(tie, place > ties_kept), thr - 1, key)
                return key, place[key.shape[0] - 1:, :]

            def far_ties(t, seen):
                key_ref[t], seen = drop_late_ties(key_ref[t], seen)
                return seen

            seen = lax.fori_loop(0, n_far, far_ties, jnp.zeros((1, QT), F32))
            keyn_ref[...], _ = drop_late_ties(keyn_ref[...], seen)

        m_ref[...] = jnp.full(m_ref.shape, M_INIT, F32)
        den_ref[...] = jnp.zeros(den_ref.shape, F32)
        acc_ref[...] = jnp.zeros(acc_ref.shape, F32)

        heads = range(A_HEADS)
        sls = [slice(h * A_HEAD_DIM, (h + 1) * A_HEAD_DIM) for h in heads]

        def attend(logits, sel, first_vt_tile):
            n_sub = logits[0].shape[0] // LANES
            lg = [jnp.where(sel, logits[h], NEG) for h in heads]
            m_old = [m_ref[h] for h in heads]
            m_new = [jnp.maximum(m_old[h], jnp.max(lg[h], axis=0, keepdims=True)) for h in heads]
            p = [jnp.exp2(lg[h] - m_new[h]) for h in heads]
            alpha = [jnp.exp2(m_old[h] - m_new[h]) for h in heads]
            pb = [p[h].astype(BF16) for h in heads]
            pv = []
            for h in heads:
                terms = [_dot(vt_ref[first_vt_tile + u, sls[h], :], pb[h][u * LANES:(u + 1) * LANES])
                         for u in range(n_sub)]
                pv.append(functools.reduce(lambda a, b: a + b, terms))
            for h in heads:
                den_ref[h] = alpha[h] * den_ref[h] + jnp.sum(p[h], axis=0, keepdims=True)
                acc_ref[h] = alpha[h] * acc_ref[h] + pv[h]
                m_ref[h] = m_new[h]

        def far_tile(c, carry):
            c0 = pl.multiple_of(c * KEY_TILE, KEY_TILE)
            attend([_dot(k_ref[pl.ds(c0, KEY_TILE), sls[h]], q_ref[0, h]) for h in heads],
                   key_ref[c] >= thr, c * (KEY_TILE // LANES))
            return carry

        lax.fori_loop(0, n_far, far_tile, 0)
        attend([_dot(k_ref[near, sls[h]], q_ref[0, h]) + bn_ref[h] for h in heads],
               keyn_ref[...] >= thr, win0 // LANES)
        for h in heads:
            out_t = acc_ref[h] / jnp.maximum(den_ref[h], TINY)
            o_ref[:, sls[h]] = out_t.T[:o_ref.shape[0]].astype(o_ref.dtype)


def _dsa(q_t, qi_t, w_t, k16, v_t, ki16, bias_near, near_ok, out_buf, *, batch, nb, n_keys, out_rows, out_base,
         first_real, qb0, skip_below, k_sel):
    width = k16.shape[1]
    return pl.pallas_call(
        functools.partial(_dsa_kernel, first_real=first_real, qb0=qb0, skip_below=skip_below, k_sel=k_sel),
        out_shape=jax.ShapeDtypeStruct(out_buf.shape, out_buf.dtype),
        grid=(batch, nb),
        in_specs=[
            pl.BlockSpec((1, A_HEADS, A_HEAD_DIM, QT), lambda b, j: (b * nb + j, 0, 0, 0)),
            pl.BlockSpec((1, IDX_HEADS, IDX_DIM, QT), lambda b, j: (b * nb + j, 0, 0, 0)),
            pl.BlockSpec((1, IDX_HEADS, QT), lambda b, j: (b * nb + j, 0, 0)),
            pl.BlockSpec((n_keys, width), lambda b, j: (b, 0)),
            pl.BlockSpec((n_keys // LANES, width, LANES), lambda b, j: (b, 0, 0)),
            pl.BlockSpec((n_keys, IDX_DIM), lambda b, j: (b, 0)),
            pl.BlockSpec((A_HEADS, WN, QT), lambda b, j: (0, 0, 0)),
            pl.BlockSpec((WN, QT), lambda b, j: (0, 0)),
            pl.BlockSpec(memory_space=pl.ANY),
        ],
        out_specs=pl.BlockSpec((out_rows, width), lambda b, j: (out_base + b * nb + j, 0)),
        scratch_shapes=[pltpu.VMEM((n_keys // KEY_TILE, KEY_TILE, QT), jnp.int32),
                        pltpu.VMEM((WN, QT), jnp.int32),
                        pltpu.VMEM((A_HEADS, 1, QT), F32),
                        pltpu.VMEM((A_HEADS, 1, QT), F32),
                        pltpu.VMEM((A_HEADS, A_HEAD_DIM, QT), F32)],
        input_output_aliases={8: 0},
        compiler_params=_cparams(("parallel", "arbitrary")),
        name="dsa",
    )(q_t, qi_t, w_t, k16, v_t, ki16, bias_near, near_ok, out_buf)


def _sample_keys_kernel(ck_ref, cv_ref, cki_ref, nk_ref, nv_ref, nki_ref, k_ref, vt_ref, ki_ref):
    c = pl.program_id(1)
    n_cache = pl.num_programs(1) - 1
    rows = k_ref.shape[0]

    @pl.when(c < n_cache)
    def _():
        for h in range(A_HEADS):
            sl = slice(h * A_HEAD_DIM, (h + 1) * A_HEAD_DIM)
            head_rows = pl.ds(h, rows, stride=A_HEADS)
            k_ref[:, sl] = ck_ref[0, head_rows, :].astype(BF16)
            vh = cv_ref[0, head_rows, :]
            for u in range(rows // LANES):
                vt_ref[u, sl, :] = vh[u * LANES:(u + 1) * LANES, :].T.astype(BF16)
        ki_ref[...] = cki_ref[0].astype(BF16)

    @pl.when(c == n_cache)
    def _():
        k_ref[...] = jnp.zeros(k_ref.shape, BF16)
        vt_ref[...] = jnp.zeros(vt_ref.shape, BF16)
        ki_ref[...] = jnp.zeros(ki_ref.shape, BF16)
        k_ref[0:CHUNK, :] = nk_ref[...].astype(BF16)
        ki_ref[0:CHUNK, :] = nki_ref[:, :IDX_DIM].astype(BF16)
        pad = jnp.zeros((LANES - CHUNK, A_HEAD_DIM), F32)
        for h in range(A_HEADS):
            sl = slice(h * A_HEAD_DIM, (h + 1) * A_HEAD_DIM)
            vt_ref[0, sl, :] = jnp.concatenate([nv_ref[:, sl], pad], axis=0).T.astype(BF16)


def _sample_keys(l, cache_k, cache_v, cache_ki, k_new, v_new, ki_new, *, new_row0, n_keys, rows=512):
    _, bs, past = cache_k.shape[:3]
    width = k_new.shape[1]
    n_cache = past // rows
    nb = -(-n_keys // rows)
    new_blk = new_row0 // CHUNK

    def cached(block_rows, w):
        return pl.BlockSpec((None, 1, block_rows, w), lambda b, c: (l, b, jnp.minimum(c, n_cache - 1), 0))

    def new(w):
        return pl.BlockSpec((CHUNK, w), lambda b, c: (new_blk + b, 0))

    kv_shape = cache_k.shape[:2] + (past * A_HEADS, A_HEAD_DIM)

    k16, v_t, ki16 = pl.pallas_call(
        _sample_keys_kernel,
        out_shape=(jax.ShapeDtypeStruct((bs, n_keys, width), BF16),
                   jax.ShapeDtypeStruct((bs, n_keys // LANES, width, LANES), BF16),
                   jax.ShapeDtypeStruct((bs, n_keys, IDX_DIM), BF16)),
        grid=(bs, nb),
        in_specs=[cached(rows * A_HEADS, A_HEAD_DIM), cached(rows * A_HEADS, A_HEAD_DIM), cached(rows, IDX_DIM),
                  new(width), new(width), new(LANES)],
        out_specs=(pl.BlockSpec((None, rows, width), lambda b, c: (b, c, 0)),
                   pl.BlockSpec((None, rows // LANES, width, LANES), lambda b, c: (b, c, 0, 0)),
                   pl.BlockSpec((None, rows, IDX_DIM), lambda b, c: (b, c, 0))),
        compiler_params=_cparams(("parallel", "arbitrary")),
        name="sample_keys",
    )(cache_k.reshape(kv_shape), cache_v.reshape(kv_shape), cache_ki, k_new, v_new, ki_new)
    return (k16.reshape(bs * n_keys, width), v_t.reshape(bs * (n_keys // LANES), width, LANES),
            ki16.reshape(bs * n_keys, IDX_DIM))


def _t5_bucket(rel):
    half = T5_BUCKETS // 2
    exact = half // 2
    n = jnp.abs(rel)
    large = exact + (jnp.log(jnp.maximum(n, 1).astype(F32) / exact)
                     / math.log(T5_MAX_DIST / exact) * (half - exact)).astype(jnp.int32)
    large = jnp.minimum(large, half - 1)
    return jnp.where(rel > 0, half, 0) + jnp.where(n < exact, n, large)


def _bias_tables(t5_bias):
    a = jnp.arange(WN, dtype=jnp.int32)[:, None]
    t = jnp.arange(QT, dtype=jnp.int32)[None, :]
    bucket = _t5_bucket(a - (WN - QT) - t)
    onehot = (bucket[:, :, None] == jnp.arange(T5_BUCKETS, dtype=jnp.int32)).astype(F32)
    near = jnp.einsum("atk,kh->hat", onehot, t5_bias.astype(F32), precision=lax.Precision.HIGHEST)
    far = t5_bias[_t5_bucket(jnp.int32(-2 * CHUNK - 1))].astype(F32)
    ok = (a // CHUNK - (WN - QT) // CHUNK <= t // CHUNK).astype(jnp.int32)
    return (near - far[:, None, None]) * math.log2(math.e), ok


def _rope_tables(pos):
    half = RET_HEAD_DIM // 2
    inv = ROPE_BASE ** (-jnp.arange(half, dtype=F32) / half)
    ang = pos.astype(F32)[:, None] * inv[None, :]
    cos, sin = jnp.cos(ang), jnp.sin(ang)
    return jnp.concatenate([cos, cos], axis=-1), jnp.concatenate([-sin, sin], axis=-1)


def _decay_tables():
    n = CHUNK
    log_g = jnp.log(1.0 - 2.0 ** (-5.0 - jnp.arange(RET_HEADS, dtype=F32)))
    i = jnp.arange(n, dtype=F32)
    diff = i[:, None] - i[None, :]
    dmat = jnp.where(diff >= 0, jnp.exp(jnp.maximum(diff, 0.0)[None] * log_g[:, None, None]), 0.0)
    cross = jnp.exp((i[None, :] + 1.0) * log_g[:, None])
    kdec = jnp.exp((n - 1.0 - i)[None, :] * log_g[:, None])
    gn = jnp.exp(n * log_g)
    wide = (RET_HEADS, n, RET_HEAD_DIM)
    return (dmat, jnp.broadcast_to(cross[:, :, None], wide), jnp.broadcast_to(kdec[:, :, None], wide),
            jnp.broadcast_to(gn[:, None, None], (RET_HEADS, 1, RET_HEAD_DIM)))


def kernel(x_prompt, x_sample, cache_k, cache_v, cache_ki, cache_pool, state_ret, meta_tokens, ln_in_g, ln_in_b, w_in, t5_bias, pool_w, pool_scale, w_branch, w_gate, b_gate, w_out, ln1_g, ln1_b, ln2_g, ln2_b, ffn_w_gate, ffn_w_up, ffn_w_down, moe_w_router, moe_b_router, moe_w_gate, moe_w_up, moe_w_down):
    bp, seq, d = x_prompt.shape
    bs, ts, _ = x_sample.shape
    past = cache_k.shape[2]
    t_real = seq + N_META
    tp = -(-(t_real + 2 * CHUNK) // KEY_TILE) * KEY_TILE
    front = tp - t_real
    assert front % CHUNK == CHUNK - N_META and ts == CHUNK and past % CHUNK == 0
    mp, ms = bp * tp, bs * ts
    m = mp + ms
    assert mp % TM == 0 and ms % TM == 0
    ksel_p = min(TOPK_MAX, seq // 4)
    ksel_s = min(TOPK_MAX, (past + ts) // 4)
    n_keys_s = -(-(past + QT) // KEY_TILE) * KEY_TILE
    assert tp % QT == 0 and front >= WN - QT and past % QT == 0 and past + QT >= WN

    assert front + N_META == LN_TILE and seq % LN_TILE == 0 and ms % LN_TILE == 0
    head = jnp.concatenate([jnp.zeros((front, d), F32), meta_tokens.astype(F32)], axis=0)
    x32, xb = _input_norm(head, x_prompt, x_sample.reshape(ms, d), ln_in_g, ln_in_b)

    bias_near, near_ok = _bias_tables(t5_bias)
    cos_p, sin_p = _rope_tables(jnp.arange(tp, dtype=jnp.int32) - (front + N_META))
    cos_s, sin_s = _rope_tables(past + jnp.arange(ts, dtype=jnp.int32))
    decay = _decay_tables()
    zero_state = jnp.zeros((bp, RET_HEADS, RET_HEAD_DIM, RET_HEAD_DIM), F32)
    w_in_t = jnp.swapaxes(w_in, 1, 2)
    assert TAIL_OFF % SUBLANES == 0
    row_in_seq = jnp.arange(mp, dtype=jnp.int32) % tp
    token_rows = jnp.concatenate([row_in_seq >= front, jnp.ones((ms,), jnp.bool_)])

    outs = {name: [] for name in ("kp", "vp", "kip", "poolp", "retp", "ks", "vs", "kis", "pools", "rets")}
    bw = BRANCH_WIDTH
    for l in range(DEPTH):
        q16, q_t = _proj(xb, w_in_t, (l,), 0, nat16=True, trans=True, out_scale=ATTN_LOG2_SCALE, name="proj_q")
        k_new, k16 = _proj(xb, w_in_t, (l,), 1, nat32=True, nat16=True, name="proj_k")
        v_new, v_t = _proj(xb, w_in_t, (l,), 2, nat32=True, trans=True, name="proj_v")
        qi16, qi_t = _proj(xb, w_in_t, (l,), 3, nat16=True, trans=True, name="proj_qi")
        p2, ki16, w_t = _proj_idx(xb, w_in_t, (l,), 4 * bw // LANES)
        p3 = _matmul(xb, w_in_t, (l,), TAIL_OFF, 5, bw, name="proj_tail", w_transposed=True)
        ki_new = p2[:, :IDX_DIM]
        qi_t = qi_t.reshape(m // LANES, IDX_HEADS, IDX_DIM, LANES)
        v_t = v_t.reshape(m // LANES, bw, LANES)

        def sample_queries_t(a16, heads):
            a = jnp.pad(a16[mp:].reshape(bs, ts, heads, bw // heads), ((0, 0), (0, QT - ts), (0, 0), (0, 0)))
            return jnp.transpose(a, (0, 2, 3, 1))

        w_t_s = jnp.transpose(jnp.pad(p2[mp:, IDX_DIM:IDX_DIM + IDX_HEADS].reshape(bs, ts, IDX_HEADS),
                                      ((0, 0), (0, QT - ts), (0, 0))), (0, 2, 1))
        k16_s, v_t_s, ki16_s = _sample_keys(l, cache_k, cache_v, cache_ki, k_new, v_new, p2, new_row0=mp,
                                            n_keys=n_keys_s)

        branch_buf = jnp.zeros((m, bw), BF16)
        oa = _dsa(q_t, qi_t, w_t, k16, v_t, ki16, bias_near, near_ok, branch_buf, batch=bp, nb=tp // QT, n_keys=tp,
                  out_rows=QT, out_base=0, first_real=front, qb0=0, skip_below=max(1, front // QT), k_sel=ksel_p)
        oa = _dsa(sample_queries_t(q16, A_HEADS), sample_queries_t(qi16, IDX_HEADS), w_t_s, k16_s, v_t_s, ki16_s,
                  bias_near, near_ok, oa, batch=bs, nb=1, n_keys=n_keys_s, out_rows=ts, out_base=mp // ts,
                  first_real=0, qb0=past // CHUNK, skip_below=0, k_sel=ksel_s)

        tb_p = 256
        ob = _pool(p3, 0, p3, 0, pool_w[l], pool_scale[l], branch_buf, batch=bp, rows=tp, tb=tb_p, row0=0,
                   halo_map=lambda b, i: jnp.maximum((b * tp + i * tb_p) // HALO - 1, 0), first_valid=front)
        pool_hist = jnp.pad(cache_pool[l], ((0, 0), (HALO - POOL_PAST, 0), (0, 0))).reshape(bs * HALO, bw)
        ob = _pool(p3, 0, pool_hist, 0, pool_w[l], pool_scale[l], ob, batch=bs, rows=ts, tb=ts, row0=mp,
                   halo_map=lambda b, i: b, first_valid=-POOL_PAST)

        oc, ret_p = _retention(p3, zero_state, cos_p, sin_p, decay, branch_buf, batch=bp, rows=tp, row0=0,
                               first_valid=front)
        oc, ret_s = _retention(p3, state_ret[l].astype(F32), cos_s, sin_s, decay, oc, batch=bs, rows=ts, row0=mp,
                               first_valid=0)

        merged = _merge(l, xb, oa, ob, oc, w_branch, w_gate, b_gate)
        x32, xb = _matmul_norm(merged, w_out, (l,), x32, ln1_g[l], ln1_b[l])

        if l % 2 == 0:
            h = _gate_up(xb, ffn_w_gate, ffn_w_up, (l // 2,))
            f = _matmul(h, ffn_w_down, (l // 2,), 0, d // 512, 512, name="ffn_down")
        else:
            gates, experts = _router(x32, moe_w_router[l // 2], moe_b_router[l // 2])
            src, pos, tile_expert, n_used = _dispatch_plan(experts, MOE_TILE, token_rows)
            x_sorted = _gather_rows(x32, src, n_used, MOE_TILE)
            y_sorted = _moe_experts(x_sorted, tile_expert, n_used, moe_w_gate, moe_w_up, moe_w_down, l // 2, MOE_TILE)
            f = _combine_rows(y_sorted, pos, gates, MOE_TILE)
        if l + 1 < DEPTH:
            x32, xb = _layer_norm(x32, f, ln2_g[l], ln2_b[l])
        else:
            y_prompt, y_sample = _output_norm(x32, f, ln2_g[l], ln2_b[l], bp, seq)

        def prompt_rows(a, width):
            return a[:mp].reshape(bp, tp, width)[:, front:]

        u = p3[:, :bw]
        outs["kp"].append(prompt_rows(k_new, bw).reshape(bp, t_real, A_HEADS, A_HEAD_DIM))
        outs["vp"].append(prompt_rows(v_new, bw).reshape(bp, t_real, A_HEADS, A_HEAD_DIM))
        outs["kip"].append(prompt_rows(ki_new, IDX_DIM))
        outs["poolp"].append(prompt_rows(u, bw)[:, -POOL_PAST:])
        outs["retp"].append(ret_p)
        outs["ks"].append(k_new[mp:].reshape(bs, ts, A_HEADS, A_HEAD_DIM))
        outs["vs"].append(v_new[mp:].reshape(bs, ts, A_HEADS, A_HEAD_DIM))
        outs["kis"].append(ki_new[mp:].reshape(bs, ts, IDX_DIM))
        outs["pools"].append(u[mp:].reshape(bs, ts, bw)[:, -POOL_PAST:])
        outs["rets"].append(ret_s)

    return (y_prompt, y_sample.reshape(bs, ts, d)) + tuple(
        jnp.stack(outs[name]) for name in ("kp", "vp", "kip", "poolp", "retp", "ks", "vs", "kis", "pools", "rets"))
```

```python
import functools
import math

import jax
import jax.numpy as jnp
from jax import lax
from jax.experimental import pallas as pl
from jax.experimental.pallas import tpu as pltpu

F32 = jnp.float32
BF16 = jnp.bfloat16

D_MODEL = 2048
DEPTH = 2
CHUNK = 64
N_META = 16
BRANCH_WIDTH = D_MODEL // 2
A_HEADS = 8
A_HEAD_DIM = BRANCH_WIDTH // A_HEADS
IDX_HEADS = 16
IDX_DIM = 64
TOPK_MAX = 256
T5_BUCKETS = 32
T5_MAX_DIST = 128
POOL_WINDOWS = (2, 4, 8, 16)
POOL_GROUP = BRANCH_WIDTH // 4
POOL_PAST = 15
RET_HEADS = 8
RET_HEAD_DIM = BRANCH_WIDTH // RET_HEADS
ROPE_BASE = 10000.0
N_BRANCH = 3
D_FF = 11 * D_MODEL // 4
N_EXPERTS = 8
TOP_K = 2
ALPHA = (2 * DEPTH) ** 0.25
LN_EPS = 1e-5
IN_SPLITS = (BRANCH_WIDTH, BRANCH_WIDTH, BRANCH_WIDTH, IDX_HEADS * IDX_DIM, IDX_DIM, IDX_HEADS,
             BRANCH_WIDTH, BRANCH_WIDTH, BRANCH_WIDTH, BRANCH_WIDTH, BRANCH_WIDTH)
IN_WIDTH = sum(IN_SPLITS)
TAIL_OFF = 4 * BRANCH_WIDTH + IDX_DIM + IDX_HEADS

LANES = 128
SUBLANES = 8
HALO = 16
QT = 2 * CHUNK
WN = QT + 2 * CHUNK
KEY_TILE = 256
VMEM_LIMIT = 56 * 1024 * 1024
TM = 512
MOE_TILE = 512
LN_TILE = 256
INT_MIN = -2 ** 31
NEG = -1e30
M_INIT = -1e20
TINY = 1e-30
ATTN_LOG2_SCALE = A_HEAD_DIM ** -0.5 * math.log2(math.e)


def _cparams(sem):
    return pltpu.CompilerParams(dimension_semantics=sem, vmem_limit_bytes=VMEM_LIMIT)


def _dot(a, b):
    return jnp.dot(a, b, preferred_element_type=F32)


def _dot_nt(a, b):
    return lax.dot_general(a, b, (((1,), (1,)), ((), ())), preferred_element_type=F32)


def _dot_tn(a, b):
    return lax.dot_general(a, b, (((0,), (0,)), ((), ())), preferred_element_type=F32)


def _ln(x, g_ref, b_ref):
    mu = jnp.mean(x, axis=-1, keepdims=True)
    xc = x - mu
    var = jnp.mean(xc * xc, axis=-1, keepdims=True)
    return xc * lax.rsqrt(var + LN_EPS) * g_ref[...] + b_ref[...]


def _ln_res_kernel(x_ref, y_ref, g_ref, b_ref, o32_ref, o16_ref):
    y = _ln(ALPHA * x_ref[...] + y_ref[...], g_ref, b_ref)
    o32_ref[...] = y
    o16_ref[...] = y.astype(BF16)


def _layer_norm(x, y, g, b, tm=LN_TILE):
    m, d = x.shape
    row = pl.BlockSpec((tm, d), lambda i: (i, 0))
    vec = pl.BlockSpec((1, d), lambda i: (0, 0))
    return pl.pallas_call(
        _ln_res_kernel,
        out_shape=(jax.ShapeDtypeStruct((m, d), F32), jax.ShapeDtypeStruct((m, d), BF16)),
        grid=(m // tm,),
        in_specs=[row, row, vec, vec],
        out_specs=(row, row),
        compiler_params=_cparams(("parallel",)),
        name="layer_norm",
    )(x, y, g.reshape(1, d), b.reshape(1, d))


def _seq_tile_maps(bp, seq, tm):
    tiles_per_seq = seq // tm + 1
    n_prompt_tiles = bp * tiles_per_seq

    def prompt_map(i):
        in_prompt = i < n_prompt_tiles
        b = jnp.minimum(i // tiles_per_seq, bp - 1)
        r = jnp.where(in_prompt, jnp.maximum(i % tiles_per_seq - 1, 0), seq // tm - 1)
        return (b, r, 0)

    def sample_map(i):
        return (jnp.maximum(i - n_prompt_tiles, 0), 0)

    return tiles_per_seq, n_prompt_tiles, prompt_map, sample_map


def _ln_in_kernel(head_ref, xp_ref, xs_ref, g_ref, b_ref, o32_ref, o16_ref, *, tiles_per_seq, n_prompt_tiles):
    i = pl.program_id(0)
    x = jnp.where(i >= n_prompt_tiles, xs_ref[...],
                  jnp.where(i % tiles_per_seq == 0, head_ref[...], xp_ref[...]))
    y = _ln(x, g_ref, b_ref)
    o32_ref[...] = y
    o16_ref[...] = y.astype(BF16)


def _input_norm(head, x_prompt, x_sample, g, b, tm=LN_TILE):
    bp, seq, d = x_prompt.shape
    ms = x_sample.shape[0]
    tiles_per_seq, n_prompt_tiles, prompt_map, sample_map = _seq_tile_maps(bp, seq, tm)
    m = n_prompt_tiles * tm + ms
    row = pl.BlockSpec((tm, d), lambda i: (i, 0))
    vec = pl.BlockSpec((1, d), lambda i: (0, 0))
    return pl.pallas_call(
        functools.partial(_ln_in_kernel, tiles_per_seq=tiles_per_seq, n_prompt_tiles=n_prompt_tiles),
        out_shape=(jax.ShapeDtypeStruct((m, d), F32), jax.ShapeDtypeStruct((m, d), BF16)),
        grid=(m // tm,),
        in_specs=[pl.BlockSpec((tm, d), lambda i: (0, 0)), pl.BlockSpec((None, tm, d), prompt_map),
                  pl.BlockSpec((tm, d), sample_map), vec, vec],
        out_specs=(row, row),
        compiler_params=_cparams(("parallel",)),
        name="input_norm",
    )(head, x_prompt, x_sample, g.reshape(1, d), b.reshape(1, d))


def _ln_out_kernel(x_ref, y_ref, g_ref, b_ref, yp_ref, ys_ref, *, tiles_per_seq, n_prompt_tiles):
    i = pl.program_id(0)
    out = _ln(ALPHA * x_ref[...] + y_ref[...], g_ref, b_ref)

    @pl.when(jnp.logical_and(i < n_prompt_tiles, i % tiles_per_seq != 0))
    def _():
        yp_ref[...] = out

    @pl.when(i >= n_prompt_tiles)
    def _():
        ys_ref[...] = out


def _output_norm(x, y, g, b, bp, seq, tm=LN_TILE):
    m, d = x.shape
    tiles_per_seq, n_prompt_tiles, prompt_map, sample_map = _seq_tile_maps(bp, seq, tm)
    ms = m - n_prompt_tiles * tm
    row = pl.BlockSpec((tm, d), lambda i: (i, 0))
    vec = pl.BlockSpec((1, d), lambda i: (0, 0))
    return pl.pallas_call(
        functools.partial(_ln_out_kernel, tiles_per_seq=tiles_per_seq, n_prompt_tiles=n_prompt_tiles),
        out_shape=(jax.ShapeDtypeStruct((bp, seq, d), F32), jax.ShapeDtypeStruct((ms, d), F32)),
        grid=(m // tm,),
        in_specs=[row, row, vec, vec],
        out_specs=(pl.BlockSpec((None, tm, d), prompt_map), pl.BlockSpec((tm, d), sample_map)),
        compiler_params=_cparams(("arbitrary",)),
        name="output_norm",
    )(x, y, g.reshape(1, d), b.reshape(1, d))


def _mm_kernel(x_ref, w_ref, o_ref, wb_ref, *, w_rows_are_outputs):
    @pl.when(pl.program_id(1) == 0)
    def _():
        if w_rows_are_outputs:
            wb_ref[...] = w_ref[...].reshape(w_ref.shape[-2:]).T.astype(BF16)
        else:
            wb_ref[...] = w_ref[...].astype(BF16)

    o_ref[...] = _dot(x_ref[...], wb_ref[...]).astype(o_ref.dtype)


def _matmul(x, w, lead, col0, n_tiles, tn, name="matmul", w_transposed=False):
    m, k = x.shape
    nl = len(lead)
    if w_transposed:
        w_spec = pl.BlockSpec((pl.Element(1),) * nl + (pl.Element(tn), pl.Element(k)),
                              lambda j, i: tuple(lead) + (pl.multiple_of(j * tn + col0, SUBLANES), 0))
    else:
        w_spec = pl.BlockSpec((None,) * nl + (k, tn), lambda j, i: tuple(lead) + (0, j + col0))
    x_spec = pl.BlockSpec((TM, k), lambda j, i: (i, 0))
    o_spec = pl.BlockSpec((TM, tn), lambda j, i: (i, j))
    return pl.pallas_call(
        functools.partial(_mm_kernel, w_rows_are_outputs=w_transposed),
        out_shape=jax.ShapeDtypeStruct((m, n_tiles * tn), F32),
        grid=(n_tiles, m // TM),
        in_specs=[x_spec, w_spec],
        out_specs=o_spec,
        scratch_shapes=[pltpu.VMEM((k, tn), BF16)],
        compiler_params=_cparams(("arbitrary", "arbitrary")),
        name=name,
    )(x, w)


def _matmul_norm_kernel(a_ref, w_ref, x_ref, g_ref, b_ref, o32_ref, o16_ref, wb_ref):
    @pl.when(pl.program_id(0) == 0)
    def _():
        wb_ref[...] = w_ref[...].astype(BF16)

    y = _ln(ALPHA * x_ref[...] + _dot(a_ref[...], wb_ref[...]), g_ref, b_ref)
    o32_ref[...] = y
    o16_ref[...] = y.astype(BF16)


def _matmul_norm(a, w, lead, x, g, b, tm=LN_TILE):
    m, k = a.shape
    d = x.shape[1]
    nl = len(lead)
    row = pl.BlockSpec((tm, d), lambda i: (i, 0))
    vec = pl.BlockSpec((1, d), lambda i: (0, 0))
    return pl.pallas_call(
        _matmul_norm_kernel,
        out_shape=(jax.ShapeDtypeStruct((m, d), F32), jax.ShapeDtypeStruct((m, d), BF16)),
        grid=(m // tm,),
        in_specs=[pl.BlockSpec((tm, k), lambda i: (i, 0)),
                  pl.BlockSpec((None,) * nl + (k, d), lambda i: tuple(lead) + (0, 0), pipeline_mode=pl.Buffered(1)),
                  row, vec, vec],
        out_specs=(row, row),
        scratch_shapes=[pltpu.VMEM((k, d), BF16)],
        compiler_params=_cparams(("arbitrary",)),
        name="matmul_norm",
    )(a, w, x, g.reshape(1, d), b.reshape(1, d))


def _store_lane_tiles_t(res, out_ref):
    rows, width = res.shape
    for r in range(rows // LANES):
        for c in range(width // LANES):
            tile = res[r * LANES:(r + 1) * LANES, c * LANES:(c + 1) * LANES]
            out_ref[r, c] = tile.T.astype(out_ref.dtype)


def _proj_kernel(x_ref, w_ref, *refs, nat32, nat16, trans, out_scale):
    outs, wb_ref = list(refs[:-1]), refs[-1]

    @pl.when(pl.program_id(0) == 0)
    def _():
        wb_ref[...] = w_ref[...].T.astype(BF16)

    res = _dot(x_ref[...], wb_ref[...])
    if out_scale is not None:
        res = res * out_scale
    if nat32:
        outs.pop(0)[...] = res
    if nat16:
        outs.pop(0)[...] = res.astype(BF16)
    if trans:
        _store_lane_tiles_t(res, outs.pop(0))


def _proj(x, w, lead, col0, *, nat32=False, nat16=False, trans=False, out_scale=None, tm=TM, name="proj"):
    m, k = x.shape
    bw = BRANCH_WIDTH
    nl = len(lead)
    shapes, specs = [], []
    row_spec = pl.BlockSpec((tm, bw), lambda i: (i, 0))
    if nat32:
        shapes.append(jax.ShapeDtypeStruct((m, bw), F32)); specs.append(row_spec)
    if nat16:
        shapes.append(jax.ShapeDtypeStruct((m, bw), BF16)); specs.append(row_spec)
    if trans:
        shapes.append(jax.ShapeDtypeStruct((m // LANES, bw // LANES, LANES, LANES), BF16))
        specs.append(pl.BlockSpec((tm // LANES, bw // LANES, LANES, LANES), lambda i: (i, 0, 0, 0)))
    return pl.pallas_call(
        functools.partial(_proj_kernel, nat32=nat32, nat16=nat16, trans=trans, out_scale=out_scale),
        out_shape=tuple(shapes),
        grid=(m // tm,),
        in_specs=[pl.BlockSpec((tm, k), lambda i: (i, 0)),
                  pl.BlockSpec((None,) * nl + (bw, k), lambda i: tuple(lead) + (col0, 0))],
        out_specs=tuple(specs),
        scratch_shapes=[pltpu.VMEM((k, bw), BF16)],
        compiler_params=_cparams(("arbitrary",)),
        name=name,
    )(x, w)


def _proj_idx_kernel(x_ref, w_ref, nat_ref, ki_ref, wt_ref, wb_ref):
    @pl.when(pl.program_id(0) == 0)
    def _():
        wb_ref[...] = w_ref[...].T.astype(BF16)

    res = _dot(x_ref[...], wb_ref[...])
    nat_ref[...] = res
    ki_ref[...] = res[:, :IDX_DIM].astype(BF16)
    for r in range(res.shape[0] // LANES):
        wt_ref[r] = res[r * LANES:(r + 1) * LANES, :].T[IDX_DIM:IDX_DIM + IDX_HEADS, :]


def _proj_idx(x, w, lead, col_block, tm=TM):
    m, k = x.shape
    nl = len(lead)
    return pl.pallas_call(
        _proj_idx_kernel,
        out_shape=(jax.ShapeDtypeStruct((m, LANES), F32), jax.ShapeDtypeStruct((m, IDX_DIM), BF16),
                   jax.ShapeDtypeStruct((m // LANES, IDX_HEADS, LANES), F32)),
        grid=(m // tm,),
        in_specs=[pl.BlockSpec((tm, k), lambda i: (i, 0)),
                  pl.BlockSpec((None,) * nl + (LANES, k), lambda i: tuple(lead) + (col_block, 0))],
        out_specs=(pl.BlockSpec((tm, LANES), lambda i: (i, 0)), pl.BlockSpec((tm, IDX_DIM), lambda i: (i, 0)),
                   pl.BlockSpec((tm // LANES, IDX_HEADS, LANES), lambda i: (i, 0, 0))),
        scratch_shapes=[pltpu.VMEM((k, LANES), BF16)],
        compiler_params=_cparams(("arbitrary",)),
        name="proj_idx",
    )(x, w)


def _gate_up_kernel(x_ref, wg_ref, wu_ref, o_ref, wgb_ref, wub_ref):
    @pl.when(pl.program_id(1) == 0)
    def _():
        wgb_ref[...] = wg_ref[...].astype(BF16)
        wub_ref[...] = wu_ref[...].astype(BF16)

    x = x_ref[...]
    g = _dot(x, wgb_ref[...])
    u = _dot(x, wub_ref[...])
    o_ref[...] = (g * jax.nn.sigmoid(g) * u).astype(o_ref.dtype)


def _gate_up(x, wg, wu, lead, tn=512):
    m, k = x.shape
    n = wg.shape[-1]
    nl = len(lead)
    w_spec = pl.BlockSpec((None,) * nl + (k, tn), lambda j, i: tuple(lead) + (0, j))
    x_spec = pl.BlockSpec((TM, k), lambda j, i: (i, 0))
    return pl.pallas_call(
        _gate_up_kernel,
        out_shape=jax.ShapeDtypeStruct((m, n), BF16),
        grid=(n // tn, m // TM),
        in_specs=[x_spec, w_spec, w_spec],
        out_specs=pl.BlockSpec((TM, tn), lambda j, i: (i, j)),
        scratch_shapes=[pltpu.VMEM((k, tn), BF16), pltpu.VMEM((k, tn), BF16)],
        compiler_params=_cparams(("arbitrary", "arbitrary")),
        name="gate_up",
    )(x, wg, wu)


def _merge_kernel(x_ref, oa_ref, ob_ref, oc_ref, wg0_ref, wg1_ref, wg2_ref, wb0_ref, wb1_ref, wb2_ref,
                  bg0_ref, bg1_ref, bg2_ref, o_ref, wgb_ref, wbb_ref):
    wg_refs = (wg0_ref, wg1_ref, wg2_ref)
    wb_refs = (wb0_ref, wb1_ref, wb2_ref)

    @pl.when(pl.program_id(1) == 0)
    def _():
        for n in range(N_BRANCH):
            wgb_ref[n] = wg_refs[n][...].astype(BF16)
            wbb_ref[n] = wb_refs[n][...].astype(BF16)

    x = x_ref[...]
    acc = None
    for n, (o_in, bg) in enumerate(zip((oa_ref, ob_ref, oc_ref), (bg0_ref, bg1_ref, bg2_ref))):
        gate = jax.nn.sigmoid(_dot(x, wgb_ref[n]) + bg[...])
        term = gate * _dot(o_in[...], wbb_ref[n])
        acc = term if acc is None else acc + term
    o_ref[...] = acc.astype(o_ref.dtype)


def _merge(l, x, oa, ob, oc, w_branch, w_gate, b_gate, tn=256):
    m, d = x.shape
    w = oa.shape[1]
    nt = d // tn
    x_spec = pl.BlockSpec((TM, d), lambda j, i: (i, 0))
    o_in_spec = pl.BlockSpec((TM, w), lambda j, i: (i, 0))
    wg_specs = [pl.BlockSpec((None, d, tn), lambda j, i, n=n: (l, 0, n * nt + j)) for n in range(N_BRANCH)]
    wb_specs = [pl.BlockSpec((None, None, w, tn), lambda j, i, n=n: (l, n, 0, j)) for n in range(N_BRANCH)]
    bg_specs = [pl.BlockSpec((None, 1, tn), lambda j, i, n=n: (l, 0, n * nt + j)) for n in range(N_BRANCH)]
    return pl.pallas_call(
        _merge_kernel,
        out_shape=jax.ShapeDtypeStruct((m, d), BF16),
        grid=(nt, m // TM),
        in_specs=[x_spec, o_in_spec, o_in_spec, o_in_spec] + wg_specs + wb_specs + bg_specs,
        out_specs=pl.BlockSpec((TM, tn), lambda j, i: (i, j)),
        scratch_shapes=[pltpu.VMEM((N_BRANCH, d, tn), BF16), pltpu.VMEM((N_BRANCH, w, tn), BF16)],
        compiler_params=_cparams(("arbitrary", "arbitrary")),
        name="merge",
    )(x, oa, ob, oc, w_gate, w_gate, w_gate, w_branch, w_branch, w_branch,
      b_gate.reshape(DEPTH, 1, N_BRANCH * d), b_gate.reshape(DEPTH, 1, N_BRANCH * d),
      b_gate.reshape(DEPTH, 1, N_BRANCH * d))


def _split_bf16(a):
    hi = a.astype(BF16)
    lo = (a - hi.astype(F32)).astype(BF16)
    return hi, lo


def _router_kernel(x_ref, w_ref, b_ref, gate_ref, expert_ref):
    xh, xl = _split_bf16(x_ref[...])
    wh, wl = _split_bf16(w_ref[...])
    logits = _dot(xh, wh) + (_dot(xh, wl) + _dot(xl, wh)) + b_ref[...]
    lane = lax.broadcasted_iota(jnp.int32, logits.shape, 1)
    logits = jnp.where(lane < N_EXPERTS, logits, -jnp.inf)
    m1 = jnp.max(logits, axis=-1, keepdims=True)
    i1 = jnp.min(jnp.where(logits == m1, lane, LANES), axis=-1, keepdims=True)
    rest = jnp.where(lane == i1, -jnp.inf, logits)
    m2 = jnp.max(rest, axis=-1, keepdims=True)
    i2 = jnp.min(jnp.where(rest == m2, lane, LANES), axis=-1, keepdims=True)
    e = jnp.exp(m2 - m1)
    p1 = 1.0 / (1.0 + e)
    p2 = e / (1.0 + e)
    gate_ref[...] = jnp.where(lane == 0, p1, jnp.where(lane == 1, p2, 0.0))
    expert_ref[...] = jnp.where(lane == 0, i1, jnp.where(lane == 1, i2, 0))


def _router(x32, w_r, b_r, tm=256):
    m, d = x32.shape
    w_pad = jnp.pad(w_r, ((0, 0), (0, LANES - N_EXPERTS)))
    b_pad = jnp.pad(b_r, (0, LANES - N_EXPERTS)).reshape(1, LANES)
    out_spec = pl.BlockSpec((tm, LANES), lambda i: (i, 0))
    gates, experts = pl.pallas_call(
        _router_kernel,
        out_shape=(jax.ShapeDtypeStruct((m, LANES), F32), jax.ShapeDtypeStruct((m, LANES), jnp.int32)),
        grid=(m // tm,),
        in_specs=[pl.BlockSpec((tm, d), lambda i: (i, 0)), pl.BlockSpec((d, LANES), lambda i: (0, 0)),
                  pl.BlockSpec((1, LANES), lambda i: (0, 0))],
        out_specs=(out_spec, out_spec),
        compiler_params=_cparams(("parallel",)),
        name="router",
    )(x32, w_pad, b_pad)
    return gates[:, :TOP_K], experts[:, :TOP_K]


def _dispatch_plan(experts, tile, token_rows):
    m = experts.shape[0]
    n_assign = TOP_K * m
    n_tiles = n_assign // tile + N_EXPERTS
    e = experts.reshape(n_assign)
    live = jnp.repeat(token_rows, TOP_K)
    onehot = jnp.logical_and(e[:, None] == jnp.arange(N_EXPERTS, dtype=jnp.int32)[None, :],
                             live[:, None]).astype(jnp.int32)
    rank = jnp.sum((jnp.cumsum(onehot, axis=0) - onehot) * onehot, axis=1)
    tiles_per = (jnp.sum(onehot, axis=0) + tile - 1) // tile
    tile_end = jnp.cumsum(tiles_per)
    pos = ((tile_end - tiles_per)[e] * tile + rank).astype(jnp.int32)
    slot = jnp.where(live, pos, n_tiles * tile)
    src = jnp.zeros((n_tiles * tile,), jnp.int32).at[slot].set(jnp.arange(n_assign, dtype=jnp.int32) // TOP_K,
                                                               mode="drop")
    pos = jnp.where(live, pos, 0)
    n_used = tile_end[-1:]
    t_idx = jnp.minimum(jnp.arange(n_tiles, dtype=jnp.int32), n_used - 1)
    tile_expert = jnp.sum((t_idx[:, None] >= tile_end[None, :]).astype(jnp.int32), axis=1)
    return src, pos, tile_expert.astype(jnp.int32), n_used.astype(jnp.int32)


def _issue_rows(idx_ref, first, count, stride, src_hbm, dst, sem):
    def body(i, carry):
        pltpu.make_async_copy(src_hbm.at[pl.ds(idx_ref[first + i * stride], 1), :], dst.at[pl.ds(i, 1), :],
                              sem).start()
        return carry

    lax.fori_loop(0, count, body, 0, unroll=8)


def _wait_rows(count, src_hbm, dst, sem):
    pltpu.make_async_copy(src_hbm.at[pl.ds(0, count), :], dst, sem).wait()


def _gather_kernel(src_ref, nu_ref, x_hbm, o_ref, buf, sem, *, tile):
    t = pl.program_id(0)

    @pl.when(t == 0)
    def _():
        _issue_rows(src_ref, 0, tile, 1, x_hbm, buf.at[0], sem.at[0])

    @pl.when(t + 1 < nu_ref[0])
    def _():
        nxt = (t + 1) % 2
        _issue_rows(src_ref, (t + 1) * tile, tile, 1, x_hbm, buf.at[nxt], sem.at[nxt])

    @pl.when(t < nu_ref[0])
    def _():
        cur = t % 2
        _wait_rows(tile, x_hbm, buf.at[cur], sem.at[cur])
        o_ref[...] = buf[cur].astype(o_ref.dtype)

    @pl.when(t >= nu_ref[0])
    def _():
        o_ref[...] = jnp.zeros(o_ref.shape, o_ref.dtype)


def _gather_rows(x32, src, n_used, tile):
    d = x32.shape[1]
    n_rows = src.shape[0]
    return pl.pallas_call(
        functools.partial(_gather_kernel, tile=tile),
        out_shape=jax.ShapeDtypeStruct((n_rows, d), BF16),
        grid_spec=pltpu.PrefetchScalarGridSpec(
            num_scalar_prefetch=2, grid=(n_rows // tile,),
            in_specs=[pl.BlockSpec(memory_space=pl.ANY)],
            out_specs=pl.BlockSpec((tile, d), lambda t, src, nu: (t, 0)),
            scratch_shapes=[pltpu.VMEM((2, tile, d), F32), pltpu.SemaphoreType.DMA((2,))]),
        compiler_params=_cparams(("arbitrary",)),
        name="moe_gather",
    )(src, n_used, x32)


def _gathered_expert_sum(pos_ref, y_hbm, g_ref, buf, sem, tile):
    t = pl.program_id(0)

    def issue(tile_idx, slot):
        for s in range(TOP_K):
            _issue_rows(pos_ref, tile_idx * tile * TOP_K + s, tile, TOP_K, y_hbm, buf.at[slot, s], sem.at[slot])

    @pl.when(t == 0)
    def _():
        issue(0, 0)

    @pl.when(t + 1 < pl.num_programs(0))
    def _():
        issue(t + 1, (t + 1) % 2)

    cur = t % 2
    for s in range(TOP_K):
        _wait_rows(tile, y_hbm, buf.at[cur, s], sem.at[cur])
    gates = g_ref[...]
    total = buf[cur, 0] * gates[:, 0:1]
    for s in range(1, TOP_K):
        total = total + buf[cur, s] * gates[:, s:s + 1]
    return total


def _combine_kernel(pos_ref, y_hbm, g_ref, o_ref, buf, sem, *, tile):
    o_ref[...] = _gathered_expert_sum(pos_ref, y_hbm, g_ref, buf, sem, tile)


def _combine_norm_out_kernel(pos_ref, y_hbm, g_ref, x_ref, lg_ref, lb_ref, yp_ref, ys_ref, buf, sem, *, tile,
                             tiles_per_seq, n_prompt_tiles):
    t = pl.program_id(0)
    out = _ln(ALPHA * x_ref[...] + _gathered_expert_sum(pos_ref, y_hbm, g_ref, buf, sem, tile), lg_ref, lb_ref)

    @pl.when(jnp.logical_and(t < n_prompt_tiles, t % tiles_per_seq != 0))
    def _():
        yp_ref[...] = out

    @pl.when(t >= n_prompt_tiles)
    def _():
        ys_ref[...] = out


def _combine_norm_out(y_sorted, pos, gates, x, g, b, bp, seq, tile=LN_TILE):
    m, d = x.shape
    tiles_per_seq, n_prompt_tiles, prompt_map, sample_map = _seq_tile_maps(bp, seq, tile)
    ms = m - n_prompt_tiles * tile
    row = pl.BlockSpec((tile, d), lambda t, pos: (t, 0))
    vec = pl.BlockSpec((1, d), lambda t, pos: (0, 0))
    return pl.pallas_call(
        functools.partial(_combine_norm_out_kernel, tile=tile, tiles_per_seq=tiles_per_seq,
                          n_prompt_tiles=n_prompt_tiles),
        out_shape=(jax.ShapeDtypeStruct((bp, seq, d), F32), jax.ShapeDtypeStruct((ms, d), F32)),
        grid_spec=pltpu.PrefetchScalarGridSpec(
            num_scalar_prefetch=1, grid=(m // tile,),
            in_specs=[pl.BlockSpec(memory_space=pl.ANY), pl.BlockSpec((tile, TOP_K), lambda t, pos: (t, 0)),
                      row, vec, vec],
            out_specs=(pl.BlockSpec((None, tile, d), lambda t, pos: prompt_map(t)),
                       pl.BlockSpec((tile, d), lambda t, pos: sample_map(t))),
            scratch_shapes=[pltpu.VMEM((2, TOP_K, tile, d), F32), pltpu.SemaphoreType.DMA((2,))]),
        compiler_params=_cparams(("arbitrary",)),
        name="moe_combine_norm",
    )(pos, y_sorted, gates, x, g.reshape(1, d), b.reshape(1, d))


def _combine_rows(y_sorted, pos, gates, tile):
    m = gates.shape[0]
    d = y_sorted.shape[1]
    return pl.pallas_call(
        functools.partial(_combine_kernel, tile=tile),
        out_shape=jax.ShapeDtypeStruct((m, d), F32),
        grid_spec=pltpu.PrefetchScalarGridSpec(
            num_scalar_prefetch=1, grid=(m // tile,),
            in_specs=[pl.BlockSpec(memory_space=pl.ANY), pl.BlockSpec((tile, TOP_K), lambda t, pos: (t, 0))],
            out_specs=pl.BlockSpec((tile, d), lambda t, pos: (t, 0)),
            scratch_shapes=[pltpu.VMEM((2, TOP_K, tile, d), F32), pltpu.SemaphoreType.DMA((2,))]),
        compiler_params=_cparams(("arbitrary",)),
        name="moe_combine",
    )(pos, y_sorted, gates)


def _expert_changed(te_ref, t):
    return jnp.logical_or(t == 0, te_ref[t] != te_ref[jnp.maximum(t - 1, 0)])


def _moe_gate_up_kernel(te_ref, nu_ref, x_ref, wg_ref, wu_ref, o_ref, wgb_ref, wub_ref):
    t = pl.program_id(1)

    @pl.when(_expert_changed(te_ref, t))
    def _():
        wgb_ref[...] = wg_ref[...].astype(BF16)
        wub_ref[...] = wu_ref[...].astype(BF16)

    @pl.when(t < nu_ref[0])
    def _():
        x = x_ref[...]
        g = _dot(x, wgb_ref[...])
        u = _dot(x, wub_ref[...])
        o_ref[...] = (g * jax.nn.sigmoid(g) * u).astype(o_ref.dtype)

    @pl.when(t >= nu_ref[0])
    def _():
        o_ref[...] = jnp.zeros(o_ref.shape, o_ref.dtype)


def _moe_down_kernel(te_ref, nu_ref, h_ref, w_ref, o_ref, wb_ref):
    t = pl.program_id(1)

    @pl.when(_expert_changed(te_ref, t))
    def _():
        wb_ref[...] = w_ref[...].astype(BF16)

    @pl.when(t < nu_ref[0])
    def _():
        o_ref[...] = _dot(h_ref[...], wb_ref[...])

    @pl.when(t >= nu_ref[0])
    def _():
        o_ref[...] = jnp.zeros(o_ref.shape, o_ref.dtype)


def _moe_experts(x_sorted, tile_expert, n_used, w_gate, w_up, w_down, layer, tile, tn=512):
    r, d = x_sorted.shape
    ff = w_gate.shape[-1]
    n_tiles = r // tile

    def x_rows(width):
        return pl.BlockSpec((tile, width), lambda j, t, te, nu: (jnp.minimum(t, nu[0] - 1), 0))

    def w_cols(k):
        return pl.BlockSpec((None, None, k, tn), lambda j, t, te, nu: (layer, te[t], 0, j))

    h = pl.pallas_call(
        _moe_gate_up_kernel,
        out_shape=jax.ShapeDtypeStruct((r, ff), BF16),
        grid_spec=pltpu.PrefetchScalarGridSpec(
            num_scalar_prefetch=2, grid=(ff // tn, n_tiles),
            in_specs=[x_rows(d), w_cols(d), w_cols(d)],
            out_specs=pl.BlockSpec((tile, tn), lambda j, t, te, nu: (t, j)),
            scratch_shapes=[pltpu.VMEM((d, tn), BF16), pltpu.VMEM((d, tn), BF16)]),
        compiler_params=_cparams(("arbitrary", "arbitrary")),
        name="moe_gate_up",
    )(tile_expert, n_used, x_sorted, w_gate, w_up)
    return pl.pallas_call(
        _moe_down_kernel,
        out_shape=jax.ShapeDtypeStruct((r, d), F32),
        grid_spec=pltpu.PrefetchScalarGridSpec(
            num_scalar_prefetch=2, grid=(d // tn, n_tiles),
            in_specs=[x_rows(ff), w_cols(ff)],
            out_specs=pl.BlockSpec((tile, tn), lambda j, t, te, nu: (t, j)),
            scratch_shapes=[pltpu.VMEM((ff, tn), BF16)]),
        compiler_params=_cparams(("arbitrary", "arbitrary")),
        name="moe_down",
    )(tile_expert, n_used, h, w_down)


def _pool_kernel(u_ref, halo_ref, w_ref, sc_ref, _buf_ref, o_ref, ext_ref, *, tb, first_valid):
    row0 = pl.program_id(1) * tb
    r_cur = row0 + lax.broadcasted_iota(jnp.int32, (tb, 1), 0)
    r_halo = row0 - HALO + lax.broadcasted_iota(jnp.int32, (HALO, 1), 0)
    u = jnp.where(r_cur >= first_valid, u_ref[...], 0.0)
    ext_ref[0:HALO, :] = jnp.where(r_halo >= first_valid, halo_ref[...], 0.0)
    ext_ref[HALO:, :] = u
    seen = (r_cur - first_valid + 1).astype(F32)
    for g, win in enumerate(POOL_WINDOWS):
        c0, c1 = g * POOL_GROUP, (g + 1) * POOL_GROUP
        s = u[:, c0:c1]
        for back in range(1, win):
            s = s + ext_ref[HALO - back:HALO - back + tb, c0:c1]
        cnt = jnp.clip(seen, 1.0, float(win))
        diff = s / cnt - u[:, c0:c1]
        y = _dot(diff.astype(BF16), w_ref[g].astype(BF16))
        o_ref[:, c0:c1] = (y * sc_ref[:, c0:c1]).astype(o_ref.dtype)


def _pool(u2d, col_block, halo2d, halo_col_block, pool_w_l, pool_scale_l, out_buf, *, batch, rows, tb, row0,
          halo_map, first_valid):
    nb = rows // tb
    base = row0 // tb
    return pl.pallas_call(
        functools.partial(_pool_kernel, tb=tb, first_valid=first_valid),
        out_shape=jax.ShapeDtypeStruct(out_buf.shape, out_buf.dtype),
        grid=(batch, nb),
        in_specs=[
            pl.BlockSpec((tb, BRANCH_WIDTH), lambda b, i: (base + b * nb + i, col_block)),
            pl.BlockSpec((HALO, BRANCH_WIDTH), lambda b, i: (halo_map(b, i), halo_col_block)),
            pl.BlockSpec((len(POOL_WINDOWS), POOL_GROUP, POOL_GROUP), lambda b, i: (0, 0, 0)),
            pl.BlockSpec((1, BRANCH_WIDTH), lambda b, i: (0, 0)),
            pl.BlockSpec(memory_space=pl.ANY),
        ],
        out_specs=pl.BlockSpec((tb, BRANCH_WIDTH), lambda b, i: (base + b * nb + i, 0)),
        scratch_shapes=[pltpu.VMEM((HALO + tb, BRANCH_WIDTH), F32)],
        input_output_aliases={4: 0},
        compiler_params=_cparams(("parallel", "arbitrary")),
        name="pool",
    )(u2d, halo2d, pool_w_l, pool_scale_l.reshape(1, BRANCH_WIDTH), out_buf)


def _ret_kernel(q_ref, k_ref, v_ref, g_ref, cos_ref, sin_ref, dmat_ref, cross_ref, kdec_ref, gn_ref, s0_ref,
                _buf_ref, o_ref, s_out_ref, s_scr, *, first_valid):
    i = pl.program_id(1)

    @pl.when(i == 0)
    def _():
        s_scr[...] = s0_ref[...]

    rows = i * CHUNK + lax.broadcasted_iota(jnp.int32, (CHUNK, 1), 0)
    valid = rows >= first_valid
    cos = cos_ref[...]
    sin = sin_ref[...]
    rscale = RET_HEAD_DIM ** -0.5
    half = RET_HEAD_DIM // 2
    heads = range(RET_HEADS)
    sls = [slice(h * RET_HEAD_DIM, (h + 1) * RET_HEAD_DIM) for h in heads]

    def rotate(x):
        return x * cos + pltpu.roll(x, half, 1) * sin

    qr = [rotate(q_ref[:, sl]).astype(BF16) for sl in sls]
    kr = [jnp.where(valid, rotate(k_ref[:, sl]) * rscale, 0.0) for sl in sls]
    v = [jnp.where(valid, v_ref[:, sl], 0.0).astype(BF16) for sl in sls]
    state = [s_scr[h] for h in heads]
    inner = [_dot_nt(qr[h], kr[h].astype(BF16)) * dmat_ref[h] for h in heads]
    carried = [_dot(qr[h], state[h].astype(BF16)) * cross_ref[h] for h in heads]
    update = [_dot_tn((kr[h] * kdec_ref[h]).astype(BF16), v[h]) for h in heads]
    o = [_dot(inner[h].astype(BF16), v[h]) + carried[h] for h in heads]
    for h in heads:
        s_scr[h] = gn_ref[h] * state[h] + update[h]
    for h in heads:
        mu = jnp.mean(o[h], axis=-1, keepdims=True)
        oc = o[h] - mu
        var = jnp.mean(oc * oc, axis=-1, keepdims=True)
        gate = g_ref[:, sls[h]]
        o_ref[:, sls[h]] = (gate * jax.nn.sigmoid(gate) * (oc * lax.rsqrt(var + LN_EPS))).astype(o_ref.dtype)

    @pl.when(i == pl.num_programs(1) - 1)
    def _():
        s_out_ref[...] = s_scr[...]


def _retention(p3, s0, cos, sin, tabs, out_buf, *, batch, rows, row0, first_valid):
    nb = rows // CHUNK
    base = row0 // CHUNK
    dmat, cross, kdec, gn = tabs

    def col(c):
        return pl.BlockSpec((CHUNK, BRANCH_WIDTH), lambda b, i: (base + b * nb + i, c))

    tab_rows = pl.BlockSpec((CHUNK, RET_HEAD_DIM), lambda b, i: (i, 0))

    def full(a):
        return pl.BlockSpec(a.shape, lambda b, i: (0,) * a.ndim)

    state_spec = pl.BlockSpec((None, RET_HEADS, RET_HEAD_DIM, RET_HEAD_DIM), lambda b, i: (b, 0, 0, 0))
    return pl.pallas_call(
        functools.partial(_ret_kernel, first_valid=first_valid),
        out_shape=(jax.ShapeDtypeStruct(out_buf.shape, out_buf.dtype),
                   jax.ShapeDtypeStruct((batch, RET_HEADS, RET_HEAD_DIM, RET_HEAD_DIM), F32)),
        grid=(batch, nb),
        in_specs=[col(1), col(2), col(3), col(4), tab_rows, tab_rows, full(dmat), full(cross), full(kdec),
                  full(gn), state_spec, pl.BlockSpec(memory_space=pl.ANY)],
        out_specs=(pl.BlockSpec((CHUNK, BRANCH_WIDTH), lambda b, i: (base + b * nb + i, 0)), state_spec),
        scratch_shapes=[pltpu.VMEM((RET_HEADS, RET_HEAD_DIM, RET_HEAD_DIM), F32)],
        input_output_aliases={11: 0},
        compiler_params=_cparams(("parallel", "arbitrary")),
        name="retention",
    )(p3, p3, p3, p3, cos, sin, dmat, cross, kdec, gn, s0, out_buf)


def _sortable(x):
    bits = pltpu.bitcast(x + 0.0, jnp.int32)
    return bits ^ ((bits >> 31) & 0x7FFFFFFF)


def _dsa_kernel(q_ref, qi_ref, w_ref, k_ref, vt_ref, ki_ref, bn_ref, ok_ref, _buf_ref, o_ref,
                key_ref, keyn_ref, m_ref, den_ref, acc_ref, *, first_real, qb0, skip_below, k_sel):
    j = pl.program_id(1)

    @pl.when(j < skip_below)
    def _():
        o_ref[...] = jnp.zeros(o_ref.shape, o_ref.dtype)

    @pl.when(j >= skip_below)
    def _():
        win0 = pl.multiple_of(j * QT + (qb0 * CHUNK - (WN - QT)), LANES)
        n_far = (win0 + KEY_TILE - 1) // KEY_TILE
        near = pl.ds(win0, WN)

        def index_keys(ki, adm):
            acc = None
            for h in range(0, IDX_HEADS, 2):
                pair = _dot(ki, jnp.concatenate([qi_ref[0, h], qi_ref[0, h + 1]], axis=1))
                term = (w_ref[0, h:h + 1, :] * jnp.maximum(pair[:, :QT], 0.0)
                        + w_ref[0, h + 1:h + 2, :] * jnp.maximum(pair[:, QT:], 0.0))
                acc = term if acc is None else acc + term
            return jnp.where(adm, _sortable(acc), INT_MIN)

        def far_keys(c, carry):
            c0 = pl.multiple_of(c * KEY_TILE, KEY_TILE)
            row = c0 + lax.broadcasted_iota(jnp.int32, (KEY_TILE, 1), 0)
            adm = jnp.logical_and(row >= first_real, row < win0)
            key_ref[c] = index_keys(ki_ref[pl.ds(c0, KEY_TILE), :], adm)
            return carry

        lax.fori_loop(0, n_far, far_keys, 0)
        row_n = win0 + lax.broadcasted_iota(jnp.int32, (WN, 1), 0)
        keyn_ref[...] = index_keys(ki_ref[near, :], jnp.logical_and(row_n >= first_real, ok_ref[...] != 0))

        def fold(hit):
            parts = hit.reshape(hit.shape[0] // SUBLANES, SUBLANES, QT)
            while parts.shape[0] > 1:
                half = parts.reshape(parts.shape[0] // 2, 2, SUBLANES, QT)
                parts = half[:, 0] + half[:, 1]
            return parts[0]

        def count_ge(c):
            part = lax.fori_loop(0, n_far, lambda t, p: p + fold(jnp.where(key_ref[t] >= c, 1.0, 0.0)),
                                 fold(jnp.where(keyn_ref[...] >= c, 1.0, 0.0)))
            return jnp.sum(part, axis=0, keepdims=True)

        zero = jnp.zeros((1, QT), jnp.int32)
        n_zero = count_ge(zero)
        start = (jnp.where(n_zero >= k_sel, zero, INT_MIN), jnp.where(n_zero >= k_sel, n_zero, 0.0))

        def bit_step(it, carry):
            thr, n_at = carry
            cand = thr + jnp.left_shift(jnp.int32(1), jnp.int32(30) - it)
            n_cand = count_ge(cand)
            return jnp.where(n_cand >= k_sel, cand, thr), jnp.where(n_cand >= k_sel, n_cand, n_at)

        thr, n_at = lax.fori_loop(0, 31, bit_step, start)
        thr = jnp.maximum(thr, INT_MIN + 1)

        surplus = jnp.maximum(n_at - k_sel, 0.0)

        @pl.when(jnp.max(surplus) > 0.0)
        def _():
            ties_kept = n_at - count_ge(thr + 1) - surplus
            lower = (lax.broadcasted_iota(jnp.int32, (KEY_TILE, KEY_TILE), 0)
                     >= lax.broadcasted_iota(jnp.int32, (KEY_TILE, KEY_TILE), 1)).astype(BF16)

            def drop_late_ties(key, seen):
                tie = key == thr
                place = seen + _dot(lower[:key.shape[0], :key.shape[0]], jnp.where(tie, 1.0, 0.0).astype(BF16))
                key = jnp.where(jnp.logical_and(tie, place > ties_kept), thr - 1, key)
                return key, place[key.shape[0] - 1:, :]

            def far_ties(t, seen):
                key_ref[t], seen = drop_late_ties(key_ref[t], seen)
                return seen

            seen = lax.fori_loop(0, n_far, far_ties, jnp.zeros((1, QT), F32))
            keyn_ref[...], _ = drop_late_ties(keyn_ref[...], seen)

        m_ref[...] = jnp.full(m_ref.shape, M_INIT, F32)
        den_ref[...] = jnp.zeros(den_ref.shape, F32)
        acc_ref[...] = jnp.zeros(acc_ref.shape, F32)

        heads = range(A_HEADS)
        sls = [slice(h * A_HEAD_DIM, (h + 1) * A_HEAD_DIM) for h in heads]

        def attend(logits, sel, first_vt_tile):
            n_sub = logits[0].shape[0] // LANES
            lg = [jnp.where(sel, logits[h], NEG) for h in heads]
            m_old = [m_ref[h] for h in heads]
            m_new = [jnp.maximum(m_old[h], jnp.max(lg[h], axis=0, keepdims=True)) for h in heads]
            p = [jnp.exp2(lg[h] - m_new[h]) for h in heads]
            alpha = [jnp.exp2(m_old[h] - m_new[h]) for h in heads]
            pb = [p[h].astype(BF16) for h in heads]
            pv = []
            for h in heads:
                terms = [_dot(vt_ref[first_vt_tile + u, sls[h], :], pb[h][u * LANES:(u + 1) * LANES])
                         for u in range(n_sub)]
                pv.append(functools.reduce(lambda a, b: a + b, terms))
            for h in heads:
                den_ref[h] = alpha[h] * den_ref[h] + jnp.sum(p[h], axis=0, keepdims=True)
                acc_ref[h] = alpha[h] * acc_ref[h] + pv[h]
                m_ref[h] = m_new[h]

        def far_tile(c, carry):
            c0 = pl.multiple_of(c * KEY_TILE, KEY_TILE)
            attend([_dot(k_ref[pl.ds(c0, KEY_TILE), sls[h]], q_ref[0, h]) for h in heads],
                   key_ref[c] >= thr, c * (KEY_TILE // LANES))
            return carry

        lax.fori_loop(0, n_far, far_tile, 0)
        attend([_dot(k_ref[near, sls[h]], q_ref[0, h]) + bn_ref[h] for h in heads],
               keyn_ref[...] >= thr, win0 // LANES)
        for h in heads:
            out_t = acc_ref[h] / jnp.maximum(den_ref[h], TINY)
            o_ref[:, sls[h]] = out_t.T[:o_ref.shape[0]].astype(o_ref.dtype)


def _dsa(q_t, qi_t, w_t, k16, v_t, ki16, bias_near, near_ok, out_buf, *, batch, nb, n_keys, out_rows, out_base,
         first_real, qb0, skip_below, k_sel):
    width = k16.shape[1]
    return pl.pallas_call(
        functools.partial(_dsa_kernel, first_real=first_real, qb0=qb0, skip_below=skip_below, k_sel=k_sel),
        out_shape=jax.ShapeDtypeStruct(out_buf.shape, out_buf.dtype),
        grid=(batch, nb),
        in_specs=[
            pl.BlockSpec((1, A_HEADS, A_HEAD_DIM, QT), lambda b, j: (b * nb + j, 0, 0, 0)),
            pl.BlockSpec((1, IDX_HEADS, IDX_DIM, QT), lambda b, j: (b * nb + j, 0, 0, 0)),
            pl.BlockSpec((1, IDX_HEADS, QT), lambda b, j: (b * nb + j, 0, 0)),
            pl.BlockSpec((n_keys, width), lambda b, j: (b, 0)),
            pl.BlockSpec((n_keys // LANES, width, LANES), lambda b, j: (b, 0, 0)),
            pl.BlockSpec((n_keys, IDX_DIM), lambda b, j: (b, 0)),
            pl.BlockSpec((A_HEADS, WN, QT), lambda b, j: (0, 0, 0)),
            pl.BlockSpec((WN, QT), lambda b, j: (0, 0)),
            pl.BlockSpec(memory_space=pl.ANY),
        ],
        out_specs=pl.BlockSpec((out_rows, width), lambda b, j: (out_base + b * nb + j, 0)),
        scratch_shapes=[pltpu.VMEM((n_keys // KEY_TILE, KEY_TILE, QT), jnp.int32),
                        pltpu.VMEM((WN, QT), jnp.int32),
                        pltpu.VMEM((A_HEADS, 1, QT), F32),
                        pltpu.VMEM((A_HEADS, 1, QT), F32),
                        pltpu.VMEM((A_HEADS, A_HEAD_DIM, QT), F32)],
        input_output_aliases={8: 0},
        compiler_params=_cparams(("parallel", "arbitrary")),
        name="dsa",
    )(q_t, qi_t, w_t, k16, v_t, ki16, bias_near, near_ok, out_buf)


def _sample_keys_kernel(ck_ref, cv_ref, cki_ref, nk_ref, nv_ref, nki_ref, k_ref, vt_ref, ki_ref):
    c = pl.program_id(1)
    n_cache = pl.num_programs(1) - 1
    rows = k_ref.shape[0]

    @pl.when(c < n_cache)
    def _():
        for h in range(A_HEADS):
            sl = slice(h * A_HEAD_DIM, (h + 1) * A_HEAD_DIM)
            head_rows = pl.ds(h, rows, stride=A_HEADS)
            k_ref[:, sl] = ck_ref[0, head_rows, :].astype(BF16)
            vh = cv_ref[0, head_rows, :]
            for u in range(rows // LANES):
                vt_ref[u, sl, :] = vh[u * LANES:(u + 1) * LANES, :].T.astype(BF16)
        ki_ref[...] = cki_ref[0].astype(BF16)

    @pl.when(c == n_cache)
    def _():
        k_ref[...] = jnp.zeros(k_ref.shape, BF16)
        vt_ref[...] = jnp.zeros(vt_ref.shape, BF16)
        ki_ref[...] = jnp.zeros(ki_ref.shape, BF16)
        k_ref[0:CHUNK, :] = nk_ref[...].astype(BF16)
        ki_ref[0:CHUNK, :] = nki_ref[:, :IDX_DIM].astype(BF16)
        pad = jnp.zeros((LANES - CHUNK, A_HEAD_DIM), F32)
        for h in range(A_HEADS):
            sl = slice(h * A_HEAD_DIM, (h + 1) * A_HEAD_DIM)
            vt_ref[0, sl, :] = jnp.concatenate([nv_ref[:, sl], pad], axis=0).T.astype(BF16)


def _sample_keys(l, cache_k, cache_v, cache_ki, k_new, v_new, ki_new, *, new_row0, n_keys, rows=512):
    _, bs, past = cache_k.shape[:3]
    width = k_new.shape[1]
    n_cache = past // rows
    nb = -(-n_keys // rows)
    new_blk = new_row0 // CHUNK

    def cached(block_rows, w):
        return pl.BlockSpec((None, 1, block_rows, w), lambda b, c: (l, b, jnp.minimum(c, n_cache - 1), 0))

    def new(w):
        return pl.BlockSpec((CHUNK, w), lambda b, c: (new_blk + b, 0))

    kv_shape = cache_k.shape[:2] + (past * A_HEADS, A_HEAD_DIM)

    k16, v_t, ki16 = pl.pallas_call(
        _sample_keys_kernel,
        out_shape=(jax.ShapeDtypeStruct((bs, n_keys, width), BF16),
                   jax.ShapeDtypeStruct((bs, n_keys // LANES, width, LANES), BF16),
                   jax.ShapeDtypeStruct((bs, n_keys, IDX_DIM), BF16)),
        grid=(bs, nb),
        in_specs=[cached(rows * A_HEADS, A_HEAD_DIM), cached(rows * A_HEADS, A_HEAD_DIM), cached(rows, IDX_DIM),
                  new(width), new(width), new(LANES)],
        out_specs=(pl.BlockSpec((None, rows, width), lambda b, c: (b, c, 0)),
                   pl.BlockSpec((None, rows // LANES, width, LANES), lambda b, c: (b, c, 0, 0)),
                   pl.BlockSpec((None, rows, IDX_DIM), lambda b, c: (b, c, 0))),
        compiler_params=_cparams(("parallel", "arbitrary")),
        name="sample_keys",
    )(cache_k.reshape(kv_shape), cache_v.reshape(kv_shape), cache_ki, k_new, v_new, ki_new)
    return (k16.reshape(bs * n_keys, width), v_t.reshape(bs * (n_keys // LANES), width, LANES),
            ki16.reshape(bs * n_keys, IDX_DIM))


def _t5_bucket(rel):
    half = T5_BUCKETS // 2
    exact = half // 2
    n = jnp.abs(rel)
    large = exact + (jnp.log(jnp.maximum(n, 1).astype(F32) / exact)
                     / math.log(T5_MAX_DIST / exact) * (half - exact)).astype(jnp.int32)
    large = jnp.minimum(large, half - 1)
    return jnp.where(rel > 0, half, 0) + jnp.where(n < exact, n, large)


def _bias_tables(t5_bias):
    a = jnp.arange(WN, dtype=jnp.int32)[:, None]
    t = jnp.arange(QT, dtype=jnp.int32)[None, :]
    bucket = _t5_bucket(a - (WN - QT) - t)
    onehot = (bucket[:, :, None] == jnp.arange(T5_BUCKETS, dtype=jnp.int32)).astype(F32)
    near = jnp.einsum("atk,kh->hat", onehot, t5_bias.astype(F32), precision=lax.Precision.HIGHEST)
    far = t5_bias[_t5_bucket(jnp.int32(-2 * CHUNK - 1))].astype(F32)
    ok = (a // CHUNK - (WN - QT) // CHUNK <= t // CHUNK).astype(jnp.int32)
    return (near - far[:, None, None]) * math.log2(math.e), ok


def _rope_tables(pos):
    half = RET_HEAD_DIM // 2
    inv = ROPE_BASE ** (-jnp.arange(half, dtype=F32) / half)
    ang = pos.astype(F32)[:, None] * inv[None, :]
    cos, sin = jnp.cos(ang), jnp.sin(ang)
    return jnp.concatenate([cos, cos], axis=-1), jnp.concatenate([-sin, sin], axis=-1)


def _decay_tables():
    n = CHUNK
    log_g = jnp.log(1.0 - 2.0 ** (-5.0 - jnp.arange(RET_HEADS, dtype=F32)))
    i = jnp.arange(n, dtype=F32)
    diff = i[:, None] - i[None, :]
    dmat = jnp.where(diff >= 0, jnp.exp(jnp.maximum(diff, 0.0)[None] * log_g[:, None, None]), 0.0)
    cross = jnp.exp((i[None, :] + 1.0) * log_g[:, None])
    kdec = jnp.exp((n - 1.0 - i)[None, :] * log_g[:, None])
    gn = jnp.exp(n * log_g)
    wide = (RET_HEADS, n, RET_HEAD_DIM)
    return (dmat, jnp.broadcast_to(cross[:, :, None], wide), jnp.broadcast_to(kdec[:, :, None], wide),
            jnp.broadcast_to(gn[:, None, None], (RET_HEADS, 1, RET_HEAD_DIM)))


def kernel(x_prompt, x_sample, cache_k, cache_v, cache_ki, cache_pool, state_ret, meta_tokens, ln_in_g, ln_in_b, w_in, t5_bias, pool_w, pool_scale, w_branch, w_gate, b_gate, w_out, ln1_g, ln1_b, ln2_g, ln2_b, ffn_w_gate, ffn_w_up, ffn_w_down, moe_w_router, moe_b_router, moe_w_gate, moe_w_up, moe_w_down):
    bp, seq, d = x_prompt.shape
    bs, ts, _ = x_sample.shape
    past = cache_k.shape[2]
    t_real = seq + N_META
    tp = -(-(t_real + 2 * CHUNK) // KEY_TILE) * KEY_TILE
    front = tp - t_real
    assert front % CHUNK == CHUNK - N_META and ts == CHUNK and past % CHUNK == 0
    mp, ms = bp * tp, bs * ts
    m = mp + ms
    assert mp % TM == 0 and ms % TM == 0
    ksel_p = min(TOPK_MAX, seq // 4)
    ksel_s = min(TOPK_MAX, (past + ts) // 4)
    n_keys_s = -(-(past + QT) // KEY_TILE) * KEY_TILE
    assert tp % QT == 0 and front >= WN - QT and past % QT == 0 and past + QT >= WN

    assert front + N_META == LN_TILE and seq % LN_TILE == 0 and ms % LN_TILE == 0
    head = jnp.concatenate([jnp.zeros((front, d), F32), meta_tokens.astype(F32)], axis=0)
    x32, xb = _input_norm(head, x_prompt, x_sample.reshape(ms, d), ln_in_g, ln_in_b)

    bias_near, near_ok = _bias_tables(t5_bias)
    cos_p, sin_p = _rope_tables(jnp.arange(tp, dtype=jnp.int32) - (front + N_META))
    cos_s, sin_s = _rope_tables(past + jnp.arange(ts, dtype=jnp.int32))
    decay = _decay_tables()
    zero_state = jnp.zeros((bp, RET_HEADS, RET_HEAD_DIM, RET_HEAD_DIM), F32)
    w_in_t = jnp.swapaxes(w_in, 1, 2)
    assert TAIL_OFF % SUBLANES == 0
    row_in_seq = jnp.arange(mp, dtype=jnp.int32) % tp
    token_rows = jnp.concatenate([row_in_seq >= front, jnp.ones((ms,), jnp.bool_)])

    outs = {name: [] for name in ("kp", "vp", "kip", "poolp", "retp", "ks", "vs", "kis", "pools", "rets")}
    bw = BRANCH_WIDTH
    for l in range(DEPTH):
        q16, q_t = _proj(xb, w_in_t, (l,), 0, nat16=True, trans=True, out_scale=ATTN_LOG2_SCALE, name="proj_q")
        k_new, k16 = _proj(xb, w_in_t, (l,), 1, nat32=True, nat16=True, name="proj_k")
        v_new, v_t = _proj(xb, w_in_t, (l,), 2, nat32=True, trans=True, name="proj_v")
        qi16, qi_t = _proj(xb, w_in_t, (l,), 3, nat16=True, trans=True, name="proj_qi")
        p2, ki16, w_t = _proj_idx(xb, w_in_t, (l,), 4 * bw // LANES)
        p3 = _matmul(xb, w_in_t, (l,), TAIL_OFF, 5, bw, name="proj_tail", w_transposed=True)
        ki_new = p2[:, :IDX_DIM]
        qi_t = qi_t.reshape(m // LANES, IDX_HEADS, IDX_DIM, LANES)
        v_t = v_t.reshape(m // LANES, bw, LANES)

        def sample_queries_t(a16, heads):
            a = jnp.pad(a16[mp:].reshape(bs, ts, heads, bw // heads), ((0, 0), (0, QT - ts), (0, 0), (0, 0)))
            return jnp.transpose(a, (0, 2, 3, 1))

        w_t_s = jnp.transpose(jnp.pad(p2[mp:, IDX_DIM:IDX_DIM + IDX_HEADS].reshape(bs, ts, IDX_HEADS),
                                      ((0, 0), (0, QT - ts), (0, 0))), (0, 2, 1))
        k16_s, v_t_s, ki16_s = _sample_keys(l, cache_k, cache_v, cache_ki, k_new, v_new, p2, new_row0=mp,
                                            n_keys=n_keys_s)

        branch_buf = jnp.zeros((m, bw), BF16)
        oa = _dsa(q_t, qi_t, w_t, k16, v_t, ki16, bias_near, near_ok, branch_buf, batch=bp, nb=tp // QT, n_keys=tp,
                  out_rows=QT, out_base=0, first_real=front, qb0=0, skip_below=max(1, front // QT), k_sel=ksel_p)
        oa = _dsa(sample_queries_t(q16, A_HEADS), sample_queries_t(qi16, IDX_HEADS), w_t_s, k16_s, v_t_s, ki16_s,
                  bias_near, near_ok, oa, batch=bs, nb=1, n_keys=n_keys_s, out_rows=ts, out_base=mp // ts,
                  first_real=0, qb0=past // CHUNK, skip_below=0, k_sel=ksel_s)

        tb_p = 256
        ob = _pool(p3, 0, p3, 0, pool_w[l], pool_scale[l], branch_buf, batch=bp, rows=tp, tb=tb_p, row0=0,
                   halo_map=lambda b, i: jnp.maximum((b * tp + i * tb_p) // HALO - 1, 0), first_valid=front)
        pool_hist = jnp.pad(cache_pool[l], ((0, 0), (HALO - POOL_PAST, 0), (0, 0))).reshape(bs * HALO, bw)
        ob = _pool(p3, 0, pool_hist, 0, pool_w[l], pool_scale[l], ob, batch=bs, rows=ts, tb=ts, row0=mp,
                   halo_map=lambda b, i: b, first_valid=-POOL_PAST)

        oc, ret_p = _retention(p3, zero_state, cos_p, sin_p, decay, branch_buf, batch=bp, rows=tp, row0=0,
                               first_valid=front)
        oc, ret_s = _retention(p3, state_ret[l].astype(F32), cos_s, sin_s, decay, oc, batch=bs, rows=ts, row0=mp,
                               first_valid=0)

        merged = _merge(l, xb, oa, ob, oc, w_branch, w_gate, b_gate)
        x32, xb = _matmul_norm(merged, w_out, (l,), x32, ln1_g[l], ln1_b[l])

        if l % 2 == 0:
            h = _gate_up(xb, ffn_w_gate, ffn_w_up, (l // 2,))
            f = _matmul(h, ffn_w_down, (l // 2,), 0, d // 512, 512, name="ffn_down")
        else:
            gates, experts = _router(x32, moe_w_router[l // 2], moe_b_router[l // 2])
            src, pos, tile_expert, n_used = _dispatch_plan(experts, MOE_TILE, token_rows)
            x_sorted = _gather_rows(x32, src, n_used, MOE_TILE)
            y_sorted = _moe_experts(x_sorted, tile_expert, n_used, moe_w_gate, moe_w_up, moe_w_down, l // 2, MOE_TILE)
            if l + 1 == DEPTH:
                f = None
            else:
                f = _combine_rows(y_sorted, pos, gates, MOE_TILE)
        if l + 1 < DEPTH:
            x32, xb = _layer_norm(x32, f, ln2_g[l], ln2_b[l])
        elif f is None:
            y_prompt, y_sample = _combine_norm_out(y_sorted, pos, gates, x32, ln2_g[l], ln2_b[l], bp, seq)
        else:
            y_prompt, y_sample = _output_norm(x32, f, ln2_g[l], ln2_b[l], bp, seq)

        def prompt_rows(a, width):
            return a[:mp].reshape(bp, tp, width)[:, front:]

        u = p3[:, :bw]
        outs["kp"].append(prompt_rows(k_new, bw).reshape(bp, t_real, A_HEADS, A_HEAD_DIM))
        outs["vp"].append(prompt_rows(v_new, bw).reshape(bp, t_real, A_HEADS, A_HEAD_DIM))
        outs["kip"].append(prompt_rows(ki_new, IDX_DIM))
        outs["poolp"].append(prompt_rows(u, bw)[:, -POOL_PAST:])
        outs["retp"].append(ret_p)
        outs["ks"].append(k_new[mp:].reshape(bs, ts, A_HEADS, A_HEAD_DIM))
        outs["vs"].append(v_new[mp:].reshape(bs, ts, A_HEADS, A_HEAD_DIM))
        outs["kis"].append(ki_new[mp:].reshape(bs, ts, IDX_DIM))
        outs["pools"].append(u[mp:].reshape(bs, ts, bw)[:, -POOL_PAST:])
        outs["rets"].append(ret_s)

    return (y_prompt, y_sample.reshape(bs, ts, d)) + tuple(
        jnp.stack(outs[name]) for name in ("kp", "vp", "kip", "poolp", "retp", "ks", "vs", "kis", "pools", "rets"))
```

```python
import functools
import math

import jax
import jax.numpy as jnp
from jax import lax
from jax.experimental import pallas as pl
from jax.experimental.pallas import tpu as pltpu

F32 = jnp.float32
BF16 = jnp.bfloat16

D_MODEL = 2048
DEPTH = 2
CHUNK = 64
N_META = 16
BRANCH_WIDTH = D_MODEL // 2
A_HEADS = 8
A_HEAD_DIM = BRANCH_WIDTH // A_HEADS
IDX_HEADS = 16
IDX_DIM = 64
TOPK_MAX = 256
T5_BUCKETS = 32
T5_MAX_DIST = 128
POOL_WINDOWS = (2, 4, 8, 16)
POOL_GROUP = BRANCH_WIDTH // 4
POOL_PAST = 15
RET_HEADS = 8
RET_HEAD_DIM = BRANCH_WIDTH // RET_HEADS
ROPE_BASE = 10000.0
N_BRANCH = 3
D_FF = 11 * D_MODEL // 4
N_EXPERTS = 8
TOP_K = 2
ALPHA = (2 * DEPTH) ** 0.25
LN_EPS = 1e-5
IN_SPLITS = (BRANCH_WIDTH, BRANCH_WIDTH, BRANCH_WIDTH, IDX_HEADS * IDX_DIM, IDX_DIM, IDX_HEADS,
             BRANCH_WIDTH, BRANCH_WIDTH, BRANCH_WIDTH, BRANCH_WIDTH, BRANCH_WIDTH)
IN_WIDTH = sum(IN_SPLITS)
TAIL_OFF = 4 * BRANCH_WIDTH + IDX_DIM + IDX_HEADS

LANES = 128
SUBLANES = 8
HALO = 16
QT = 2 * CHUNK
WN = QT + 2 * CHUNK
KEY_TILE = 256
VMEM_LIMIT = 56 * 1024 * 1024
TM = 512
MOE_TILE = 512
LN_TILE = 256
DMA_PRIORITIES = 2
INT_MIN = -2 ** 31
NEG = -1e30
M_INIT = -1e20
TINY = 1e-30
ATTN_LOG2_SCALE = A_HEAD_DIM ** -0.5 * math.log2(math.e)


def _cparams(sem):
    return pltpu.CompilerParams(dimension_semantics=sem, vmem_limit_bytes=VMEM_LIMIT)


def _dot(a, b):
    return jnp.dot(a, b, preferred_element_type=F32)


def _dot_nt(a, b):
    return lax.dot_general(a, b, (((1,), (1,)), ((), ())), preferred_element_type=F32)


def _dot_tn(a, b):
    return lax.dot_general(a, b, (((0,), (0,)), ((), ())), preferred_element_type=F32)


def _ln(x, g_ref, b_ref):
    mu = jnp.mean(x, axis=-1, keepdims=True)
    xc = x - mu
    var = jnp.mean(xc * xc, axis=-1, keepdims=True)
    return xc * lax.rsqrt(var + LN_EPS) * g_ref[...] + b_ref[...]


def _ln_res_kernel(x_ref, y_ref, g_ref, b_ref, o32_ref, o16_ref):
    y = _ln(ALPHA * x_ref[...] + y_ref[...], g_ref, b_ref)
    o32_ref[...] = y
    o16_ref[...] = y.astype(BF16)


def _layer_norm(x, y, g, b, tm=LN_TILE):
    m, d = x.shape
    row = pl.BlockSpec((tm, d), lambda i: (i, 0))
    vec = pl.BlockSpec((1, d), lambda i: (0, 0))
    return pl.pallas_call(
        _ln_res_kernel,
        out_shape=(jax.ShapeDtypeStruct((m, d), F32), jax.ShapeDtypeStruct((m, d), BF16)),
        grid=(m // tm,),
        in_specs=[row, row, vec, vec],
        out_specs=(row, row),
        compiler_params=_cparams(("parallel",)),
        name="layer_norm",
    )(x, y, g.reshape(1, d), b.reshape(1, d))


def _seq_tile_maps(bp, seq, tm):
    tiles_per_seq = seq // tm + 1
    n_prompt_tiles = bp * tiles_per_seq

    def prompt_map(i):
        in_prompt = i < n_prompt_tiles
        b = jnp.minimum(i // tiles_per_seq, bp - 1)
        r = jnp.where(in_prompt, jnp.maximum(i % tiles_per_seq - 1, 0), seq // tm - 1)
        return (b, r, 0)

    def sample_map(i):
        return (jnp.maximum(i - n_prompt_tiles, 0), 0)

    return tiles_per_seq, n_prompt_tiles, prompt_map, sample_map


def _ln_in_kernel(head_ref, xp_ref, xs_ref, g_ref, b_ref, o32_ref, o16_ref, *, tiles_per_seq, n_prompt_tiles):
    i = pl.program_id(0)
    x = jnp.where(i >= n_prompt_tiles, xs_ref[...],
                  jnp.where(i % tiles_per_seq == 0, head_ref[...], xp_ref[...]))
    y = _ln(x, g_ref, b_ref)
    o32_ref[...] = y
    o16_ref[...] = y.astype(BF16)


def _input_norm(head, x_prompt, x_sample, g, b, tm=LN_TILE):
    bp, seq, d = x_prompt.shape
    ms = x_sample.shape[0]
    tiles_per_seq, n_prompt_tiles, prompt_map, sample_map = _seq_tile_maps(bp, seq, tm)
    m = n_prompt_tiles * tm + ms
    row = pl.BlockSpec((tm, d), lambda i: (i, 0))
    vec = pl.BlockSpec((1, d), lambda i: (0, 0))
    return pl.pallas_call(
        functools.partial(_ln_in_kernel, tiles_per_seq=tiles_per_seq, n_prompt_tiles=n_prompt_tiles),
        out_shape=(jax.ShapeDtypeStruct((m, d), F32), jax.ShapeDtypeStruct((m, d), BF16)),
        grid=(m // tm,),
        in_specs=[pl.BlockSpec((tm, d), lambda i: (0, 0)), pl.BlockSpec((None, tm, d), prompt_map),
                  pl.BlockSpec((tm, d), sample_map), vec, vec],
        out_specs=(row, row),
        compiler_params=_cparams(("parallel",)),
        name="input_norm",
    )(head, x_prompt, x_sample, g.reshape(1, d), b.reshape(1, d))


def _ln_out_kernel(x_ref, y_ref, g_ref, b_ref, yp_ref, ys_ref, *, tiles_per_seq, n_prompt_tiles):
    i = pl.program_id(0)
    out = _ln(ALPHA * x_ref[...] + y_ref[...], g_ref, b_ref)

    @pl.when(jnp.logical_and(i < n_prompt_tiles, i % tiles_per_seq != 0))
    def _():
        yp_ref[...] = out

    @pl.when(i >= n_prompt_tiles)
    def _():
        ys_ref[...] = out


def _output_norm(x, y, g, b, bp, seq, tm=LN_TILE):
    m, d = x.shape
    tiles_per_seq, n_prompt_tiles, prompt_map, sample_map = _seq_tile_maps(bp, seq, tm)
    ms = m - n_prompt_tiles * tm
    row = pl.BlockSpec((tm, d), lambda i: (i, 0))
    vec = pl.BlockSpec((1, d), lambda i: (0, 0))
    return pl.pallas_call(
        functools.partial(_ln_out_kernel, tiles_per_seq=tiles_per_seq, n_prompt_tiles=n_prompt_tiles),
        out_shape=(jax.ShapeDtypeStruct((bp, seq, d), F32), jax.ShapeDtypeStruct((ms, d), F32)),
        grid=(m // tm,),
        in_specs=[row, row, vec, vec],
        out_specs=(pl.BlockSpec((None, tm, d), prompt_map), pl.BlockSpec((tm, d), sample_map)),
        compiler_params=_cparams(("arbitrary",)),
        name="output_norm",
    )(x, y, g.reshape(1, d), b.reshape(1, d))


def _mm_kernel(x_ref, w_ref, o_ref, wb_ref, *, w_rows_are_outputs):
    @pl.when(pl.program_id(1) == 0)
    def _():
        if w_rows_are_outputs:
            wb_ref[...] = w_ref[...].reshape(w_ref.shape[-2:]).T.astype(BF16)
        else:
            wb_ref[...] = w_ref[...].astype(BF16)

    o_ref[...] = _dot(x_ref[...], wb_ref[...]).astype(o_ref.dtype)


def _matmul(x, w, lead, col0, n_tiles, tn, name="matmul", w_transposed=False):
    m, k = x.shape
    nl = len(lead)
    if w_transposed:
        w_spec = pl.BlockSpec((pl.Element(1),) * nl + (pl.Element(tn), pl.Element(k)),
                              lambda j, i: tuple(lead) + (pl.multiple_of(j * tn + col0, SUBLANES), 0))
    else:
        w_spec = pl.BlockSpec((None,) * nl + (k, tn), lambda j, i: tuple(lead) + (0, j + col0))
    x_spec = pl.BlockSpec((TM, k), lambda j, i: (i, 0))
    o_spec = pl.BlockSpec((TM, tn), lambda j, i: (i, j))
    return pl.pallas_call(
        functools.partial(_mm_kernel, w_rows_are_outputs=w_transposed),
        out_shape=jax.ShapeDtypeStruct((m, n_tiles * tn), F32),
        grid=(n_tiles, m // TM),
        in_specs=[x_spec, w_spec],
        out_specs=o_spec,
        scratch_shapes=[pltpu.VMEM((k, tn), BF16)],
        compiler_params=_cparams(("arbitrary", "arbitrary")),
        name=name,
    )(x, w)


def _matmul_norm_kernel(a_ref, w_ref, x_ref, g_ref, b_ref, o32_ref, o16_ref, wb_ref):
    @pl.when(pl.program_id(0) == 0)
    def _():
        wb_ref[...] = w_ref[...].astype(BF16)

    y = _ln(ALPHA * x_ref[...] + _dot(a_ref[...], wb_ref[...]), g_ref, b_ref)
    o32_ref[...] = y
    o16_ref[...] = y.astype(BF16)


def _matmul_norm(a, w, lead, x, g, b, tm=LN_TILE):
    m, k = a.shape
    d = x.shape[1]
    nl = len(lead)
    row = pl.BlockSpec((tm, d), lambda i: (i, 0))
    vec = pl.BlockSpec((1, d), lambda i: (0, 0))
    return pl.pallas_call(
        _matmul_norm_kernel,
        out_shape=(jax.ShapeDtypeStruct((m, d), F32), jax.ShapeDtypeStruct((m, d), BF16)),
        grid=(m // tm,),
        in_specs=[pl.BlockSpec((tm, k), lambda i: (i, 0)),
                  pl.BlockSpec((None,) * nl + (k, d), lambda i: tuple(lead) + (0, 0), pipeline_mode=pl.Buffered(1)),
                  row, vec, vec],
        out_specs=(row, row),
        scratch_shapes=[pltpu.VMEM((k, d), BF16)],
        compiler_params=_cparams(("arbitrary",)),
        name="matmul_norm",
    )(a, w, x, g.reshape(1, d), b.reshape(1, d))


def _store_lane_tiles_t(res, out_ref):
    rows, width = res.shape
    for r in range(rows // LANES):
        for c in range(width // LANES):
            tile = res[r * LANES:(r + 1) * LANES, c * LANES:(c + 1) * LANES]
            out_ref[r, c] = tile.T.astype(out_ref.dtype)


def _proj_kernel(x_ref, w_ref, *refs, nat32, nat16, trans, out_scale):
    outs, wb_ref = list(refs[:-1]), refs[-1]

    @pl.when(pl.program_id(0) == 0)
    def _():
        wb_ref[...] = w_ref[...].T.astype(BF16)

    res = _dot(x_ref[...], wb_ref[...])
    if out_scale is not None:
        res = res * out_scale
    if nat32:
        outs.pop(0)[...] = res
    if nat16:
        outs.pop(0)[...] = res.astype(BF16)
    if trans:
        _store_lane_tiles_t(res, outs.pop(0))


def _proj(x, w, lead, col0, *, nat32=False, nat16=False, trans=False, out_scale=None, tm=TM, name="proj"):
    m, k = x.shape
    bw = BRANCH_WIDTH
    nl = len(lead)
    shapes, specs = [], []
    row_spec = pl.BlockSpec((tm, bw), lambda i: (i, 0))
    if nat32:
        shapes.append(jax.ShapeDtypeStruct((m, bw), F32)); specs.append(row_spec)
    if nat16:
        shapes.append(jax.ShapeDtypeStruct((m, bw), BF16)); specs.append(row_spec)
    if trans:
        shapes.append(jax.ShapeDtypeStruct((m // LANES, bw // LANES, LANES, LANES), BF16))
        specs.append(pl.BlockSpec((tm // LANES, bw // LANES, LANES, LANES), lambda i: (i, 0, 0, 0)))
    return pl.pallas_call(
        functools.partial(_proj_kernel, nat32=nat32, nat16=nat16, trans=trans, out_scale=out_scale),
        out_shape=tuple(shapes),
        grid=(m // tm,),
        in_specs=[pl.BlockSpec((tm, k), lambda i: (i, 0)),
                  pl.BlockSpec((None,) * nl + (bw, k), lambda i: tuple(lead) + (col0, 0))],
        out_specs=tuple(specs),
        scratch_shapes=[pltpu.VMEM((k, bw), BF16)],
        compiler_params=_cparams(("arbitrary",)),
        name=name,
    )(x, w)


def _proj_idx_kernel(x_ref, w_ref, nat_ref, ki_ref, wt_ref, wb_ref):
    @pl.when(pl.program_id(0) == 0)
    def _():
        wb_ref[...] = w_ref[...].T.astype(BF16)

    res = _dot(x_ref[...], wb_ref[...])
    nat_ref[...] = res
    ki_ref[...] = res[:, :IDX_DIM].astype(BF16)
    for r in range(res.shape[0] // LANES):
        wt_ref[r] = res[r * LANES:(r + 1) * LANES, :].T[IDX_DIM:IDX_DIM + IDX_HEADS, :]


def _proj_idx(x, w, lead, col_block, tm=TM):
    m, k = x.shape
    nl = len(lead)
    return pl.pallas_call(
        _proj_idx_kernel,
        out_shape=(jax.ShapeDtypeStruct((m, LANES), F32), jax.ShapeDtypeStruct((m, IDX_DIM), BF16),
                   jax.ShapeDtypeStruct((m // LANES, IDX_HEADS, LANES), F32)),
        grid=(m // tm,),
        in_specs=[pl.BlockSpec((tm, k), lambda i: (i, 0)),
                  pl.BlockSpec((None,) * nl + (LANES, k), lambda i: tuple(lead) + (col_block, 0))],
        out_specs=(pl.BlockSpec((tm, LANES), lambda i: (i, 0)), pl.BlockSpec((tm, IDX_DIM), lambda i: (i, 0)),
                   pl.BlockSpec((tm // LANES, IDX_HEADS, LANES), lambda i: (i, 0, 0))),
        scratch_shapes=[pltpu.VMEM((k, LANES), BF16)],
        compiler_params=_cparams(("arbitrary",)),
        name="proj_idx",
    )(x, w)


def _gate_up_kernel(x_ref, wg_ref, wu_ref, o_ref, wgb_ref, wub_ref):
    @pl.when(pl.program_id(1) == 0)
    def _():
        wgb_ref[...] = wg_ref[...].astype(BF16)
        wub_ref[...] = wu_ref[...].astype(BF16)

    x = x_ref[...]
    g = _dot(x, wgb_ref[...])
    u = _dot(x, wub_ref[...])
    o_ref[...] = (g * jax.nn.sigmoid(g) * u).astype(o_ref.dtype)


def _gate_up(x, wg, wu, lead, tn=512):
    m, k = x.shape
    n = wg.shape[-1]
    nl = len(lead)
    w_spec = pl.BlockSpec((None,) * nl + (k, tn), lambda j, i: tuple(lead) + (0, j))
    x_spec = pl.BlockSpec((TM, k), lambda j, i: (i, 0))
    return pl.pallas_call(
        _gate_up_kernel,
        out_shape=jax.ShapeDtypeStruct((m, n), BF16),
        grid=(n // tn, m // TM),
        in_specs=[x_spec, w_spec, w_spec],
        out_specs=pl.BlockSpec((TM, tn), lambda j, i: (i, j)),
        scratch_shapes=[pltpu.VMEM((k, tn), BF16), pltpu.VMEM((k, tn), BF16)],
        compiler_params=_cparams(("arbitrary", "arbitrary")),
        name="gate_up",
    )(x, wg, wu)


def _merge_kernel(x_ref, oa_ref, ob_ref, oc_ref, wg0_ref, wg1_ref, wg2_ref, wb0_ref, wb1_ref, wb2_ref,
                  bg0_ref, bg1_ref, bg2_ref, o_ref, wgb_ref, wbb_ref):
    wg_refs = (wg0_ref, wg1_ref, wg2_ref)
    wb_refs = (wb0_ref, wb1_ref, wb2_ref)

    @pl.when(pl.program_id(1) == 0)
    def _():
        for n in range(N_BRANCH):
            wgb_ref[n] = wg_refs[n][...].astype(BF16)
            wbb_ref[n] = wb_refs[n][...].astype(BF16)

    x = x_ref[...]
    acc = None
    for n, (o_in, bg) in enumerate(zip((oa_ref, ob_ref, oc_ref), (bg0_ref, bg1_ref, bg2_ref))):
        gate = jax.nn.sigmoid(_dot(x, wgb_ref[n]) + bg[...])
        term = gate * _dot(o_in[...], wbb_ref[n])
        acc = term if acc is None else acc + term
    o_ref[...] = acc.astype(o_ref.dtype)


def _merge(l, x, oa, ob, oc, w_branch, w_gate, b_gate, tn=256):
    m, d = x.shape
    w = oa.shape[1]
    nt = d // tn
    x_spec = pl.BlockSpec((TM, d), lambda j, i: (i, 0))
    o_in_spec = pl.BlockSpec((TM, w), lambda j, i: (i, 0))
    wg_specs = [pl.BlockSpec((None, d, tn), lambda j, i, n=n: (l, 0, n * nt + j)) for n in range(N_BRANCH)]
    wb_specs = [pl.BlockSpec((None, None, w, tn), lambda j, i, n=n: (l, n, 0, j)) for n in range(N_BRANCH)]
    bg_specs = [pl.BlockSpec((None, 1, tn), lambda j, i, n=n: (l, 0, n * nt + j)) for n in range(N_BRANCH)]
    return pl.pallas_call(
        _merge_kernel,
        out_shape=jax.ShapeDtypeStruct((m, d), BF16),
        grid=(nt, m // TM),
        in_specs=[x_spec, o_in_spec, o_in_spec, o_in_spec] + wg_specs + wb_specs + bg_specs,
        out_specs=pl.BlockSpec((TM, tn), lambda j, i: (i, j)),
        scratch_shapes=[pltpu.VMEM((N_BRANCH, d, tn), BF16), pltpu.VMEM((N_BRANCH, w, tn), BF16)],
        compiler_params=_cparams(("arbitrary", "arbitrary")),
        name="merge",
    )(x, oa, ob, oc, w_gate, w_gate, w_gate, w_branch, w_branch, w_branch,
      b_gate.reshape(DEPTH, 1, N_BRANCH * d), b_gate.reshape(DEPTH, 1, N_BRANCH * d),
      b_gate.reshape(DEPTH, 1, N_BRANCH * d))


def _split_bf16(a):
    hi = a.astype(BF16)
    lo = (a - hi.astype(F32)).astype(BF16)
    return hi, lo


def _router_kernel(x_ref, w_ref, b_ref, gate_ref, expert_ref):
    xh, xl = _split_bf16(x_ref[...])
    wh, wl = _split_bf16(w_ref[...])
    logits = _dot(xh, wh) + (_dot(xh, wl) + _dot(xl, wh)) + b_ref[...]
    lane = lax.broadcasted_iota(jnp.int32, logits.shape, 1)
    logits = jnp.where(lane < N_EXPERTS, logits, -jnp.inf)
    m1 = jnp.max(logits, axis=-1, keepdims=True)
    i1 = jnp.min(jnp.where(logits == m1, lane, LANES), axis=-1, keepdims=True)
    rest = jnp.where(lane == i1, -jnp.inf, logits)
    m2 = jnp.max(rest, axis=-1, keepdims=True)
    i2 = jnp.min(jnp.where(rest == m2, lane, LANES), axis=-1, keepdims=True)
    e = jnp.exp(m2 - m1)
    p1 = 1.0 / (1.0 + e)
    p2 = e / (1.0 + e)
    gate_ref[...] = jnp.where(lane == 0, p1, jnp.where(lane == 1, p2, 0.0))
    expert_ref[...] = jnp.where(lane == 0, i1, jnp.where(lane == 1, i2, 0))


def _router(x32, w_r, b_r, tm=256):
    m, d = x32.shape
    w_pad = jnp.pad(w_r, ((0, 0), (0, LANES - N_EXPERTS)))
    b_pad = jnp.pad(b_r, (0, LANES - N_EXPERTS)).reshape(1, LANES)
    out_spec = pl.BlockSpec((tm, LANES), lambda i: (i, 0))
    gates, experts = pl.pallas_call(
        _router_kernel,
        out_shape=(jax.ShapeDtypeStruct((m, LANES), F32), jax.ShapeDtypeStruct((m, LANES), jnp.int32)),
        grid=(m // tm,),
        in_specs=[pl.BlockSpec((tm, d), lambda i: (i, 0)), pl.BlockSpec((d, LANES), lambda i: (0, 0)),
                  pl.BlockSpec((1, LANES), lambda i: (0, 0))],
        out_specs=(out_spec, out_spec),
        compiler_params=_cparams(("parallel",)),
        name="router",
    )(x32, w_pad, b_pad)
    return gates[:, :TOP_K], experts[:, :TOP_K]


def _dispatch_plan(experts, tile, token_rows):
    m = experts.shape[0]
    n_assign = TOP_K * m
    n_tiles = n_assign // tile + N_EXPERTS
    e = experts.reshape(n_assign)
    live = jnp.repeat(token_rows, TOP_K)
    onehot = jnp.logical_and(e[:, None] == jnp.arange(N_EXPERTS, dtype=jnp.int32)[None, :],
                             live[:, None]).astype(jnp.int32)
    rank = jnp.sum((jnp.cumsum(onehot, axis=0) - onehot) * onehot, axis=1)
    tiles_per = (jnp.sum(onehot, axis=0) + tile - 1) // tile
    tile_end = jnp.cumsum(tiles_per)
    pos = ((tile_end - tiles_per)[e] * tile + rank).astype(jnp.int32)
    slot = jnp.where(live, pos, n_tiles * tile)
    src = jnp.zeros((n_tiles * tile,), jnp.int32).at[slot].set(jnp.arange(n_assign, dtype=jnp.int32) // TOP_K,
                                                               mode="drop")
    pos = jnp.where(live, pos, 0)
    n_used = tile_end[-1:]
    t_idx = jnp.minimum(jnp.arange(n_tiles, dtype=jnp.int32), n_used - 1)
    tile_expert = jnp.sum((t_idx[:, None] >= tile_end[None, :]).astype(jnp.int32), axis=1)
    return src, pos, tile_expert.astype(jnp.int32), n_used.astype(jnp.int32)


def _issue_rows(idx_ref, first, count, stride, src_hbm, dst, sem):
    def body(pair, carry):
        for priority in range(DMA_PRIORITIES):
            i = pair * DMA_PRIORITIES + priority
            pltpu.make_async_copy(src_hbm.at[pl.ds(idx_ref[first + i * stride], 1), :], dst.at[pl.ds(i, 1), :],
                                  sem).start(priority=priority)
        return carry

    lax.fori_loop(0, count // DMA_PRIORITIES, body, 0, unroll=4)


def _wait_rows(count, src_hbm, dst, sem):
    pltpu.make_async_copy(src_hbm.at[pl.ds(0, count), :], dst, sem).wait()


def _gather_kernel(src_ref, nu_ref, x_hbm, o_ref, buf, sem, *, tile):
    t = pl.program_id(0)

    @pl.when(t == 0)
    def _():
        _issue_rows(src_ref, 0, tile, 1, x_hbm, buf.at[0], sem.at[0])

    @pl.when(t + 1 < nu_ref[0])
    def _():
        nxt = (t + 1) % 2
        _issue_rows(src_ref, (t + 1) * tile, tile, 1, x_hbm, buf.at[nxt], sem.at[nxt])

    @pl.when(t < nu_ref[0])
    def _():
        cur = t % 2
        _wait_rows(tile, x_hbm, buf.at[cur], sem.at[cur])
        o_ref[...] = buf[cur].astype(o_ref.dtype)

    @pl.when(t >= nu_ref[0])
    def _():
        o_ref[...] = jnp.zeros(o_ref.shape, o_ref.dtype)


def _gather_rows(x32, src, n_used, tile):
    d = x32.shape[1]
    n_rows = src.shape[0]
    return pl.pallas_call(
        functools.partial(_gather_kernel, tile=tile),
        out_shape=jax.ShapeDtypeStruct((n_rows, d), BF16),
        grid_spec=pltpu.PrefetchScalarGridSpec(
            num_scalar_prefetch=2, grid=(n_rows // tile,),
            in_specs=[pl.BlockSpec(memory_space=pl.ANY)],
            out_specs=pl.BlockSpec((tile, d), lambda t, src, nu: (t, 0)),
            scratch_shapes=[pltpu.VMEM((2, tile, d), F32), pltpu.SemaphoreType.DMA((2,))]),
        compiler_params=_cparams(("arbitrary",)),
        name="moe_gather",
    )(src, n_used, x32)


def _combine_kernel(pos_ref, y_hbm, g_ref, o_ref, buf, sem, *, tile):
    t = pl.program_id(0)

    def issue(tile_idx, slot):
        for s in range(TOP_K):
            _issue_rows(pos_ref, tile_idx * tile * TOP_K + s, tile, TOP_K, y_hbm, buf.at[slot, s], sem.at[slot])

    @pl.when(t == 0)
    def _():
        issue(0, 0)

    @pl.when(t + 1 < pl.num_programs(0))
    def _():
        issue(t + 1, (t + 1) % 2)

    cur = t % 2
    for s in range(TOP_K):
        _wait_rows(tile, y_hbm, buf.at[cur, s], sem.at[cur])
    gates = g_ref[...]
    total = buf[cur, 0] * gates[:, 0:1]
    for s in range(1, TOP_K):
        total = total + buf[cur, s] * gates[:, s:s + 1]
    o_ref[...] = total


def _combine_rows(y_sorted, pos, gates, tile):
    m = gates.shape[0]
    d = y_sorted.shape[1]
    return pl.pallas_call(
        functools.partial(_combine_kernel, tile=tile),
        out_shape=jax.ShapeDtypeStruct((m, d), F32),
        grid_spec=pltpu.PrefetchScalarGridSpec(
            num_scalar_prefetch=1, grid=(m // tile,),
            in_specs=[pl.BlockSpec(memory_space=pl.ANY), pl.BlockSpec((tile, TOP_K), lambda t, pos: (t, 0))],
            out_specs=pl.BlockSpec((tile, d), lambda t, pos: (t, 0)),
            scratch_shapes=[pltpu.VMEM((2, TOP_K, tile, d), F32), pltpu.SemaphoreType.DMA((2,))]),
        compiler_params=_cparams(("arbitrary",)),
        name="moe_combine",
    )(pos, y_sorted, gates)


def _expert_changed(te_ref, t):
    return jnp.logical_or(t == 0, te_ref[t] != te_ref[jnp.maximum(t - 1, 0)])


def _moe_gate_up_kernel(te_ref, nu_ref, x_ref, wg_ref, wu_ref, o_ref, wgb_ref, wub_ref):
    t = pl.program_id(1)

    @pl.when(_expert_changed(te_ref, t))
    def _():
        wgb_ref[...] = wg_ref[...].astype(BF16)
        wub_ref[...] = wu_ref[...].astype(BF16)

    @pl.when(t < nu_ref[0])
    def _():
        x = x_ref[...]
        g = _dot(x, wgb_ref[...])
        u = _dot(x, wub_ref[...])
        o_ref[...] = (g * jax.nn.sigmoid(g) * u).astype(o_ref.dtype)

    @pl.when(t >= nu_ref[0])
    def _():
        o_ref[...] = jnp.zeros(o_ref.shape, o_ref.dtype)


def _moe_down_kernel(te_ref, nu_ref, h_ref, w_ref, o_ref, wb_ref):
    t = pl.program_id(1)

    @pl.when(_expert_changed(te_ref, t))
    def _():
        wb_ref[...] = w_ref[...].astype(BF16)

    @pl.when(t < nu_ref[0])
    def _():
        o_ref[...] = _dot(h_ref[...], wb_ref[...])

    @pl.when(t >= nu_ref[0])
    def _():
        o_ref[...] = jnp.zeros(o_ref.shape, o_ref.dtype)


def _moe_experts(x_sorted, tile_expert, n_used, w_gate, w_up, w_down, layer, tile, tn=512):
    r, d = x_sorted.shape
    ff = w_gate.shape[-1]
    n_tiles = r // tile

    def x_rows(width):
        return pl.BlockSpec((tile, width), lambda j, t, te, nu: (jnp.minimum(t, nu[0] - 1), 0))

    def w_cols(k):
        return pl.BlockSpec((None, None, k, tn), lambda j, t, te, nu: (layer, te[t], 0, j))

    h = pl.pallas_call(
        _moe_gate_up_kernel,
        out_shape=jax.ShapeDtypeStruct((r, ff), BF16),
        grid_spec=pltpu.PrefetchScalarGridSpec(
            num_scalar_prefetch=2, grid=(ff // tn, n_tiles),
            in_specs=[x_rows(d), w_cols(d), w_cols(d)],
            out_specs=pl.BlockSpec((tile, tn), lambda j, t, te, nu: (t, j)),
            scratch_shapes=[pltpu.VMEM((d, tn), BF16), pltpu.VMEM((d, tn), BF16)]),
        compiler_params=_cparams(("arbitrary", "arbitrary")),
        name="moe_gate_up",
    )(tile_expert, n_used, x_sorted, w_gate, w_up)
    return pl.pallas_call(
        _moe_down_kernel,
        out_shape=jax.ShapeDtypeStruct((r, d), F32),
        grid_spec=pltpu.PrefetchScalarGridSpec(
            num_scalar_prefetch=2, grid=(d // tn, n_tiles),
            in_specs=[x_rows(ff), w_cols(ff)],
            out_specs=pl.BlockSpec((tile, tn), lambda j, t, te, nu: (t, j)),
            scratch_shapes=[pltpu.VMEM((ff, tn), BF16)]),
        compiler_params=_cparams(("arbitrary", "arbitrary")),
        name="moe_down",
    )(tile_expert, n_used, h, w_down)


def _pool_kernel(u_ref, halo_ref, w_ref, sc_ref, _buf_ref, o_ref, ext_ref, *, tb, first_valid):
    row0 = pl.program_id(1) * tb
    r_cur = row0 + lax.broadcasted_iota(jnp.int32, (tb, 1), 0)
    r_halo = row0 - HALO + lax.broadcasted_iota(jnp.int32, (HALO, 1), 0)
    u = jnp.where(r_cur >= first_valid, u_ref[...], 0.0)
    ext_ref[0:HALO, :] = jnp.where(r_halo >= first_valid, halo_ref[...], 0.0)
    ext_ref[HALO:, :] = u
    seen = (r_cur - first_valid + 1).astype(F32)
    for g, win in enumerate(POOL_WINDOWS):
        c0, c1 = g * POOL_GROUP, (g + 1) * POOL_GROUP
        s = u[:, c0:c1]
        for back in range(1, win):
            s = s + ext_ref[HALO - back:HALO - back + tb, c0:c1]
        cnt = jnp.clip(seen, 1.0, float(win))
        diff = s / cnt - u[:, c0:c1]
        y = _dot(diff.astype(BF16), w_ref[g].astype(BF16))
        o_ref[:, c0:c1] = (y * sc_ref[:, c0:c1]).astype(o_ref.dtype)


def _pool(u2d, col_block, halo2d, halo_col_block, pool_w_l, pool_scale_l, out_buf, *, batch, rows, tb, row0,
          halo_map, first_valid):
    nb = rows // tb
    base = row0 // tb
    return pl.pallas_call(
        functools.partial(_pool_kernel, tb=tb, first_valid=first_valid),
        out_shape=jax.ShapeDtypeStruct(out_buf.shape, out_buf.dtype),
        grid=(batch, nb),
        in_specs=[
            pl.BlockSpec((tb, BRANCH_WIDTH), lambda b, i: (base + b * nb + i, col_block)),
            pl.BlockSpec((HALO, BRANCH_WIDTH), lambda b, i: (halo_map(b, i), halo_col_block)),
            pl.BlockSpec((len(POOL_WINDOWS), POOL_GROUP, POOL_GROUP), lambda b, i: (0, 0, 0)),
            pl.BlockSpec((1, BRANCH_WIDTH), lambda b, i: (0, 0)),
            pl.BlockSpec(memory_space=pl.ANY),
        ],
        out_specs=pl.BlockSpec((tb, BRANCH_WIDTH), lambda b, i: (base + b * nb + i, 0)),
        scratch_shapes=[pltpu.VMEM((HALO + tb, BRANCH_WIDTH), F32)],
        input_output_aliases={4: 0},
        compiler_params=_cparams(("parallel", "arbitrary")),
        name="pool",
    )(u2d, halo2d, pool_w_l, pool_scale_l.reshape(1, BRANCH_WIDTH), out_buf)


def _ret_kernel(q_ref, k_ref, v_ref, g_ref, cos_ref, sin_ref, dmat_ref, cross_ref, kdec_ref, gn_ref, s0_ref,
                _buf_ref, o_ref, s_out_ref, s_scr, *, first_valid):
    i = pl.program_id(1)

    @pl.when(i == 0)
    def _():
        s_scr[...] = s0_ref[...]

    rows = i * CHUNK + lax.broadcasted_iota(jnp.int32, (CHUNK, 1), 0)
    valid = rows >= first_valid
    cos = cos_ref[...]
    sin = sin_ref[...]
    rscale = RET_HEAD_DIM ** -0.5
    half = RET_HEAD_DIM // 2
    heads = range(RET_HEADS)
    sls = [slice(h * RET_HEAD_DIM, (h + 1) * RET_HEAD_DIM) for h in heads]

    def rotate(x):
        return x * cos + pltpu.roll(x, half, 1) * sin

    qr = [rotate(q_ref[:, sl]).astype(BF16) for sl in sls]
    kr = [jnp.where(valid, rotate(k_ref[:, sl]) * rscale, 0.0) for sl in sls]
    v = [jnp.where(valid, v_ref[:, sl], 0.0).astype(BF16) for sl in sls]
    state = [s_scr[h] for h in heads]
    inner = [_dot_nt(qr[h], kr[h].astype(BF16)) * dmat_ref[h] for h in heads]
    carried = [_dot(qr[h], state[h].astype(BF16)) * cross_ref[h] for h in heads]
    update = [_dot_tn((kr[h] * kdec_ref[h]).astype(BF16), v[h]) for h in heads]
    o = [_dot(inner[h].astype(BF16), v[h]) + carried[h] for h in heads]
    for h in heads:
        s_scr[h] = gn_ref[h] * state[h] + update[h]
    for h in heads:
        mu = jnp.mean(o[h], axis=-1, keepdims=True)
        oc = o[h] - mu
        var = jnp.mean(oc * oc, axis=-1, keepdims=True)
        gate = g_ref[:, sls[h]]
        o_ref[:, sls[h]] = (gate * jax.nn.sigmoid(gate) * (oc * lax.rsqrt(var + LN_EPS))).astype(o_ref.dtype)

    @pl.when(i == pl.num_programs(1) - 1)
    def _():
        s_out_ref[...] = s_scr[...]


def _retention(p3, s0, cos, sin, tabs, out_buf, *, batch, rows, row0, first_valid):
    nb = rows // CHUNK
    base = row0 // CHUNK
    dmat, cross, kdec, gn = tabs

    def col(c):
        return pl.BlockSpec((CHUNK, BRANCH_WIDTH), lambda b, i: (base + b * nb + i, c))

    tab_rows = pl.BlockSpec((CHUNK, RET_HEAD_DIM), lambda b, i: (i, 0))

    def full(a):
        return pl.BlockSpec(a.shape, lambda b, i: (0,) * a.ndim)

    state_spec = pl.BlockSpec((None, RET_HEADS, RET_HEAD_DIM, RET_HEAD_DIM), lambda b, i: (b, 0, 0, 0))
    return pl.pallas_call(
        functools.partial(_ret_kernel, first_valid=first_valid),
        out_shape=(jax.ShapeDtypeStruct(out_buf.shape, out_buf.dtype),
                   jax.ShapeDtypeStruct((batch, RET_HEADS, RET_HEAD_DIM, RET_HEAD_DIM), F32)),
        grid=(batch, nb),
        in_specs=[col(1), col(2), col(3), col(4), tab_rows, tab_rows, full(dmat), full(cross), full(kdec),
                  full(gn), state_spec, pl.BlockSpec(memory_space=pl.ANY)],
        out_specs=(pl.BlockSpec((CHUNK, BRANCH_WIDTH), lambda b, i: (base + b * nb + i, 0)), state_spec),
        scratch_shapes=[pltpu.VMEM((RET_HEADS, RET_HEAD_DIM, RET_HEAD_DIM), F32)],
        input_output_aliases={11: 0},
        compiler_params=_cparams(("parallel", "arbitrary")),
        name="retention",
    )(p3, p3, p3, p3, cos, sin, dmat, cross, kdec, gn, s0, out_buf)


def _sortable(x):
    bits = pltpu.bitcast(x + 0.0, jnp.int32)
    return bits ^ ((bits >> 31) & 0x7FFFFFFF)


def _dsa_kernel(q_ref, qi_ref, w_ref, k_ref, vt_ref, ki_ref, bn_ref, ok_ref, _buf_ref, o_ref,
                key_ref, keyn_ref, m_ref, den_ref, acc_ref, *, first_real, qb0, skip_below, k_sel):
    j = pl.program_id(1)

    @pl.when(j < skip_below)
    def _():
        o_ref[...] = jnp.zeros(o_ref.shape, o_ref.dtype)

    @pl.when(j >= skip_below)
    def _():
        win0 = pl.multiple_of(j * QT + (qb0 * CHUNK - (WN - QT)), LANES)
        n_far = (win0 + KEY_TILE - 1) // KEY_TILE
        near = pl.ds(win0, WN)

        def index_keys(ki, adm):
            acc = None
            for h in range(0, IDX_HEADS, 2):
                pair = _dot(ki, jnp.concatenate([qi_ref[0, h], qi_ref[0, h + 1]], axis=1))
                term = (w_ref[0, h:h + 1, :] * jnp.maximum(pair[:, :QT], 0.0)
                        + w_ref[0, h + 1:h + 2, :] * jnp.maximum(pair[:, QT:], 0.0))
                acc = term if acc is None else acc + term
            return jnp.where(adm, _sortable(acc), INT_MIN)

        def far_keys(c, carry):
            c0 = pl.multiple_of(c * KEY_TILE, KEY_TILE)
            row = c0 + lax.broadcasted_iota(jnp.int32, (KEY_TILE, 1), 0)
            adm = jnp.logical_and(row >= first_real, row < win0)
            key_ref[c] = index_keys(ki_ref[pl.ds(c0, KEY_TILE), :], adm)
            return carry

        lax.fori_loop(0, n_far, far_keys, 0)
        row_n = win0 + lax.broadcasted_iota(jnp.int32, (WN, 1), 0)
        keyn_ref[...] = index_keys(ki_ref[near, :], jnp.logical_and(row_n >= first_real, ok_ref[...] != 0))

        def fold(hit):
            parts = hit.reshape(hit.shape[0] // SUBLANES, SUBLANES, QT)
            while parts.shape[0] > 1:
                half = parts.reshape(parts.shape[0] // 2, 2, SUBLANES, QT)
                parts = half[:, 0] + half[:, 1]
            return parts[0]

        def count_ge(c):
            part = lax.fori_loop(0, n_far, lambda t, p: p + fold(jnp.where(key_ref[t] >= c, 1.0, 0.0)),
                                 fold(jnp.where(keyn_ref[...] >= c, 1.0, 0.0)))
            return jnp.sum(part, axis=0, keepdims=True)

        zero = jnp.zeros((1, QT), jnp.int32)
        n_zero = count_ge(zero)
        start = (jnp.where(n_zero >= k_sel, zero, INT_MIN), jnp.where(n_zero >= k_sel, n_zero, 0.0))

        def bit_step(it, carry):
            thr, n_at = carry
            cand = thr + jnp.left_shift(jnp.int32(1), jnp.int32(30) - it)
            n_cand = count_ge(cand)
            return jnp.where(n_cand >= k_sel, cand, thr), jnp.where(n_cand >= k_sel, n_cand, n_at)

        thr, n_at = lax.fori_loop(0, 31, bit_step, start)
        thr = jnp.maximum(thr, INT_MIN + 1)

        surplus = jnp.maximum(n_at - k_sel, 0.0)

        @pl.when(jnp.max(surplus) > 0.0)
        def _():
            ties_kept = n_at - count_ge(thr + 1) - surplus
            lower = (lax.broadcasted_iota(jnp.int32, (KEY_TILE, KEY_TILE), 0)
                     >= lax.broadcasted_iota(jnp.int32, (KEY_TILE, KEY_TILE), 1)).astype(BF16)

            def drop_late_ties(key, seen):
                tie = key == thr
                place = seen + _dot(lower[:key.shape[0], :key.shape[0]], jnp.where(tie, 1.0, 0.0).astype(BF16))
                key = jnp.where(jnp.logical_and(tie, place > ties_kept), thr - 1, key)
                return key, place[key.shape[0] - 1:, :]

            def far_ties(t, seen):
                key_ref[t], seen = drop_late_ties(key_ref[t], seen)
                return seen

            seen = lax.fori_loop(0, n_far, far_ties, jnp.zeros((1, QT), F32))
            keyn_ref[...], _ = drop_late_ties(keyn_ref[...], seen)

        m_ref[...] = jnp.full(m_ref.shape, M_INIT, F32)
        den_ref[...] = jnp.zeros(den_ref.shape, F32)
        acc_ref[...] = jnp.zeros(acc_ref.shape, F32)

        heads = range(A_HEADS)
        sls = [slice(h * A_HEAD_DIM, (h + 1) * A_HEAD_DIM) for h in heads]

        def attend(logits, sel, first_vt_tile):
            n_sub = logits[0].shape[0] // LANES
            lg = [jnp.where(sel, logits[h], NEG) for h in heads]
            m_old = [m_ref[h] for h in heads]
            m_new = [jnp.maximum(m_old[h], jnp.max(lg[h], axis=0, keepdims=True)) for h in heads]
            p = [jnp.exp2(lg[h] - m_new[h]) for h in heads]
            alpha = [jnp.exp2(m_old[h] - m_new[h]) for h in heads]
            pb = [p[h].astype(BF16) for h in heads]
            pv = []
            for h in heads:
                terms = [_dot(vt_ref[first_vt_tile + u, sls[h], :], pb[h][u * LANES:(u + 1) * LANES])
                         for u in range(n_sub)]
                pv.append(functools.reduce(lambda a, b: a + b, terms))
            for h in heads:
                den_ref[h] = alpha[h] * den_ref[h] + jnp.sum(p[h], axis=0, keepdims=True)
                acc_ref[h] = alpha[h] * acc_ref[h] + pv[h]
                m_ref[h] = m_new[h]

        def far_tile(c, carry):
            c0 = pl.multiple_of(c * KEY_TILE, KEY_TILE)
            attend([_dot(k_ref[pl.ds(c0, KEY_TILE), sls[h]], q_ref[0, h]) for h in heads],
                   key_ref[c] >= thr, c * (KEY_TILE // LANES))
            return carry

        lax.fori_loop(0, n_far, far_tile, 0)
        attend([_dot(k_ref[near, sls[h]], q_ref[0, h]) + bn_ref[h] for h in heads],
               keyn_ref[...] >= thr, win0 // LANES)
        for h in heads:
            out_t = acc_ref[h] / jnp.maximum(den_ref[h], TINY)
            o_ref[:, sls[h]] = out_t.T[:o_ref.shape[0]].astype(o_ref.dtype)


def _dsa(q_t, qi_t, w_t, k16, v_t, ki16, bias_near, near_ok, out_buf, *, batch, nb, n_keys, out_rows, out_base,
         first_real, qb0, skip_below, k_sel):
    width = k16.shape[1]
    return pl.pallas_call(
        functools.partial(_dsa_kernel, first_real=first_real, qb0=qb0, skip_below=skip_below, k_sel=k_sel),
        out_shape=jax.ShapeDtypeStruct(out_buf.shape, out_buf.dtype),
        grid=(batch, nb),
        in_specs=[
            pl.BlockSpec((1, A_HEADS, A_HEAD_DIM, QT), lambda b, j: (b * nb + j, 0, 0, 0)),
            pl.BlockSpec((1, IDX_HEADS, IDX_DIM, QT), lambda b, j: (b * nb + j, 0, 0, 0)),
            pl.BlockSpec((1, IDX_HEADS, QT), lambda b, j: (b * nb + j, 0, 0)),
            pl.BlockSpec((n_keys, width), lambda b, j: (b, 0)),
            pl.BlockSpec((n_keys // LANES, width, LANES), lambda b, j: (b, 0, 0)),
            pl.BlockSpec((n_keys, IDX_DIM), lambda b, j: (b, 0)),
            pl.BlockSpec((A_HEADS, WN, QT), lambda b, j: (0, 0, 0)),
            pl.BlockSpec((WN, QT), lambda b, j: (0, 0)),
            pl.BlockSpec(memory_space=pl.ANY),
        ],
        out_specs=pl.BlockSpec((out_rows, width), lambda b, j: (out_base + b * nb + j, 0)),
        scratch_shapes=[pltpu.VMEM((n_keys // KEY_TILE, KEY_TILE, QT), jnp.int32),
                        pltpu.VMEM((WN, QT), jnp.int32),
                        pltpu.VMEM((A_HEADS, 1, QT), F32),
                        pltpu.VMEM((A_HEADS, 1, QT), F32),
                        pltpu.VMEM((A_HEADS, A_HEAD_DIM, QT), F32)],
        input_output_aliases={8: 0},
        compiler_params=_cparams(("parallel", "arbitrary")),
        name="dsa",
    )(q_t, qi_t, w_t, k16, v_t, ki16, bias_near, near_ok, out_buf)


def _sample_keys_kernel(ck_ref, cv_ref, cki_ref, nk_ref, nv_ref, nki_ref, k_ref, vt_ref, ki_ref):
    c = pl.program_id(1)
    n_cache = pl.num_programs(1) - 1
    rows = k_ref.shape[0]

    @pl.when(c < n_cache)
    def _():
        for h in range(A_HEADS):
            sl = slice(h * A_HEAD_DIM, (h + 1) * A_HEAD_DIM)
            head_rows = pl.ds(h, rows, stride=A_HEADS)
            k_ref[:, sl] = ck_ref[0, head_rows, :].astype(BF16)
            vh = cv_ref[0, head_rows, :]
            for u in range(rows // LANES):
                vt_ref[u, sl, :] = vh[u * LANES:(u + 1) * LANES, :].T.astype(BF16)
        ki_ref[...] = cki_ref[0].astype(BF16)

    @pl.when(c == n_cache)
    def _():
        k_ref[...] = jnp.zeros(k_ref.shape, BF16)
        vt_ref[...] = jnp.zeros(vt_ref.shape, BF16)
        ki_ref[...] = jnp.zeros(ki_ref.shape, BF16)
        k_ref[0:CHUNK, :] = nk_ref[...].astype(BF16)
        ki_ref[0:CHUNK, :] = nki_ref[:, :IDX_DIM].astype(BF16)
        pad = jnp.zeros((LANES - CHUNK, A_HEAD_DIM), F32)
        for h in range(A_HEADS):
            sl = slice(h * A_HEAD_DIM, (h + 1) * A_HEAD_DIM)
            vt_ref[0, sl, :] = jnp.concatenate([nv_ref[:, sl], pad], axis=0).T.astype(BF16)


def _sample_keys(l, cache_k, cache_v, cache_ki, k_new, v_new, ki_new, *, new_row0, n_keys, rows=512):
    _, bs, past = cache_k.shape[:3]
    width = k_new.shape[1]
    n_cache = past // rows
    nb = -(-n_keys // rows)
    new_blk = new_row0 // CHUNK

    def cached(block_rows, w):
        return pl.BlockSpec((None, 1, block_rows, w), lambda b, c: (l, b, jnp.minimum(c, n_cache - 1), 0))

    def new(w):
        return pl.BlockSpec((CHUNK, w), lambda b, c: (new_blk + b, 0))

    kv_shape = cache_k.shape[:2] + (past * A_HEADS, A_HEAD_DIM)

    k16, v_t, ki16 = pl.pallas_call(
        _sample_keys_kernel,
        out_shape=(jax.ShapeDtypeStruct((bs, n_keys, width), BF16),
                   jax.ShapeDtypeStruct((bs, n_keys // LANES, width, LANES), BF16),
                   jax.ShapeDtypeStruct((bs, n_keys, IDX_DIM), BF16)),
        grid=(bs, nb),
        in_specs=[cached(rows * A_HEADS, A_HEAD_DIM), cached(rows * A_HEADS, A_HEAD_DIM), cached(rows, IDX_DIM),
                  new(width), new(width), new(LANES)],
        out_specs=(pl.BlockSpec((None, rows, width), lambda b, c: (b, c, 0)),
                   pl.BlockSpec((None, rows // LANES, width, LANES), lambda b, c: (b, c, 0, 0)),
                   pl.BlockSpec((None, rows, IDX_DIM), lambda b, c: (b, c, 0))),
        compiler_params=_cparams(("parallel", "arbitrary")),
        name="sample_keys",
    )(cache_k.reshape(kv_shape), cache_v.reshape(kv_shape), cache_ki, k_new, v_new, ki_new)
    return (k16.reshape(bs * n_keys, width), v_t.reshape(bs * (n_keys // LANES), width, LANES),
            ki16.reshape(bs * n_keys, IDX_DIM))


def _t5_bucket(rel):
    half = T5_BUCKETS // 2
    exact = half // 2
    n = jnp.abs(rel)
    large = exact + (jnp.log(jnp.maximum(n, 1).astype(F32) / exact)
                     / math.log(T5_MAX_DIST / exact) * (half - exact)).astype(jnp.int32)
    large = jnp.minimum(large, half - 1)
    return jnp.where(rel > 0, half, 0) + jnp.where(n < exact, n, large)


def _bias_tables(t5_bias):
    a = jnp.arange(WN, dtype=jnp.int32)[:, None]
    t = jnp.arange(QT, dtype=jnp.int32)[None, :]
    bucket = _t5_bucket(a - (WN - QT) - t)
    onehot = (bucket[:, :, None] == jnp.arange(T5_BUCKETS, dtype=jnp.int32)).astype(F32)
    near = jnp.einsum("atk,kh->hat", onehot, t5_bias.astype(F32), precision=lax.Precision.HIGHEST)
    far = t5_bias[_t5_bucket(jnp.int32(-2 * CHUNK - 1))].astype(F32)
    ok = (a // CHUNK - (WN - QT) // CHUNK <= t // CHUNK).astype(jnp.int32)
    return (near - far[:, None, None]) * math.log2(math.e), ok


def _rope_tables(pos):
    half = RET_HEAD_DIM // 2
    inv = ROPE_BASE ** (-jnp.arange(half, dtype=F32) / half)
    ang = pos.astype(F32)[:, None] * inv[None, :]
    cos, sin = jnp.cos(ang), jnp.sin(ang)
    return jnp.concatenate([cos, cos], axis=-1), jnp.concatenate([-sin, sin], axis=-1)


def _decay_tables():
    n = CHUNK
    log_g = jnp.log(1.0 - 2.0 ** (-5.0 - jnp.arange(RET_HEADS, dtype=F32)))
    i = jnp.arange(n, dtype=F32)
    diff = i[:, None] - i[None, :]
    dmat = jnp.where(diff >= 0, jnp.exp(jnp.maximum(diff, 0.0)[None] * log_g[:, None, None]), 0.0)
    cross = jnp.exp((i[None, :] + 1.0) * log_g[:, None])
    kdec = jnp.exp((n - 1.0 - i)[None, :] * log_g[:, None])
    gn = jnp.exp(n * log_g)
    wide = (RET_HEADS, n, RET_HEAD_DIM)
    return (dmat, jnp.broadcast_to(cross[:, :, None], wide), jnp.broadcast_to(kdec[:, :, None], wide),
            jnp.broadcast_to(gn[:, None, None], (RET_HEADS, 1, RET_HEAD_DIM)))


def kernel(x_prompt, x_sample, cache_k, cache_v, cache_ki, cache_pool, state_ret, meta_tokens, ln_in_g, ln_in_b, w_in, t5_bias, pool_w, pool_scale, w_branch, w_gate, b_gate, w_out, ln1_g, ln1_b, ln2_g, ln2_b, ffn_w_gate, ffn_w_up, ffn_w_down, moe_w_router, moe_b_router, moe_w_gate, moe_w_up, moe_w_down):
    bp, seq, d = x_prompt.shape
    bs, ts, _ = x_sample.shape
    past = cache_k.shape[2]
    t_real = seq + N_META
    tp = -(-(t_real + 2 * CHUNK) // KEY_TILE) * KEY_TILE
    front = tp - t_real
    assert front % CHUNK == CHUNK - N_META and ts == CHUNK and past % CHUNK == 0
    mp, ms = bp * tp, bs * ts
    m = mp + ms
    assert mp % TM == 0 and ms % TM == 0
    ksel_p = min(TOPK_MAX, seq // 4)
    ksel_s = min(TOPK_MAX, (past + ts) // 4)
    n_keys_s = -(-(past + QT) // KEY_TILE) * KEY_TILE
    assert tp % QT == 0 and front >= WN - QT and past % QT == 0 and past + QT >= WN

    assert front + N_META == LN_TILE and seq % LN_TILE == 0 and ms % LN_TILE == 0
    head = jnp.concatenate([jnp.zeros((front, d), F32), meta_tokens.astype(F32)], axis=0)
    x32, xb = _input_norm(head, x_prompt, x_sample.reshape(ms, d), ln_in_g, ln_in_b)

    bias_near, near_ok = _bias_tables(t5_bias)
    cos_p, sin_p = _rope_tables(jnp.arange(tp, dtype=jnp.int32) - (front + N_META))
    cos_s, sin_s = _rope_tables(past + jnp.arange(ts, dtype=jnp.int32))
    decay = _decay_tables()
    zero_state = jnp.zeros((bp, RET_HEADS, RET_HEAD_DIM, RET_HEAD_DIM), F32)
    w_in_t = jnp.swapaxes(w_in, 1, 2)
    assert TAIL_OFF % SUBLANES == 0
    row_in_seq = jnp.arange(mp, dtype=jnp.int32) % tp
    token_rows = jnp.concatenate([row_in_seq >= front, jnp.ones((ms,), jnp.bool_)])

    outs = {name: [] for name in ("kp", "vp", "kip", "poolp", "retp", "ks", "vs", "kis", "pools", "rets")}
    bw = BRANCH_WIDTH
    for l in range(DEPTH):
        q16, q_t = _proj(xb, w_in_t, (l,), 0, nat16=True, trans=True, out_scale=ATTN_LOG2_SCALE, name="proj_q")
        k_new, k16 = _proj(xb, w_in_t, (l,), 1, nat32=True, nat16=True, name="proj_k")
        v_new, v_t = _proj(xb, w_in_t, (l,), 2, nat32=True, trans=True, name="proj_v")
        qi16, qi_t = _proj(xb, w_in_t, (l,), 3, nat16=True, trans=True, name="proj_qi")
        p2, ki16, w_t = _proj_idx(xb, w_in_t, (l,), 4 * bw // LANES)
        p3 = _matmul(xb, w_in_t, (l,), TAIL_OFF, 5, bw, name="proj_tail", w_transposed=True)
        ki_new = p2[:, :IDX_DIM]
        qi_t = qi_t.reshape(m // LANES, IDX_HEADS, IDX_DIM, LANES)
        v_t = v_t.reshape(m // LANES, bw, LANES)

        def sample_queries_t(a16, heads):
            a = jnp.pad(a16[mp:].reshape(bs, ts, heads, bw // heads), ((0, 0), (0, QT - ts), (0, 0), (0, 0)))
            return jnp.transpose(a, (0, 2, 3, 1))

        w_t_s = jnp.transpose(jnp.pad(p2[mp:, IDX_DIM:IDX_DIM + IDX_HEADS].reshape(bs, ts, IDX_HEADS),
                                      ((0, 0), (0, QT - ts), (0, 0))), (0, 2, 1))
        k16_s, v_t_s, ki16_s = _sample_keys(l, cache_k, cache_v, cache_ki, k_new, v_new, p2, new_row0=mp,
                                            n_keys=n_keys_s)

        branch_buf = jnp.zeros((m, bw), BF16)
        oa = _dsa(q_t, qi_t, w_t, k16, v_t, ki16, bias_near, near_ok, branch_buf, batch=bp, nb=tp // QT, n_keys=tp,
                  out_rows=QT, out_base=0, first_real=front, qb0=0, skip_below=max(1, front // QT), k_sel=ksel_p)
        oa = _dsa(sample_queries_t(q16, A_HEADS), sample_queries_t(qi16, IDX_HEADS), w_t_s, k16_s, v_t_s, ki16_s,
                  bias_near, near_ok, oa, batch=bs, nb=1, n_keys=n_keys_s, out_rows=ts, out_base=mp // ts,
                  first_real=0, qb0=past // CHUNK, skip_below=0, k_sel=ksel_s)

        tb_p = 256
        ob = _pool(p3, 0, p3, 0, pool_w[l], pool_scale[l], branch_buf, batch=bp, rows=tp, tb=tb_p, row0=0,
                   halo_map=lambda b, i: jnp.maximum((b * tp + i * tb_p) // HALO - 1, 0), first_valid=front)
        pool_hist = jnp.pad(cache_pool[l], ((0, 0), (HALO - POOL_PAST, 0), (0, 0))).reshape(bs * HALO, bw)
        ob = _pool(p3, 0, pool_hist, 0, pool_w[l], pool_scale[l], ob, batch=bs, rows=ts, tb=ts, row0=mp,
                   halo_map=lambda b, i: b, first_valid=-POOL_PAST)

        oc, ret_p = _retention(p3, zero_state, cos_p, sin_p, decay, branch_buf, batch=bp, rows=tp, row0=0,
                               first_valid=front)
        oc, ret_s = _retention(p3, state_ret[l].astype(F32), cos_s, sin_s, decay, oc, batch=bs, rows=ts, row0=mp,
                               first_valid=0)

        merged = _merge(l, xb, oa, ob, oc, w_branch, w_gate, b_gate)
        x32, xb = _matmul_norm(merged, w_out, (l,), x32, ln1_g[l], ln1_b[l])

        if l % 2 == 0:
            h = _gate_up(xb, ffn_w_gate, ffn_w_up, (l // 2,))
            f = _matmul(h, ffn_w_down, (l // 2,), 0, d // 512, 512, name="ffn_down")
        else:
            gates, experts = _router(x32, moe_w_router[l // 2], moe_b_router[l // 2])
            src, pos, tile_expert, n_used = _dispatch_plan(experts, MOE_TILE, token_rows)
            x_sorted = _gather_rows(x32, src, n_used, MOE_TILE)
            y_sorted = _moe_experts(x_sorted, tile_expert, n_used, moe_w_gate, moe_w_up, moe_w_down, l // 2, MOE_TILE)
            f = _combine_rows(y_sorted, pos, gates, MOE_TILE)
        if l + 1 < DEPTH:
            x32, xb = _layer_norm(x32, f, ln2_g[l], ln2_b[l])
        else:
            y_prompt, y_sample = _output_norm(x32, f, ln2_g[l], ln2_b[l], bp, seq)

        def prompt_rows(a, width):
            return a[:mp].reshape(bp, tp, width)[:, front:]

        u = p3[:, :bw]
        outs["kp"].append(prompt_rows(k_new, bw).reshape(bp, t_real, A_HEADS, A_HEAD_DIM))
        outs["vp"].append(prompt_rows(v_new, bw).reshape(bp, t_real, A_HEADS, A_HEAD_DIM))
        outs["kip"].append(prompt_rows(ki_new, IDX_DIM))
        outs["poolp"].append(prompt_rows(u, bw)[:, -POOL_PAST:])
        outs["retp"].append(ret_p)
        outs["ks"].append(k_new[mp:].reshape(bs, ts, A_HEADS, A_HEAD_DIM))
        outs["vs"].append(v_new[mp:].reshape(bs, ts, A_HEADS, A_HEAD_DIM))
        outs["kis"].append(ki_new[mp:].reshape(bs, ts, IDX_DIM))
        outs["pools"].append(u[mp:].reshape(bs, ts, bw)[:, -POOL_PAST:])
        outs["rets"].append(ret_s)

    return (y_prompt, y_sample.reshape(bs, ts, d)) + tuple(
        jnp.stack(outs[name]) for name in ("kp", "vp", "kip", "poolp", "retp", "ks", "vs", "kis", "pools", "rets"))
```
